```python
import jax
import jax.numpy as jnp
from jax import lax
import numpy as np

D_MODEL = 2048
BATCH = 4
SEQ = 4096
DEPTH = 4

N_MIXERS = 4
MEM_LEN = 256
HEAD_DIM = 128
MEM_HEADS = 4
MEM_WIDTH = MEM_HEADS * HEAD_DIM
MIX_WIDTH = D_MODEL - MEM_WIDTH
QB = 128
D_FF = -(-(8 * D_MODEL) // (3 * 256)) * 256
EPS = 1e-6
NEG_INF = -1e30

A_HEADS = MIX_WIDTH // HEAD_DIM
A_KV = 2
A_CMP_LEN = 32
A_CMP_STRIDE = 16
A_CMP_HID = 2 * HEAD_DIM
A_SEL_LEN = 64
A_TOP_N = 16
A_WINDOW = 512
A_SEL_QC = 64
A_FORCE = 1e6
A_EXCLUDE = -1e9

B_HEADS = MIX_WIDTH // HEAD_DIM

C_HEADS = 4
C_HEAD_DIM = MIX_WIDTH // C_HEADS
C_CONV = 4
C_CHUNK = 64

D_HEAD_DIM = 64
D_HEADS = MIX_WIDTH // D_HEAD_DIM
D_KV = D_HEADS // 8
D_WINDOW = 128

W_IN_A = A_HEADS * HEAD_DIM + 6 * A_KV * HEAD_DIM + 3 * A_HEADS + MEM_WIDTH
W_IN_B = 3 * MIX_WIDTH + MEM_WIDTH
W_IN_C = 4 * MIX_WIDTH + 2 * C_HEADS + MEM_WIDTH
W_IN_D = D_HEADS * D_HEAD_DIM + 2 * D_KV * D_HEAD_DIM + MEM_WIDTH

kernel_name = 'hybrid_nsa_stickbreak_mlstm_swa_decoder'


def rms_norm(x, g):
    xf = x.astype(jnp.float32)
    y = xf * lax.rsqrt(jnp.mean(xf * xf, axis=-1, keepdims=True) + EPS)
    return (y * g.astype(jnp.float32)).astype(x.dtype)


def split_cols(u, sizes):
    idx, acc = [], 0
    for s in sizes[:-1]:
        acc += s
        idx.append(acc)
    return jnp.split(u, idx, axis=-1)


def banded_attention(q, k, v, window, sinks=None):
    B, S, H, dh = q.shape
    G = k.shape[2]
    hpg = H // G
    nb = S // QB
    kw = window + QB
    kp = jnp.pad(k, ((0, 0), (window, 0), (0, 0), (0, 0)))
    vp = jnp.pad(v, ((0, 0), (window, 0), (0, 0), (0, 0)))
    qb = q.reshape(B, nb, QB, G, hpg, dh).transpose(1, 0, 2, 3, 4, 5)
    scale = dh ** -0.5

    def block(args):
        i, qi = args
        start = i * QB
        ki = lax.dynamic_slice_in_dim(kp, start, kw, axis=1)
        vi = lax.dynamic_slice_in_dim(vp, start, kw, axis=1)
        s = jnp.einsum('bqghd,bkgd->bghqk', qi, ki).astype(jnp.float32) * scale
        qpos = start + jnp.arange(QB)
        kpos = start - window + jnp.arange(kw)
        rel = qpos[:, None] - kpos[None, :]
        mask = (rel >= 0) & (rel < window) & (kpos[None, :] >= 0)
        s = jnp.where(mask, s, NEG_INF)
        if sinks is None:
            p = jax.nn.softmax(s, axis=-1)
        else:
            sk = sinks.astype(jnp.float32).reshape(G, hpg)[None, :, :, None, None]
            m = jnp.maximum(s.max(axis=-1, keepdims=True), sk)
            e = jnp.exp(s - m)
            p = e / (e.sum(axis=-1, keepdims=True) + jnp.exp(sk - m))
        return jnp.einsum('bghqk,bkgd->bqghd', p.astype(vi.dtype), vi)

    out = lax.map(block, (jnp.arange(nb), qb))
    return out.transpose(1, 0, 2, 3, 4, 5).reshape(B, S, H, dh)


def compress_blocks(blocks, pe, w1, w2):
    z = blocks + pe[:, None, :]
    z = jnp.moveaxis(z, 3, 2)
    z = z.reshape(z.shape[:3] + (-1,))
    return jax.nn.gelu(z @ w1) @ w2


def nsa_mixer(u, pe_k, w1_k, w2_k, pe_v, w1_v, w2_v):
    B, S, _ = u.shape
    H, G, dh = A_HEADS, A_KV, HEAD_DIM
    hpg = H // G
    scale = dh ** -0.5
    q, kv, gate = split_cols(u, [H * dh, 6 * G * dh, 3 * H])
    q = q.reshape(B, S, H, dh)
    kv = kv.reshape(B, S, 6, G, dh)
    k_cmp, v_cmp, k_slc, v_slc, k_win, v_win = [kv[:, :, j] for j in range(6)]
    qg = q.reshape(B, S, G, hpg, dh)
    pos = jnp.arange(S)

    n_cmp = (S - A_CMP_LEN) // A_CMP_STRIDE + 1
    starts = jnp.arange(n_cmp) * A_CMP_STRIDE
    blk_idx = starts[:, None] + jnp.arange(A_CMP_LEN)[None, :]
    kc = compress_blocks(k_cmp[:, blk_idx], pe_k, w1_k, w2_k)
    vc = compress_blocks(v_cmp[:, blk_idx], pe_v, w1_v, w2_v)
    s = jnp.einsum('bsghd,bngd->bghsn', qg, kc).astype(jnp.float32) * scale
    cmask = (starts[None, :] + A_CMP_LEN - 1) <= pos[:, None]
    p_cmp = jax.nn.softmax(jnp.where(cmask, s, NEG_INF), axis=-1) * cmask
    o_cmp = jnp.einsum('bghsn,bngd->bsghd', p_cmp.astype(vc.dtype), vc)

    n_slc = S // A_SEL_LEN
    blk = jnp.arange(n_slc)
    overlap = ((starts[:, None] < (blk[None, :] + 1) * A_SEL_LEN)
               & (starts[:, None] + A_CMP_LEN > blk[None, :] * A_SEL_LEN)).astype(jnp.float32)
    imp = jnp.einsum('bghsn,nj->bgsj', p_cmp, overlap)
    cur = pos // A_SEL_LEN
    valid = blk[None, :] <= cur[:, None]
    forced = (blk[None, :] == 0) | (blk[None, :] == cur[:, None]) | (blk[None, :] == cur[:, None] - 1)
    score = jnp.where(valid, jnp.where(forced, A_FORCE, imp), A_EXCLUDE)
    n_sel = min(A_TOP_N, n_slc)
    sel_val, sel_idx = lax.top_k(score, n_sel)
    sel_ok = sel_val > 0.5 * A_EXCLUDE

    kb = k_slc.reshape(B, n_slc, A_SEL_LEN, G, dh).transpose(0, 3, 1, 2, 4)
    vb = v_slc.reshape(B, n_slc, A_SEL_LEN, G, dh).transpose(0, 3, 1, 2, 4)
    nqc = S // A_SEL_QC

    def to_chunks(a):
        a = a.reshape(a.shape[:2] + (nqc, A_SEL_QC) + a.shape[3:])
        return jnp.moveaxis(a, 2, 0)

    q_sel = to_chunks(qg.transpose(0, 2, 1, 3, 4))
    idx_c = to_chunks(sel_idx)
    ok_c = to_chunks(sel_ok)
    bi = jnp.arange(B)[:, None, None, None]
    gi = jnp.arange(G)[None, :, None, None]

    def sel_block(args):
        c, qc, ic, okc = args
        kg = kb[bi, gi, ic]
        vg = vb[bi, gi, ic]
        sc = jnp.einsum('bgqhd,bgqnld->bgqhnl', qc, kg).astype(jnp.float32) * scale
        qpos = c * A_SEL_QC + jnp.arange(A_SEL_QC)
        kpos = ic[..., None] * A_SEL_LEN + jnp.arange(A_SEL_LEN)
        m = okc[..., None] & (kpos <= qpos[None, None, :, None, None])
        sc = jnp.where(m[:, :, :, None], sc, NEG_INF)
        sh = sc.shape
        p = jax.nn.softmax(sc.reshape(sh[:4] + (-1,)), axis=-1).reshape(sh)
        return jnp.einsum('bgqhnl,bgqnld->bgqhd', p.astype(vg.dtype), vg)

    o_slc = lax.map(sel_block, (jnp.arange(nqc), q_sel, idx_c, ok_c))
    o_slc = jnp.moveaxis(o_slc, 0, 2).reshape(B, G, S, hpg, dh).transpose(0, 2, 1, 3, 4)

    o_win = banded_attention(q, k_win, v_win, A_WINDOW).reshape(B, S, G, hpg, dh)

    g = jax.nn.sigmoid(gate.astype(jnp.float32)).reshape(B, S, G, hpg, 3)
    o = g[..., 0:1] * o_cmp + g[..., 1:2] * o_slc + g[..., 2:3] * o_win
    return o.reshape(B, S, H * dh).astype(u.dtype)


def stick_breaking_attention(q, k, v):
    B, S, H, dh = q.shape
    nb = S // QB
    scale = dh ** -0.5
    qb = q.reshape(B, nb, QB, H, dh).transpose(1, 0, 2, 3, 4)
    kpos = jnp.arange(S)

    def block(args):
        i, qi = args
        z = jnp.einsum('bqhd,bkhd->bhqk', qi, k).astype(jnp.float32) * scale
        qpos = i * QB + jnp.arange(QB)
        causal = kpos[None, :] < qpos[:, None]
        log_keep = jnp.where(causal, jax.nn.log_sigmoid(-z), 0.0)
        after = lax.cumsum(log_keep, axis=3, reverse=True) - log_keep
        a = jnp.where(causal, jnp.exp(jax.nn.log_sigmoid(z) + after), 0.0)
        return jnp.einsum('bhqk,bkhd->bqhd', a.astype(v.dtype), v)

    out = lax.map(block, (jnp.arange(nb), qb))
    return out.transpose(1, 0, 2, 3, 4).reshape(B, S, H, dh)


def stick_breaking_mixer(u):
    B, S, _ = u.shape
    q, k, v = split_cols(u, [MIX_WIDTH, MIX_WIDTH, MIX_WIDTH])
    shp = (B, S, B_HEADS, HEAD_DIM)
    o = stick_breaking_attention(q.reshape(shp), k.reshape(shp), v.reshape(shp))
    return o.reshape(B, S, MIX_WIDTH).astype(u.dtype)


def causal_depthwise_conv(x, w, b):
    c = x.shape[-1]
    kk = w.shape[0]
    y = lax.conv_general_dilated(x, w[:, None, :].astype(x.dtype), window_strides=(1,),
                                 padding=[(kk - 1, 0)], dimension_numbers=('NWC', 'WIO', 'NWC'),
                                 feature_group_count=c)
    return y + b.astype(x.dtype)


def mlstm_chunkwise(q, k, v, i_g, log_f):
    B, S, H, dh = q.shape
    L = C_CHUNK
    nc = S // L

    def chunks(a):
        a = a.reshape((B, nc, L) + a.shape[2:])
        a = jnp.moveaxis(a, 1, 0)
        return jnp.moveaxis(a, 3, 2)

    tril = jnp.tril(jnp.ones((L, L), dtype=bool))

    def step(carry, xs):
        Cm, nv, m = carry
        qc, kc, vc, ic, fc = xs
        b = jnp.cumsum(fc, axis=-1)
        Dm = jnp.where(tril, b[..., :, None] - b[..., None, :] + ic[..., None, :], NEG_INF)
        m_inter = b + m[..., None]
        m_t = jnp.maximum(m_inter, Dm.max(axis=-1))
        w = jnp.exp(Dm - m_t[..., None])
        a_inter = jnp.exp(m_inter - m_t)
        sqk = jnp.einsum('bhtd,bhsd->bhts', qc, kc) * w
        num = (a_inter[..., None] * jnp.einsum('bhvd,bhtd->bhtv', Cm, qc)
               + jnp.einsum('bhts,bhsv->bhtv', sqk, vc))
        den = a_inter * jnp.einsum('bhd,bhtd->bht', nv, qc) + sqk.sum(axis=-1)
        den = jnp.maximum(jnp.abs(den), jnp.exp(-m_t))
        h = num / den[..., None]
        bL = b[..., -1]
        g = bL[..., None] - b + ic
        m_new = jnp.maximum(bL + m, g.max(axis=-1))
        wk = jnp.exp(g - m_new[..., None])
        decay = jnp.exp(bL + m - m_new)
        Cm = decay[..., None, None] * Cm + jnp.einsum('bhs,bhsv,bhsd->bhvd', wk, vc, kc)
        nv = decay[..., None] * nv + jnp.einsum('bhs,bhsd->bhd', wk, kc)
        return (Cm, nv, m_new), h

    init = (jnp.zeros((B, H, dh, dh), jnp.float32), jnp.zeros((B, H, dh), jnp.float32),
            jnp.zeros((B, H), jnp.float32))
    _, hs = lax.scan(step, init, (chunks(q), chunks(k), chunks(v), chunks(i_g), chunks(log_f)))
    return hs.transpose(1, 0, 3, 2, 4).reshape(B, S, H, dh)


def mlstm_mixer(u, conv_w, conv_b, gate_b, head_norm):
    B, S, _ = u.shape
    H, dh = C_HEADS, C_HEAD_DIM
    q_pre, k_pre, v, o_pre, i_pre, f_pre = split_cols(
        u, [MIX_WIDTH, MIX_WIDTH, MIX_WIDTH, MIX_WIDTH, H, H])
    qk = jax.nn.silu(causal_depthwise_conv(jnp.concatenate([q_pre, k_pre], axis=-1), conv_w, conv_b))
    q, k = jnp.split(qk.astype(jnp.float32), 2, axis=-1)
    q = q.reshape(B, S, H, dh)
    k = k.reshape(B, S, H, dh) * (dh ** -0.5)
    v = v.astype(jnp.float32).reshape(B, S, H, dh)
    gb = gate_b.astype(jnp.float32)
    i_g = i_pre.astype(jnp.float32) + gb[:H]
    log_f = jax.nn.log_sigmoid(f_pre.astype(jnp.float32) + gb[H:])
    h = mlstm_chunkwise(q, k, v, i_g, log_f)
    h = jax.nn.sigmoid(o_pre.astype(jnp.float32)).reshape(B, S, H, dh) * h
    h = h * lax.rsqrt(jnp.mean(h * h, axis=-1, keepdims=True) + EPS)
    h = h * head_norm.astype(jnp.float32).reshape(H, dh)
    return h.reshape(B, S, H * dh).astype(u.dtype)


def sink_window_mixer(u, sinks):
    B, S, _ = u.shape
    q, k, v = split_cols(u, [D_HEADS * D_HEAD_DIM, D_KV * D_HEAD_DIM, D_KV * D_HEAD_DIM])
    o = banded_attention(q.reshape(B, S, D_HEADS, D_HEAD_DIM), k.reshape(B, S, D_KV, D_HEAD_DIM),
                         v.reshape(B, S, D_KV, D_HEAD_DIM), D_WINDOW, sinks)
    return o.reshape(B, S, MIX_WIDTH).astype(u.dtype)


def memory_attention(qm, mem_k, mem_v):
    B, S, _ = qm.shape
    q = qm.reshape(B, S, MEM_HEADS, HEAD_DIM)
    s = jnp.einsum('bshd,bmhd->bhsm', q, mem_k).astype(jnp.float32) * (HEAD_DIM ** -0.5)
    p = jax.nn.softmax(s, axis=-1)
    o = jnp.einsum('bhsm,bmhd->bshd', p.astype(mem_v.dtype), mem_v)
    return o.reshape(B, S, MEM_WIDTH)


def setup_inputs(seed: int = 0) -> dict:
    key = jax.random.key(seed)
    ks = iter(jax.random.split(key, 64))
    f32 = jnp.float32

    def nrm(shape, scale):
        return jax.random.normal(next(ks), shape, f32) * scale

    def gain(n):
        return 1.0 + 0.02 * jax.random.normal(next(ks), (n,), f32)

    d = D_MODEL
    inp = {}
    inp['x'] = nrm((BATCH, SEQ, d), 1.0)
    inp['mem'] = nrm((BATCH, MEM_LEN, d), 1.0)
    inp['mem_norm'] = gain(d)
    inp['mem_w_kv'] = nrm((d, 2 * MEM_WIDTH), d ** -0.5)

    def ffn(prefix):
        inp[prefix + 'norm_ffn'] = gain(d)
        inp[prefix + 'w_gate'] = nrm((d, D_FF), d ** -0.5)
        inp[prefix + 'w_up'] = nrm((d, D_FF), d ** -0.5)
        inp[prefix + 'w_down'] = nrm((D_FF, d), D_FF ** -0.5)

    cmp_in = A_CMP_LEN * HEAD_DIM
    inp['l0_norm_mix'] = gain(d)
    inp['l0_w_in'] = nrm((d, W_IN_A), d ** -0.5)
    inp['l0_cmp_pe_k'] = nrm((A_CMP_LEN, HEAD_DIM), 0.1)
    inp['l0_cmp_w1_k'] = nrm((cmp_in, A_CMP_HID), cmp_in ** -0.5)
    inp['l0_cmp_w2_k'] = nrm((A_CMP_HID, HEAD_DIM), A_CMP_HID ** -0.5)
    inp['l0_cmp_pe_v'] = nrm((A_CMP_LEN, HEAD_DIM), 0.1)
    inp['l0_cmp_w1_v'] = nrm((cmp_in, A_CMP_HID), cmp_in ** -0.5)
    inp['l0_cmp_w2_v'] = nrm((A_CMP_HID, HEAD_DIM), A_CMP_HID ** -0.5)
    inp['l0_w_out'] = nrm((d, d), d ** -0.5)
    ffn('l0_')

    inp['l1_norm_mix'] = gain(d)
    inp['l1_w_in'] = nrm((d, W_IN_B), d ** -0.5)
    inp['l1_w_out'] = nrm((d, d), d ** -0.5)
    ffn('l1_')

    inp['l2_norm_mix'] = gain(d)
    inp['l2_w_in'] = nrm((d, W_IN_C), d ** -0.5)
    inp['l2_conv_w'] = nrm((C_CONV, 2 * MIX_WIDTH), C_CONV ** -0.5)
    inp['l2_conv_b'] = nrm((2 * MIX_WIDTH,), 0.02)
    inp['l2_gate_b'] = jnp.concatenate([nrm((C_HEADS,), 0.1),
                                        jnp.linspace(3.0, 6.0, C_HEADS, dtype=f32) + nrm((C_HEADS,), 0.1)])
    inp['l2_head_norm'] = gain(MIX_WIDTH)
    inp['l2_w_out'] = nrm((d, d), d ** -0.5)
    ffn('l2_')

    inp['l3_norm_mix'] = gain(d)
    inp['l3_w_in'] = nrm((d, W_IN_D), d ** -0.5)
    inp['l3_sinks'] = nrm((D_HEADS,), 0.5)
    inp['l3_w_out'] = nrm((d, d), d ** -0.5)
    ffn('l3_')

    inp['final_norm'] = gain(d)
    return inp


def reference(x, mem, mem_norm, mem_w_kv,
              l0_norm_mix, l0_w_in, l0_cmp_pe_k, l0_cmp_w1_k, l0_cmp_w2_k, l0_cmp_pe_v, l0_cmp_w1_v,
              l0_cmp_w2_v, l0_w_out, l0_norm_ffn, l0_w_gate, l0_w_up, l0_w_down,
              l1_norm_mix, l1_w_in, l1_w_out, l1_norm_ffn, l1_w_gate, l1_w_up, l1_w_down,
              l2_norm_mix, l2_w_in, l2_conv_w, l2_conv_b, l2_gate_b, l2_head_norm, l2_w_out,
              l2_norm_ffn, l2_w_gate, l2_w_up, l2_w_down,
              l3_norm_mix, l3_w_in, l3_sinks, l3_w_out, l3_norm_ffn, l3_w_gate, l3_w_up, l3_w_down,
              final_norm):
    B, M, _ = mem.shape
    mem_kv = rms_norm(mem, mem_norm) @ mem_w_kv
    mem_k, mem_v = jnp.split(mem_kv, 2, axis=-1)
    mem_k = mem_k.reshape(B, M, MEM_HEADS, HEAD_DIM)
    mem_v = mem_v.reshape(B, M, MEM_HEADS, HEAD_DIM)

    mixers = (nsa_mixer, stick_breaking_mixer, mlstm_mixer, sink_window_mixer)
    layers = (
        (l0_norm_mix, l0_w_in, l0_w_out, l0_norm_ffn, l0_w_gate, l0_w_up, l0_w_down,
         (l0_cmp_pe_k, l0_cmp_w1_k, l0_cmp_w2_k, l0_cmp_pe_v, l0_cmp_w1_v, l0_cmp_w2_v)),
        (l1_norm_mix, l1_w_in, l1_w_out, l1_norm_ffn, l1_w_gate, l1_w_up, l1_w_down, ()),
        (l2_norm_mix, l2_w_in, l2_w_out, l2_norm_ffn, l2_w_gate, l2_w_up, l2_w_down,
         (l2_conv_w, l2_conv_b, l2_gate_b, l2_head_norm)),
        (l3_norm_mix, l3_w_in, l3_w_out, l3_norm_ffn, l3_w_gate, l3_w_up, l3_w_down, (l3_sinks,)),
    )
    for i in range(DEPTH):
        norm_mix, w_in, w_out, norm_ffn, w_gate, w_up, w_down, extra = layers[i]
        h = rms_norm(x, norm_mix)
        u = h @ w_in
        mixed = mixers[i % N_MIXERS](u[..., :-MEM_WIDTH], *extra)
        mem_out = memory_attention(u[..., -MEM_WIDTH:], mem_k, mem_v)
        x = x + jnp.concatenate([mixed, mem_out], axis=-1) @ w_out
        h = rms_norm(x, norm_ffn)
        x = x + (jax.nn.silu(h @ w_gate) * (h @ w_up)) @ w_down
    return rms_norm(x, final_norm)
```

```python
import functools

import jax
import jax.numpy as jnp
from jax import lax
from jax.experimental import pallas as pl
from jax.experimental.pallas import tpu as pltpu

F32 = jnp.float32
BF16 = jnp.bfloat16

LANES = 128
VMEM_LIMIT = 56 * 1024 * 1024

D_MODEL = 2048
HEAD_DIM = 128
MEM_HEADS = 4
MEM_WIDTH = MEM_HEADS * HEAD_DIM
MIX_WIDTH = D_MODEL - MEM_WIDTH
EPS = 1e-6
NEG_INF = -1e30

A_HEADS = MIX_WIDTH // HEAD_DIM
A_KV = 2
A_HPG = A_HEADS // A_KV
A_CMP_LEN = 32
A_CMP_STRIDE = 16
A_SEL_LEN = 64
A_TOP_N = 16
A_WINDOW = 512
A_FORCE = 1e6
A_EXCLUDE = -1e9

B_HEADS = MIX_WIDTH // HEAD_DIM

C_HEADS = 4
C_HEAD_DIM = MIX_WIDTH // C_HEADS
C_CONV = 4
C_CHUNK = 64

D_HEAD_DIM = 64
D_HEADS = MIX_WIDTH // D_HEAD_DIM
D_KV = D_HEADS // 8
D_HPG = D_HEADS // D_KV
D_WINDOW = 128


def _params(*sem):
    return pltpu.CompilerParams(dimension_semantics=sem, vmem_limit_bytes=VMEM_LIMIT)


def _dot(a, b):
    return jnp.dot(a, b, preferred_element_type=F32)


def _dot_nt(a, b):
    return lax.dot_general(a, b, (((1,), (1,)), ((), ())), preferred_element_type=F32)


def _dot_tn(a, b):
    return lax.dot_general(a, b, (((0,), (0,)), ((), ())), preferred_element_type=F32)


def _split2(x):
    hi = x.astype(BF16)
    lo = (x - hi.astype(F32)).astype(BF16)
    return hi, lo


def _split3(x):
    hi = x.astype(BF16)
    r = x - hi.astype(F32)
    mid = r.astype(BF16)
    lo = (r - mid.astype(F32)).astype(BF16)
    return hi, mid, lo


def _rms_matmul_kernel(x_ref, g_ref, w_ref, o_ref, h_ref):
    @pl.when(pl.program_id(1) == 0)
    def _():
        x = x_ref[...]
        r = lax.rsqrt(jnp.mean(x * x, axis=-1, keepdims=True) + EPS)
        h_ref[...] = (x * r * g_ref[...]).astype(BF16)

    o_ref[...] = _dot(h_ref[...], w_ref[...]).astype(o_ref.dtype)


def rms_matmul(x, g, w, out_dtype, *, tm=512, tn=512, split_cols=False):
    T, D = x.shape
    N = w.shape[1]
    tm = min(tm, T)
    tn = min(tn, N)
    assert T % tm == 0 and N % tn == 0
    if split_cols:
        out_shape = jax.ShapeDtypeStruct((N // tn, T, tn), out_dtype)
        out_spec = pl.BlockSpec((None, tm, tn), lambda i, j: (j, i, 0))
    else:
        out_shape = jax.ShapeDtypeStruct((T, N), out_dtype)
        out_spec = pl.BlockSpec((tm, tn), lambda i, j: (i, j))
    return pl.pallas_call(
        _rms_matmul_kernel,
        out_shape=out_shape,
        grid=(T // tm, N // tn),
        in_specs=[pl.BlockSpec((tm, D), lambda i, j: (i, 0)),
                  pl.BlockSpec((1, D), lambda i, j: (0, 0)),
                  pl.BlockSpec((D, tn), lambda i, j: (0, j))],
        out_specs=out_spec,
        scratch_shapes=[pltpu.VMEM((tm, D), BF16)],
        compiler_params=_params("parallel", "arbitrary"),
        name="rms_matmul",
    )(x, g.reshape(1, D), w)


def _out_proj_kernel(x_ref, y1_ref, y2_ref, w1_ref, w2_ref, o_ref):
    o_ref[...] = x_ref[...] + _dot(y1_ref[...], w1_ref[...]) + _dot(y2_ref[...], w2_ref[...])


def out_proj(x, y1, y2, w1, w2, *, tm=512, tn=512):
    T, D = x.shape
    K1, K2 = y1.shape[1], y2.shape[1]
    tm = min(tm, T)
    return pl.pallas_call(
        _out_proj_kernel,
        out_shape=jax.ShapeDtypeStruct((T, D), F32),
        grid=(T // tm, D // tn),
        in_specs=[pl.BlockSpec((tm, tn), lambda i, j: (i, j)),
                  pl.BlockSpec((tm, K1), lambda i, j: (i, 0)),
                  pl.BlockSpec((tm, K2), lambda i, j: (i, 0)),
                  pl.BlockSpec((K1, tn), lambda i, j: (0, j)),
                  pl.BlockSpec((K2, tn), lambda i, j: (0, j))],
        out_specs=pl.BlockSpec((tm, tn), lambda i, j: (i, j)),
        compiler_params=_params("parallel", "arbitrary"),
        name="out_proj",
    )(x, y1, y2, w1, w2)


def _ffn_kernel(x_ref, g_ref, wg_ref, wu_ref, wd_ref, o_ref, h_ref):
    @pl.when(pl.program_id(1) == 0)
    def _():
        x = x_ref[...]
        r = lax.rsqrt(jnp.mean(x * x, axis=-1, keepdims=True) + EPS)
        h_ref[...] = (x * r * g_ref[...]).astype(BF16)
        o_ref[...] = x

    h = h_ref[...]
    a = _dot(h, wg_ref[...])
    u = _dot(h, wu_ref[...])
    act = (a * jax.nn.sigmoid(a) * u).astype(BF16)
    o_ref[...] += _dot(act, wd_ref[...])


def ffn(x, g, wg, wu, wd, *, tm=512, tf=512):
    T, D = x.shape
    FF = wg.shape[1]
    tm = min(tm, T)
    assert FF % tf == 0
    return pl.pallas_call(
        _ffn_kernel,
        out_shape=jax.ShapeDtypeStruct((T, D), F32),
        grid=(T // tm, FF // tf),
        in_specs=[pl.BlockSpec((tm, D), lambda i, f: (i, 0)),
                  pl.BlockSpec((1, D), lambda i, f: (0, 0)),
                  pl.BlockSpec((D, tf), lambda i, f: (0, f)),
                  pl.BlockSpec((D, tf), lambda i, f: (0, f)),
                  pl.BlockSpec((tf, D), lambda i, f: (f, 0))],
        out_specs=pl.BlockSpec((tm, D), lambda i, f: (i, 0)),
        scratch_shapes=[pltpu.VMEM((tm, D), BF16)],
        compiler_params=_params("parallel", "arbitrary"),
        name="ffn",
    )(x, g.reshape(1, D), wg, wu, wd)


def _rms_kernel(x_ref, g_ref, o_ref):
    x = x_ref[...]
    r = lax.rsqrt(jnp.mean(x * x, axis=-1, keepdims=True) + EPS)
    o_ref[...] = x * r * g_ref[...]


def rms_only(x, g, *, tm=512):
    T, D = x.shape
    tm = min(tm, T)
    return pl.pallas_call(
        _rms_kernel,
        out_shape=jax.ShapeDtypeStruct((T, D), F32),
        grid=(T // tm,),
        in_specs=[pl.BlockSpec((tm, D), lambda i: (i, 0)),
                  pl.BlockSpec((1, D), lambda i: (0, 0))],
        out_specs=pl.BlockSpec((tm, D), lambda i: (i, 0)),
        compiler_params=_params("parallel"),
        name="final_norm",
    )(x, g.reshape(1, D))


def _mem_attn_kernel(q_ref, k_ref, v_ref, o_ref):
    scale = HEAD_DIM ** -0.5
    for h in range(MEM_HEADS):
        sl = slice(h * HEAD_DIM, (h + 1) * HEAD_DIM)
        s = _dot_nt(q_ref[:, sl], k_ref[:, sl]) * scale
        m = jnp.max(s, axis=-1, keepdims=True)
        e = jnp.exp(s - m)
        p = e / jnp.sum(e, axis=-1, keepdims=True)
        o_ref[:, sl] = _dot(p.astype(BF16), v_ref[:, sl]).astype(o_ref.dtype)


def mem_attention(u3, qblk, mem_kv, *, tq=512):
    B, S, _ = u3.shape
    M = mem_kv.shape[1]
    tq = min(tq, S)
    return pl.pallas_call(
        _mem_attn_kernel,
        out_shape=jax.ShapeDtypeStruct((B, S, MEM_WIDTH), BF16),
        grid=(B, S // tq),
        in_specs=[pl.BlockSpec((None, tq, MEM_WIDTH), lambda b, i: (b, i, qblk)),
                  pl.BlockSpec((None, M, MEM_WIDTH), lambda b, i: (b, 0, 0)),
                  pl.BlockSpec((None, M, MEM_WIDTH), lambda b, i: (b, 0, 1))],
        out_specs=pl.BlockSpec((None, tq, MEM_WIDTH), lambda b, i: (b, i, 0)),
        compiler_params=_params("parallel", "parallel"),
        name="mem_attention",
    )(u3, mem_kv, mem_kv)


def _stick_kernel(q_ref, k_ref, v_ref, o_ref, *, tq):
    i = pl.program_id(2)
    scale = HEAD_DIM ** -0.5
    q = q_ref[...]
    rows = lax.broadcasted_iota(jnp.int32, (tq, tq), 0)
    cols = lax.broadcasted_iota(jnp.int32, (tq, tq), 1)
    later = (rows > cols).astype(BF16)

    def body(jj, carry):
        acc, tail = carry
        j = i - jj
        off = pl.multiple_of(j * tq, tq)
        k = k_ref[pl.ds(off, tq), :]
        v = v_ref[pl.ds(off, tq), :]
        z = _dot_nt(q, k) * scale
        softplus = jnp.maximum(z, 0.0) + jnp.log1p(jnp.exp(-jnp.abs(z)))
        causal = (cols < rows) | (jj > 0)
        log_keep = jnp.where(causal, -softplus, 0.0)
        hi, lo = _split2(log_keep)
        after = _dot(hi, later) + _dot(lo, later) + tail
        a = jnp.where(causal, jnp.exp(z + log_keep + after), 0.0)
        acc = acc + _dot(a.astype(BF16), v)
        tail = tail + jnp.sum(log_keep, axis=-1, keepdims=True)
        return acc, tail

    acc, _ = lax.fori_loop(0, i + 1, body,
                           (jnp.zeros((tq, HEAD_DIM), F32), jnp.zeros((tq, 1), F32)))
    o_ref[...] = acc.astype(o_ref.dtype)


def stick_breaking(u3, *, tq=128):
    B, S, _ = u3.shape
    H = B_HEADS
    tq = min(tq, S)
    return pl.pallas_call(
        functools.partial(_stick_kernel, tq=tq),
        out_shape=jax.ShapeDtypeStruct((B, S, MIX_WIDTH), BF16),
        grid=(B, H, S // tq),
        in_specs=[pl.BlockSpec((None, tq, HEAD_DIM), lambda b, h, i: (b, i, h)),
                  pl.BlockSpec((None, S, HEAD_DIM), lambda b, h, i: (b, 0, H + h)),
                  pl.BlockSpec((None, S, HEAD_DIM), lambda b, h, i: (b, 0, 2 * H + h))],
        out_specs=pl.BlockSpec((None, tq, HEAD_DIM), lambda b, h, i: (b, i, h)),
        compiler_params=_params("parallel", "parallel", "arbitrary"),
        name="stick_breaking",
    )(u3, u3, u3)


def _swa_kernel(sink_ref, q_ref, kvp_ref, kvc_ref, o_ref, *, tq):
    g = pl.program_id(1)
    i = pl.program_id(2)
    dh = D_HEAD_DIM
    scale = dh ** -0.5
    kv = jnp.concatenate([kvp_ref[...], kvc_ref[...]], axis=0)
    k = kv[:, :dh]
    v = kv[:, dh:]
    r = lax.broadcasted_iota(jnp.int32, (tq, 2 * tq), 0)
    c = lax.broadcasted_iota(jnp.int32, (tq, 2 * tq), 1)
    rel = r - c + tq
    mask = (rel >= 0) & (rel < D_WINDOW) & ((c >= tq) | (i > 0))
    for hh in range(D_HPG):
        sk = sink_ref[g * D_HPG + hh]
        s = _dot_nt(q_ref[:, hh * dh:(hh + 1) * dh], k) * scale
        s = jnp.where(mask, s, NEG_INF)
        m = jnp.maximum(jnp.max(s, axis=-1, keepdims=True), sk)
        e = jnp.exp(s - m)
        p = e / (jnp.sum(e, axis=-1, keepdims=True) + jnp.exp(sk - m))
        o_ref[:, hh * dh:(hh + 1) * dh] = _dot(p.astype(BF16), v).astype(o_ref.dtype)


def sink_window(u3, sinks):
    B, S, _ = u3.shape
    tq = D_WINDOW
    gw = D_HPG * D_HEAD_DIM
    kv0 = (MIX_WIDTH + MEM_WIDTH) // (2 * D_HEAD_DIM)
    return pl.pallas_call(
        functools.partial(_swa_kernel, tq=tq),
        out_shape=jax.ShapeDtypeStruct((B, S, MIX_WIDTH), BF16),
        grid=(B, D_KV, S // tq),
        in_specs=[pl.BlockSpec(memory_space=pltpu.SMEM),
                  pl.BlockSpec((None, tq, gw), lambda b, g, i: (b, i, g)),
                  pl.BlockSpec((None, tq, 2 * D_HEAD_DIM),
                               lambda b, g, i: (b, jnp.maximum(i - 1, 0), kv0 + g)),
                  pl.BlockSpec((None, tq, 2 * D_HEAD_DIM), lambda b, g, i: (b, i, kv0 + g))],
        out_specs=pl.BlockSpec((None, tq, gw), lambda b, g, i: (b, i, g)),
        compiler_params=_params("parallel", "parallel", "arbitrary"),
        name="sink_window",
    )(sinks, u3, u3, u3)


def _conv_kernel(xp_ref, xc_ref, w_ref, b_ref, s_ref, o_ref, buf_ref, *, tc, halo):
    i = pl.program_id(1)
    prev = xp_ref[...]
    buf_ref[0:halo, :] = jnp.where(i > 0, prev, jnp.zeros_like(prev))
    buf_ref[halo:halo + tc, :] = xc_ref[...]
    acc = b_ref[...] + w_ref[C_CONV - 1:C_CONV, :] * xc_ref[...]
    for j in range(C_CONV - 1):
        sh = C_CONV - 1 - j
        acc = acc + w_ref[j:j + 1, :] * buf_ref[halo - sh:halo - sh + tc, :]
    y = acc * jax.nn.sigmoid(acc)
    o_ref[...] = (y * s_ref[...]).astype(o_ref.dtype)


def conv_silu(u3, conv_w, conv_b, col_scale, *, tc=256, tn=512):
    B, S, _ = u3.shape
    C = conv_w.shape[1]
    tc = min(tc, S)
    halo = 8
    return pl.pallas_call(
        functools.partial(_conv_kernel, tc=tc, halo=halo),
        out_shape=jax.ShapeDtypeStruct((B, S, C), BF16),
        grid=(B, S // tc, C // tn),
        in_specs=[pl.BlockSpec((None, halo, tn),
                               lambda b, i, j: (b, jnp.maximum(i * (tc // halo) - 1, 0), j)),
                  pl.BlockSpec((None, tc, tn), lambda b, i, j: (b, i, j)),
                  pl.BlockSpec((C_CONV, tn), lambda b, i, j: (0, j)),
                  pl.BlockSpec((1, tn), lambda b, i, j: (0, j)),
                  pl.BlockSpec((1, tn), lambda b, i, j: (0, j))],
        out_specs=pl.BlockSpec((None, tc, tn), lambda b, i, j: (b, i, j)),
        scratch_shapes=[pltpu.VMEM((tc + halo, tn), F32)],
        compiler_params=_params("parallel", "parallel", "parallel"),
        name="conv_silu",
    )(u3, u3, conv_w, conv_b.reshape(1, C), col_scale.reshape(1, C))


def _mlstm_kernel(q_ref, k_ref, v_ref, o_ref, gt_ref, gb_ref, hn_ref, out_ref,
                  state_ref, m_ref, *, L):
    H, dh = C_HEADS, C_HEAD_DIM
    ext = dh + LANES

    @pl.when(pl.program_id(1) == 0)
    def _():
        state_ref[...] = jnp.zeros_like(state_ref)
        m_ref[...] = jnp.zeros_like(m_ref)

    rows = lax.broadcasted_iota(jnp.int32, (L, L), 0)
    cols = lax.broadcasted_iota(jnp.int32, (L, L), 1)
    tril = rows >= cols
    eye = rows == cols
    gates = gt_ref[...] + gb_ref[...]
    log_f = jnp.minimum(gates, 0.0) - jnp.log1p(jnp.exp(-jnp.abs(gates)))
    f_hi, f_mid, f_lo = _split3(log_f)
    trilb = tril.astype(BF16)
    bsum = _dot(trilb, f_hi) + _dot(trilb, f_mid) + _dot(trilb, f_lo)
    ones_col = (lax.broadcasted_iota(jnp.int32, (L, LANES), 1) == 0).astype(F32)

    for h in range(H):
        sl = slice(h * dh, (h + 1) * dh)
        q = q_ref[:, sl]
        k = k_ref[:, sl]
        v_ext = jnp.concatenate([v_ref[:, sl], ones_col], axis=1)
        ic = gates[:, h:h + 1]
        bc = bsum[:, H + h:H + h + 1]
        m_prev = m_ref[h][0:1, 0:1]
        rc = ic - bc
        rrow = jnp.sum(jnp.where(eye, rc, 0.0), axis=0, keepdims=True)
        dm = jnp.where(tril, bc + rrow, NEG_INF)
        m_inter = bc + m_prev
        m_t = jnp.maximum(m_inter, jnp.max(dm, axis=-1, keepdims=True))
        w = jnp.exp(dm - m_t)
        a_inter = jnp.exp(m_inter - m_t)
        sqk = _dot_nt(q, k) * w
        state = state_ref[h]
        num = a_inter * _dot(q, state.astype(BF16)) + _dot(sqk.astype(BF16), v_ext.astype(BF16))
        den = jnp.maximum(jnp.abs(num[:, dh:dh + 1]), jnp.exp(-m_t))
        hid = num[:, :dh] / den
        hid = jax.nn.sigmoid(o_ref[:, sl]) * hid
        hid = hid * lax.rsqrt(jnp.mean(hid * hid, axis=-1, keepdims=True) + EPS)
        out_ref[:, sl] = (hid * hn_ref[:, sl]).astype(out_ref.dtype)

        b_last = bc[L - 1:L, :]
        gk = b_last - bc + ic
        m_new = jnp.maximum(b_last + m_prev, jnp.max(gk, axis=0, keepdims=True))
        wk = jnp.exp(gk - m_new)
        decay = jnp.exp(b_last + m_prev - m_new)
        state_ref[h] = decay * state + _dot_tn(k, (wk * v_ext).astype(BF16))
        m_ref[h] = jnp.broadcast_to(m_new, m_ref.shape[1:])


def mlstm(qk3, u3, gate_b, head_norm, *, L=C_CHUNK):
    B, S, _ = u3.shape
    W = MIX_WIDTH
    H, dh = C_HEADS, C_HEAD_DIM
    L = min(L, S)
    gblk = 4 * W // LANES
    gb = jnp.zeros((1, LANES), F32).at[0, :2 * H].set(gate_b)
    return pl.pallas_call(
        functools.partial(_mlstm_kernel, L=L),
        out_shape=jax.ShapeDtypeStruct((B, S, W), BF16),
        grid=(B, S // L),
        in_specs=[pl.BlockSpec((None, L, W), lambda b, c: (b, c, 0)),
                  pl.BlockSpec((None, L, W), lambda b, c: (b, c, 1)),
                  pl.BlockSpec((None, L, W), lambda b, c: (b, c, 2)),
                  pl.BlockSpec((None, L, W), lambda b, c: (b, c, 3)),
                  pl.BlockSpec((None, L, LANES), lambda b, c: (b, c, gblk)),
                  pl.BlockSpec((1, LANES), lambda b, c: (0, 0)),
                  pl.BlockSpec((1, W), lambda b, c: (0, 0))],
        out_specs=pl.BlockSpec((None, L, W), lambda b, c: (b, c, 0)),
        scratch_shapes=[pltpu.VMEM((H, dh, dh + LANES), F32),
                        pltpu.VMEM((H, 8, LANES), F32)],
        compiler_params=_params("parallel", "arbitrary"),
        name="mlstm",
    )(qk3, qk3, u3, u3, u3, gb, head_norm.reshape(1, W))


def _compress_kernel(x_ref, pe_ref, w1_ref, w2_ref, o_ref):
    half = w1_ref.shape[0] // 2
    x = x_ref[...]
    n = x.shape[0]
    first = _dot(x, w1_ref[0:half, :])
    second = _dot(x, w1_ref[half:, :])
    pe = jnp.broadcast_to(pe_ref[...], (8, pe_ref.shape[1])).astype(BF16)
    bias = _dot(pe, w1_ref[...])[0:1, :]
    hid = first + pltpu.roll(second, n - 1, 0) + bias
    o_ref[...] = _dot(jax.nn.gelu(hid).astype(BF16), w2_ref[...]).astype(o_ref.dtype)


def compress(xc, pe, w1, w2):
    _, G, B, n, width = xc.shape
    hid = w1.shape[2]
    return pl.pallas_call(
        _compress_kernel,
        out_shape=jax.ShapeDtypeStruct((2, G, B, n, HEAD_DIM), BF16),
        grid=(2, G, B),
        in_specs=[pl.BlockSpec((None, None, None, n, width), lambda j, g, b: (j, g, b, 0, 0)),
                  pl.BlockSpec((None, 1, 2 * width), lambda j, g, b: (j, 0, 0)),
                  pl.BlockSpec((None, 2 * width, hid), lambda j, g, b: (j, 0, 0)),
                  pl.BlockSpec((None, hid, HEAD_DIM), lambda j, g, b: (j, 0, 0))],
        out_specs=pl.BlockSpec((None, None, None, n, HEAD_DIM), lambda j, g, b: (j, g, b, 0, 0)),
        compiler_params=_params("parallel", "parallel", "parallel"),
        name="nsa_compress",
    )(xc, pe, w1, w2)


def _cmp_select_kernel(q_ref, kc_ref, vc_ref, o_ref, sel_ref, *, tq, n_cmp, n_slc):
    i = pl.program_id(2)
    scale = HEAD_DIM ** -0.5
    npad = kc_ref.shape[0]
    kc = kc_ref[...]
    vc = vc_ref[...]
    pos = i * tq + lax.broadcasted_iota(jnp.int32, (tq, npad), 0)
    nidx = lax.broadcasted_iota(jnp.int32, (tq, npad), 1)
    cmask = (nidx * A_CMP_STRIDE + (A_CMP_LEN - 1) <= pos) & (nidx < n_cmp)
    psum = jnp.zeros((tq, npad), F32)
    for h in range(A_HPG):
        sl = slice(h * HEAD_DIM, (h + 1) * HEAD_DIM)
        s = jnp.where(cmask, _dot_nt(q_ref[:, sl], kc) * scale, NEG_INF)
        m = jnp.max(s, axis=-1, keepdims=True)
        e = jnp.where(cmask, jnp.exp(s - m), 0.0)
        p = e / jnp.maximum(jnp.sum(e, axis=-1, keepdims=True), 1e-30)
        o_ref[:, sl] = _dot(p.astype(BF16), vc)
        psum = psum + p

    on = lax.broadcasted_iota(jnp.int32, (npad, LANES), 0) * A_CMP_STRIDE
    oj = lax.broadcasted_iota(jnp.int32, (npad, LANES), 1)
    overlap = ((on < (oj + 1) * A_SEL_LEN) & (on + A_CMP_LEN > oj * A_SEL_LEN)
               & (on < n_cmp * A_CMP_STRIDE) & (oj < n_slc)).astype(BF16)
    p_hi, p_lo = _split2(psum)
    imp = _dot(p_hi, overlap) + _dot(p_lo, overlap)

    blk = lax.broadcasted_iota(jnp.int32, (tq, LANES), 1)
    cur = (i * tq + lax.broadcasted_iota(jnp.int32, (tq, LANES), 0)) // A_SEL_LEN
    valid = blk <= cur
    forced = (blk == 0) | (blk == cur) | (blk == cur - 1)
    score = jnp.where(valid, jnp.where(forced, A_FORCE, imp), A_EXCLUDE)
    score = jnp.where(blk < n_slc, score, 4.0 * A_EXCLUDE)
    rank = jnp.zeros((tq, LANES), jnp.int32)
    for j in range(n_slc):
        other = score[:, j:j + 1]
        ahead = (other > score) | ((other == score) & (blk > j))
        rank = rank + ahead.astype(jnp.int32)
    sel = (rank < min(A_TOP_N, n_slc)) & valid
    sel_ref[...] = sel.astype(sel_ref.dtype)


def cmp_select(u3, kvc, *, tq=128):
    B, S, _ = u3.shape
    G = A_KV
    npad = kvc.shape[3]
    n_cmp = (S - A_CMP_LEN) // A_CMP_STRIDE + 1
    n_slc = S // A_SEL_LEN
    tq = min(tq, S)
    gw = A_HPG * HEAD_DIM
    return pl.pallas_call(
        functools.partial(_cmp_select_kernel, tq=tq, n_cmp=n_cmp, n_slc=n_slc),
        out_shape=(jax.ShapeDtypeStruct((B, S, MIX_WIDTH), F32),
                   jax.ShapeDtypeStruct((B, G, S, LANES), BF16)),
        grid=(B, G, S // tq),
        in_specs=[pl.BlockSpec((None, tq, gw), lambda b, g, i: (b, i, g)),
                  pl.BlockSpec((None, None, None, npad, HEAD_DIM), lambda b, g, i: (0, g, b, 0, 0)),
                  pl.BlockSpec((None, None, None, npad, HEAD_DIM), lambda b, g, i: (1, g, b, 0, 0))],
        out_specs=(pl.BlockSpec((None, tq, gw), lambda b, g, i: (b, i, g)),
                   pl.BlockSpec((None, None, tq, LANES), lambda b, g, i: (b, g, i, 0))),
        compiler_params=_params("parallel", "parallel", "parallel"),
        name="nsa_cmp_select",
    )(u3, kvc, kvc)


def _flash_step(qs, k, v, mask, carry, scale):
    m, l, acc = carry
    hp = qs.shape[0] // mask.shape[0]
    s = _dot_nt(qs, k) * scale
    s3 = jnp.where(mask[None], s.reshape((hp,) + mask.shape), NEG_INF)
    s = s3.reshape(s.shape)
    m_new = jnp.maximum(m, jnp.max(s, axis=-1, keepdims=True))
    p = jnp.exp(s - m_new)
    p = jnp.where(mask[None], p.reshape(s3.shape), 0.0).reshape(s.shape)
    alpha = jnp.exp(m - m_new)
    l = alpha * l + jnp.sum(p, axis=-1, keepdims=True)
    acc = alpha * acc + _dot(p.astype(BF16), v)
    return m_new, l, acc


def _slc_win_kernel(q_ref, ks_ref, vs_ref, kw_ref, vw_ref, sel_ref, oc_ref, gt_ref, o_ref, *, tq):
    i = pl.program_id(2)
    scale = HEAD_DIM ** -0.5
    hp = A_HPG
    qs = jnp.concatenate([q_ref[:, h * HEAD_DIM:(h + 1) * HEAD_DIM] for h in range(hp)], axis=0)
    r = lax.broadcasted_iota(jnp.int32, (tq, tq), 0)
    c = lax.broadcasted_iota(jnp.int32, (tq, tq), 1)
    sel = sel_ref[...]
    per_tile = tq // A_SEL_LEN
    init = (jnp.full((hp * tq, 1), NEG_INF, F32), jnp.zeros((hp * tq, 1), F32),
            jnp.zeros((hp * tq, HEAD_DIM), F32))

    def slc_body(j, carry):
        off = pl.multiple_of(j * tq, tq)
        expand = (r == j * per_tile + c // A_SEL_LEN).astype(BF16)
        chosen = _dot(sel, expand) > 0.5
        mask = chosen & ((c <= r) | (j < i))
        return _flash_step(qs, ks_ref[pl.ds(off, tq), :], vs_ref[pl.ds(off, tq), :], mask, carry, scale)

    _, l_s, acc_s = lax.fori_loop(0, i + 1, slc_body, init)
    o_slc = acc_s / l_s

    def win_body(j, carry):
        off = pl.multiple_of(j * tq, tq)
        rel = (i - j) * tq + r - c
        mask = (rel >= 0) & (rel < A_WINDOW)
        return _flash_step(qs, kw_ref[pl.ds(off, tq), :], vw_ref[pl.ds(off, tq), :], mask, carry, scale)

    _, l_w, acc_w = lax.fori_loop(jnp.maximum(i - A_WINDOW // tq, 0), i + 1, win_body, init)
    o_win = acc_w / l_w

    gate = jax.nn.sigmoid(gt_ref[...])
    for h in range(hp):
        sl = slice(h * HEAD_DIM, (h + 1) * HEAD_DIM)
        rs = slice(h * tq, (h + 1) * tq)
        o = (gate[:, h:h + 1] * oc_ref[:, sl]
             + gate[:, hp + h:hp + h + 1] * o_slc[rs]
             + gate[:, 2 * hp + h:2 * hp + h + 1] * o_win[rs])
        o_ref[:, sl] = o.astype(o_ref.dtype)


def slc_win(u3, sel, o_cmp, gate3, *, tq=128):
    B, S, _ = u3.shape
    G = A_KV
    tq = min(tq, S)
    gw = A_HPG * HEAD_DIM
    kv0 = A_HEADS

    def kv_spec(which):
        return pl.BlockSpec((None, S, HEAD_DIM), lambda b, g, i: (b, 0, kv0 + which * G + g))

    return pl.pallas_call(
        functools.partial(_slc_win_kernel, tq=tq),
        out_shape=jax.ShapeDtypeStruct((B, S, MIX_WIDTH), BF16),
        grid=(B, G, S // tq),
        in_specs=[pl.BlockSpec((None, tq, gw), lambda b, g, i: (b, i, g)),
                  kv_spec(0), kv_spec(1), kv_spec(2), kv_spec(3),
                  pl.BlockSpec((None, None, tq, LANES), lambda b, g, i: (b, g, i, 0)),
                  pl.BlockSpec((None, tq, gw), lambda b, g, i: (b, i, g)),
                  pl.BlockSpec((None, tq, LANES), lambda b, g, i: (b, i, g))],
        out_specs=pl.BlockSpec((None, tq, gw), lambda b, g, i: (b, i, g)),
        compiler_params=_params("parallel", "parallel", "arbitrary"),
        name="nsa_slc_win",
    )(u3, u3, u3, u3, u3, sel, o_cmp, gate3)


def _pad_cols(w, n):
    return jnp.pad(w, ((0, 0), (0, n - w.shape[1])))


def nsa_layer_mix(x, B, S, norm, w_in, pe_k, w1_k, w2_k, pe_v, w1_v, w2_v, mem_kv):
    H, G, dh = A_HEADS, A_KV, HEAD_DIM
    qw = H * dh
    kvw = 6 * G * dh
    gw = 3 * H
    w_main = jnp.concatenate([w_in[:, :qw], w_in[:, qw + 2 * G * dh:qw + kvw],
                              w_in[:, qw + kvw + gw:]], axis=1).astype(BF16)
    w_cmp = w_in[:, qw:qw + 2 * G * dh].astype(BF16)
    wg = w_in[:, qw + kvw:qw + kvw + gw].reshape(-1, G, A_HPG, 3).transpose(0, 1, 3, 2)
    wg = jnp.pad(wg.reshape(-1, G, 3 * A_HPG), ((0, 0), (0, 0), (0, LANES - 3 * A_HPG)))
    wg = wg.reshape(-1, G * LANES).astype(BF16)

    u = rms_matmul(x, norm, w_main, BF16).reshape(B, S, -1)
    gate3 = rms_matmul(x, norm, wg, F32, tn=G * LANES).reshape(B, S, -1)
    xc = rms_matmul(x, norm, w_cmp, BF16, tn=dh, split_cols=True)
    xc = xc.reshape(2, G, B, S // A_CMP_STRIDE, A_CMP_STRIDE * dh)
    pe = jnp.stack([pe_k.reshape(1, -1), pe_v.reshape(1, -1)])
    kvc = compress(xc, pe, jnp.stack([w1_k, w1_v]).astype(BF16), jnp.stack([w2_k, w2_v]).astype(BF16))
    o_cmp, sel = cmp_select(u, kvc)
    mixed = slc_win(u, sel, o_cmp, gate3)
    mem_out = mem_attention(u, (qw + 4 * G * dh) // MEM_WIDTH, mem_kv)
    return mixed, mem_out


def stick_layer_mix(x, B, S, norm, w_in, mem_kv):
    u = rms_matmul(x, norm, w_in.astype(BF16), BF16).reshape(B, S, -1)
    mixed = stick_breaking(u)
    mem_out = mem_attention(u, 3 * MIX_WIDTH // MEM_WIDTH, mem_kv)
    return mixed, mem_out


def mlstm_layer_mix(x, B, S, norm, w_in, conv_w, conv_b, gate_b, head_norm, mem_kv):
    W = MIX_WIDTH
    nf = 4 * W + 2 * C_HEADS
    w_f = _pad_cols(w_in[:, :nf], 4 * W + LANES).astype(BF16)
    w_m = w_in[:, nf:].astype(BF16)
    u = rms_matmul(x, norm, w_f, F32, tn=LANES * 7).reshape(B, S, -1)
    um = rms_matmul(x, norm, w_m, BF16).reshape(B, S, -1)
    col_scale = jnp.concatenate([jnp.ones((W,), F32), jnp.full((W,), C_HEAD_DIM ** -0.5, F32)])
    qk = conv_silu(u, conv_w, conv_b, col_scale)
    mixed = mlstm(qk, u, gate_b, head_norm)
    mem_out = mem_attention(um, 0, mem_kv)
    return mixed, mem_out


def swa_layer_mix(x, B, S, norm, w_in, sinks, mem_kv):
    W = MIX_WIDTH
    kw = D_KV * D_HEAD_DIM
    wk = w_in[:, W:W + kw].reshape(-1, D_KV, D_HEAD_DIM)
    wv = w_in[:, W + kw:W + 2 * kw].reshape(-1, D_KV, D_HEAD_DIM)
    wkv = jnp.concatenate([wk, wv], axis=2).reshape(-1, 2 * kw)
    w_main = jnp.concatenate([w_in[:, :W], w_in[:, W + 2 * kw:], wkv], axis=1)
    w_main = _pad_cols(w_main, W + MEM_WIDTH + 4 * LANES).astype(BF16)
    u = rms_matmul(x, norm, w_main, BF16).reshape(B, S, -1)
    mixed = sink_window(u, sinks)
    mem_out = mem_attention(u, W // MEM_WIDTH, mem_kv)
    return mixed, mem_out


def kernel(x, mem, mem_norm, mem_w_kv, l0_norm_mix, l0_w_in, l0_cmp_pe_k, l0_cmp_w1_k, l0_cmp_w2_k, l0_cmp_pe_v, l0_cmp_w1_v, l0_cmp_w2_v, l0_w_out, l0_norm_ffn, l0_w_gate, l0_w_up, l0_w_down, l1_norm_mix, l1_w_in, l1_w_out, l1_norm_ffn, l1_w_gate, l1_w_up, l1_w_down, l2_norm_mix, l2_w_in, l2_conv_w, l2_conv_b, l2_gate_b, l2_head_norm, l2_w_out, l2_norm_ffn, l2_w_gate, l2_w_up, l2_w_down, l3_norm_mix, l3_w_in, l3_sinks, l3_w_out, l3_norm_ffn, l3_w_gate, l3_w_up, l3_w_down, final_norm):
    B, S, D = x.shape
    M = mem.shape[1]
    mem_kv = rms_matmul(mem.reshape(B * M, D), mem_norm, mem_w_kv.astype(BF16), BF16).reshape(B, M, -1)
    xs = x.reshape(B * S, D)

    def finish(xs, mixed, mem_out, w_out, norm_ffn, w_gate, w_up, w_down):
        w_out = w_out.astype(BF16)
        xs = out_proj(xs, mixed.reshape(B * S, -1), mem_out.reshape(B * S, -1),
                      w_out[:MIX_WIDTH], w_out[MIX_WIDTH:])
        return ffn(xs, norm_ffn, w_gate.astype(BF16), w_up.astype(BF16), w_down.astype(BF16))

    mixed, mem_out = nsa_layer_mix(xs, B, S, l0_norm_mix, l0_w_in, l0_cmp_pe_k, l0_cmp_w1_k, l0_cmp_w2_k,
                                   l0_cmp_pe_v, l0_cmp_w1_v, l0_cmp_w2_v, mem_kv)
    xs = finish(xs, mixed, mem_out, l0_w_out, l0_norm_ffn, l0_w_gate, l0_w_up, l0_w_down)
    mixed, mem_out = stick_layer_mix(xs, B, S, l1_norm_mix, l1_w_in, mem_kv)
    xs = finish(xs, mixed, mem_out, l1_w_out, l1_norm_ffn, l1_w_gate, l1_w_up, l1_w_down)
    mixed, mem_out = mlstm_layer_mix(xs, B, S, l2_norm_mix, l2_w_in, l2_conv_w, l2_conv_b, l2_gate_b,
                                     l2_head_norm, mem_kv)
    xs = finish(xs, mixed, mem_out, l2_w_out, l2_norm_ffn, l2_w_gate, l2_w_up, l2_w_down)
    mixed, mem_out = swa_layer_mix(xs, B, S, l3_norm_mix, l3_w_in, l3_sinks, mem_kv)
    xs = finish(xs, mixed, mem_out, l3_w_out, l3_norm_ffn, l3_w_gate, l3_w_up, l3_w_down)
    return rms_only(xs, final_norm).reshape(B, S, D)
```

```python
import functools

import jax
import jax.numpy as jnp
from jax import lax
from jax.experimental import pallas as pl
from jax.experimental.pallas import tpu as pltpu

F32 = jnp.float32
BF16 = jnp.bfloat16

LANES = 128
VMEM_LIMIT = 56 * 1024 * 1024

D_MODEL = 2048
HEAD_DIM = 128
MEM_HEADS = 4
MEM_WIDTH = MEM_HEADS * HEAD_DIM
MIX_WIDTH = D_MODEL - MEM_WIDTH
EPS = 1e-6
NEG_INF = -1e30

A_HEADS = MIX_WIDTH // HEAD_DIM
A_KV = 2
A_HPG = A_HEADS // A_KV
A_CMP_LEN = 32
A_CMP_STRIDE = 16
A_SEL_LEN = 64
A_TOP_N = 16
A_WINDOW = 512
A_FORCE = 1e6
A_EXCLUDE = -1e9

B_HEADS = MIX_WIDTH // HEAD_DIM

C_HEADS = 4
C_HEAD_DIM = MIX_WIDTH // C_HEADS
C_CONV = 4
C_CHUNK = 64

D_HEAD_DIM = 64
D_HEADS = MIX_WIDTH // D_HEAD_DIM
D_KV = D_HEADS // 8
D_HPG = D_HEADS // D_KV
D_WINDOW = 128


def _params(*sem):
    return pltpu.CompilerParams(dimension_semantics=sem, vmem_limit_bytes=VMEM_LIMIT)


def _dot(a, b):
    return jnp.dot(a, b, preferred_element_type=F32)


def _dot_nt(a, b):
    return lax.dot_general(a, b, (((1,), (1,)), ((), ())), preferred_element_type=F32)


def _dot_tn(a, b):
    return lax.dot_general(a, b, (((0,), (0,)), ((), ())), preferred_element_type=F32)


def _split2(x):
    hi = x.astype(BF16)
    lo = (x - hi.astype(F32)).astype(BF16)
    return hi, lo


def _split3(x):
    hi = x.astype(BF16)
    r = x - hi.astype(F32)
    mid = r.astype(BF16)
    lo = (r - mid.astype(F32)).astype(BF16)
    return hi, mid, lo


def _rms_matmul_kernel(x_ref, g_ref, w_ref, o_ref, h_ref):
    @pl.when(pl.program_id(1) == 0)
    def _():
        x = x_ref[...]
        r = lax.rsqrt(jnp.mean(x * x, axis=-1, keepdims=True) + EPS)
        h_ref[...] = (x * r * g_ref[...]).astype(BF16)

    o_ref[...] = _dot(h_ref[...], w_ref[...]).astype(o_ref.dtype)


def rms_matmul(x, g, w, out_dtype, *, tm=512, tn=512, split_cols=False):
    T, D = x.shape
    N = w.shape[1]
    tm = min(tm, T)
    tn = min(tn, N)
    assert T % tm == 0 and N % tn == 0
    if split_cols:
        out_shape = jax.ShapeDtypeStruct((N // tn, T, tn), out_dtype)
        out_spec = pl.BlockSpec((None, tm, tn), lambda i, j: (j, i, 0))
    else:
        out_shape = jax.ShapeDtypeStruct((T, N), out_dtype)
        out_spec = pl.BlockSpec((tm, tn), lambda i, j: (i, j))
    return pl.pallas_call(
        _rms_matmul_kernel,
        out_shape=out_shape,
        grid=(T // tm, N // tn),
        in_specs=[pl.BlockSpec((tm, D), lambda i, j: (i, 0)),
                  pl.BlockSpec((1, D), lambda i, j: (0, 0)),
                  pl.BlockSpec((D, tn), lambda i, j: (0, j))],
        out_specs=out_spec,
        scratch_shapes=[pltpu.VMEM((tm, D), BF16)],
        compiler_params=_params("parallel", "arbitrary"),
        name="rms_matmul",
    )(x, g.reshape(1, D), w)


def _out_proj_kernel(x_ref, y1_ref, y2_ref, w1_ref, w2_ref, o_ref):
    o_ref[...] = x_ref[...] + _dot(y1_ref[...], w1_ref[...]) + _dot(y2_ref[...], w2_ref[...])


def out_proj(x, y1, y2, w1, w2, *, tm=512, tn=512):
    T, D = x.shape
    K1, K2 = y1.shape[1], y2.shape[1]
    tm = min(tm, T)
    return pl.pallas_call(
        _out_proj_kernel,
        out_shape=jax.ShapeDtypeStruct((T, D), F32),
        grid=(T // tm, D // tn),
        in_specs=[pl.BlockSpec((tm, tn), lambda i, j: (i, j)),
                  pl.BlockSpec((tm, K1), lambda i, j: (i, 0)),
                  pl.BlockSpec((tm, K2), lambda i, j: (i, 0)),
                  pl.BlockSpec((K1, tn), lambda i, j: (0, j)),
                  pl.BlockSpec((K2, tn), lambda i, j: (0, j))],
        out_specs=pl.BlockSpec((tm, tn), lambda i, j: (i, j)),
        compiler_params=_params("parallel", "arbitrary"),
        name="out_proj",
    )(x, y1, y2, w1, w2)


def _ffn_kernel(x_ref, g_ref, wg_ref, wu_ref, wd_ref, o_ref, h_ref):
    @pl.when(pl.program_id(1) == 0)
    def _():
        x = x_ref[...]
        r = lax.rsqrt(jnp.mean(x * x, axis=-1, keepdims=True) + EPS)
        h_ref[...] = (x * r * g_ref[...]).astype(BF16)
        o_ref[...] = x

    h = h_ref[...]
    a = _dot(h, wg_ref[...])
    u = _dot(h, wu_ref[...])
    act = (a * jax.nn.sigmoid(a) * u).astype(BF16)
    o_ref[...] += _dot(act, wd_ref[...])


def ffn(x, g, wg, wu, wd, *, tm=512, tf=512):
    T, D = x.shape
    FF = wg.shape[1]
    tm = min(tm, T)
    assert FF % tf == 0
    return pl.pallas_call(
        _ffn_kernel,
        out_shape=jax.ShapeDtypeStruct((T, D), F32),
        grid=(T // tm, FF // tf),
        in_specs=[pl.BlockSpec((tm, D), lambda i, f: (i, 0)),
                  pl.BlockSpec((1, D), lambda i, f: (0, 0)),
                  pl.BlockSpec((D, tf), lambda i, f: (0, f)),
                  pl.BlockSpec((D, tf), lambda i, f: (0, f)),
                  pl.BlockSpec((tf, D), lambda i, f: (f, 0))],
        out_specs=pl.BlockSpec((tm, D), lambda i, f: (i, 0)),
        scratch_shapes=[pltpu.VMEM((tm, D), BF16)],
        compiler_params=_params("parallel", "arbitrary"),
        name="ffn",
    )(x, g.reshape(1, D), wg, wu, wd)


def _rms_kernel(x_ref, g_ref, o_ref):
    x = x_ref[...]
    r = lax.rsqrt(jnp.mean(x * x, axis=-1, keepdims=True) + EPS)
    o_ref[...] = x * r * g_ref[...]


def rms_only(x, g, *, tm=512):
    T, D = x.shape
    tm = min(tm, T)
    return pl.pallas_call(
        _rms_kernel,
        out_shape=jax.ShapeDtypeStruct((T, D), F32),
        grid=(T // tm,),
        in_specs=[pl.BlockSpec((tm, D), lambda i: (i, 0)),
                  pl.BlockSpec((1, D), lambda i: (0, 0))],
        out_specs=pl.BlockSpec((tm, D), lambda i: (i, 0)),
        compiler_params=_params("parallel"),
        name="final_norm",
    )(x, g.reshape(1, D))


def _mem_attn_kernel(q_ref, k_ref, v_ref, o_ref):
    scale = HEAD_DIM ** -0.5
    for h in range(MEM_HEADS):
        sl = slice(h * HEAD_DIM, (h + 1) * HEAD_DIM)
        s = _dot_nt(q_ref[:, sl], k_ref[:, sl]) * scale
        m = jnp.max(s, axis=-1, keepdims=True)
        e = jnp.exp(s - m)
        p = e / jnp.sum(e, axis=-1, keepdims=True)
        o_ref[:, sl] = _dot(p.astype(BF16), v_ref[:, sl]).astype(o_ref.dtype)


def mem_attention(u3, qblk, mem_kv, *, tq=512):
    B, S, _ = u3.shape
    M = mem_kv.shape[1]
    tq = min(tq, S)
    return pl.pallas_call(
        _mem_attn_kernel,
        out_shape=jax.ShapeDtypeStruct((B, S, MEM_WIDTH), BF16),
        grid=(B, S // tq),
        in_specs=[pl.BlockSpec((None, tq, MEM_WIDTH), lambda b, i: (b, i, qblk)),
                  pl.BlockSpec((None, M, MEM_WIDTH), lambda b, i: (b, 0, 0)),
                  pl.BlockSpec((None, M, MEM_WIDTH), lambda b, i: (b, 0, 1))],
        out_specs=pl.BlockSpec((None, tq, MEM_WIDTH), lambda b, i: (b, i, 0)),
        compiler_params=_params("parallel", "parallel"),
        name="mem_attention",
    )(u3, mem_kv, mem_kv)


def _stick_kernel(q_ref, k_ref, v_ref, o_ref, *, tq, tk):
    i = pl.program_id(2)
    scale = HEAD_DIM ** -0.5
    per_q = tq // tk
    q = q_ref[...]
    lrow = lax.broadcasted_iota(jnp.int32, (tk, tk), 0)
    lcol = lax.broadcasted_iota(jnp.int32, (tk, tk), 1)
    later = (lrow > lcol).astype(BF16)
    rows = lax.broadcasted_iota(jnp.int32, (tq, tk), 0)
    cols = lax.broadcasted_iota(jnp.int32, (tq, tk), 1)

    def step(j, carry, diag):
        acc, tail = carry
        off = pl.multiple_of(j * tk, tk)
        k = k_ref[pl.ds(off, tk), :]
        v = v_ref[pl.ds(off, tk), :]
        z = _dot_nt(q, k) * scale
        log_keep = -(jnp.maximum(z, 0.0) + jnp.log(1.0 + jnp.exp(-jnp.abs(z))))
        if diag is not None:
            causal = cols + diag * tk < rows
            log_keep = jnp.where(causal, log_keep, 0.0)
        hi, lo = _split2(log_keep)
        after = _dot(hi, later) + _dot(lo, later) + tail
        a = jnp.exp(z + log_keep + after)
        if diag is not None:
            a = jnp.where(causal, a, 0.0)
        acc = acc + _dot(a.astype(BF16), v)
        tail = after[:, 0:1] + log_keep[:, 0:1]
        return acc, tail

    carry = (jnp.zeros((tq, HEAD_DIM), F32), jnp.zeros((tq, 1), F32))
    for d in reversed(range(per_q)):
        carry = step(i * per_q + d, carry, d)
    acc, _ = lax.fori_loop(0, i * per_q, lambda jj, c: step(i * per_q - 1 - jj, c, None), carry)
    o_ref[...] = acc.astype(o_ref.dtype)


def stick_breaking(u3, *, tq=512, tk=256):
    B, S, _ = u3.shape
    H = B_HEADS
    tq = min(tq, S)
    tk = min(tk, tq)
    return pl.pallas_call(
        functools.partial(_stick_kernel, tq=tq, tk=tk),
        out_shape=jax.ShapeDtypeStruct((B, S, MIX_WIDTH), BF16),
        grid=(B, H, S // tq),
        in_specs=[pl.BlockSpec((None, tq, HEAD_DIM), lambda b, h, i: (b, i, h)),
                  pl.BlockSpec((None, S, HEAD_DIM), lambda b, h, i: (b, 0, H + h)),
                  pl.BlockSpec((None, S, HEAD_DIM), lambda b, h, i: (b, 0, 2 * H + h))],
        out_specs=pl.BlockSpec((None, tq, HEAD_DIM), lambda b, h, i: (b, i, h)),
        compiler_params=_params("parallel", "parallel", "arbitrary"),
        name="stick_breaking",
    )(u3, u3, u3)


def _swa_kernel(sink_ref, q_ref, kvp_ref, kvc_ref, o_ref, *, tq):
    g = pl.program_id(1)
    i = pl.program_id(2)
    dh = D_HEAD_DIM
    scale = dh ** -0.5
    kv = jnp.concatenate([kvp_ref[...], kvc_ref[...]], axis=0)
    k = kv[:, :dh]
    v = kv[:, dh:]
    r = lax.broadcasted_iota(jnp.int32, (tq, 2 * tq), 0)
    c = lax.broadcasted_iota(jnp.int32, (tq, 2 * tq), 1)
    rel = r - c + tq
    mask = (rel >= 0) & (rel < D_WINDOW) & ((c >= tq) | (i > 0))
    for hh in range(D_HPG):
        sk = sink_ref[g * D_HPG + hh]
        s = _dot_nt(q_ref[:, hh * dh:(hh + 1) * dh], k) * scale
        s = jnp.where(mask, s, NEG_INF)
        m = jnp.maximum(jnp.max(s, axis=-1, keepdims=True), sk)
        e = jnp.exp(s - m)
        p = e / (jnp.sum(e, axis=-1, keepdims=True) + jnp.exp(sk - m))
        o_ref[:, hh * dh:(hh + 1) * dh] = _dot(p.astype(BF16), v).astype(o_ref.dtype)


def sink_window(u3, sinks):
    B, S, _ = u3.shape
    tq = D_WINDOW
    gw = D_HPG * D_HEAD_DIM
    kv0 = (MIX_WIDTH + MEM_WIDTH) // (2 * D_HEAD_DIM)
    return pl.pallas_call(
        functools.partial(_swa_kernel, tq=tq),
        out_shape=jax.ShapeDtypeStruct((B, S, MIX_WIDTH), BF16),
        grid=(B, D_KV, S // tq),
        in_specs=[pl.BlockSpec(memory_space=pltpu.SMEM),
                  pl.BlockSpec((None, tq, gw), lambda b, g, i: (b, i, g)),
                  pl.BlockSpec((None, tq, 2 * D_HEAD_DIM),
                               lambda b, g, i: (b, jnp.maximum(i - 1, 0), kv0 + g)),
                  pl.BlockSpec((None, tq, 2 * D_HEAD_DIM), lambda b, g, i: (b, i, kv0 + g))],
        out_specs=pl.BlockSpec((None, tq, gw), lambda b, g, i: (b, i, g)),
        compiler_params=_params("parallel", "parallel", "arbitrary"),
        name="sink_window",
    )(sinks, u3, u3, u3)


def _conv_kernel(xp_ref, xc_ref, w_ref, b_ref, s_ref, o_ref, buf_ref, *, tc, halo):
    i = pl.program_id(1)
    prev = xp_ref[...]
    buf_ref[0:halo, :] = jnp.where(i > 0, prev, jnp.zeros_like(prev))
    buf_ref[halo:halo + tc, :] = xc_ref[...]
    acc = b_ref[...] + w_ref[C_CONV - 1:C_CONV, :] * xc_ref[...]
    for j in range(C_CONV - 1):
        sh = C_CONV - 1 - j
        acc = acc + w_ref[j:j + 1, :] * buf_ref[halo - sh:halo - sh + tc, :]
    y = acc * jax.nn.sigmoid(acc)
    o_ref[...] = (y * s_ref[...]).astype(o_ref.dtype)


def conv_silu(u3, conv_w, conv_b, col_scale, *, tc=256, tn=512):
    B, S, _ = u3.shape
    C = conv_w.shape[1]
    tc = min(tc, S)
    halo = 8
    return pl.pallas_call(
        functools.partial(_conv_kernel, tc=tc, halo=halo),
        out_shape=jax.ShapeDtypeStruct((B, S, C), BF16),
        grid=(B, S // tc, C // tn),
        in_specs=[pl.BlockSpec((None, halo, tn),
                               lambda b, i, j: (b, jnp.maximum(i * (tc // halo) - 1, 0), j)),
                  pl.BlockSpec((None, tc, tn), lambda b, i, j: (b, i, j)),
                  pl.BlockSpec((C_CONV, tn), lambda b, i, j: (0, j)),
                  pl.BlockSpec((1, tn), lambda b, i, j: (0, j)),
                  pl.BlockSpec((1, tn), lambda b, i, j: (0, j))],
        out_specs=pl.BlockSpec((None, tc, tn), lambda b, i, j: (b, i, j)),
        scratch_shapes=[pltpu.VMEM((tc + halo, tn), F32)],
        compiler_params=_params("parallel", "parallel", "parallel"),
        name="conv_silu",
    )(u3, u3, conv_w, conv_b.reshape(1, C), col_scale.reshape(1, C))


def _mlstm_kernel(q_ref, k_ref, v_ref, o_ref, gt_ref, gb_ref, hn_ref, out_ref,
                  state_ref, m_ref, *, L):
    H, dh = C_HEADS, C_HEAD_DIM
    ext = dh + LANES

    @pl.when(pl.program_id(1) == 0)
    def _():
        state_ref[...] = jnp.zeros_like(state_ref)
        m_ref[...] = jnp.zeros_like(m_ref)

    rows = lax.broadcasted_iota(jnp.int32, (L, L), 0)
    cols = lax.broadcasted_iota(jnp.int32, (L, L), 1)
    tril = rows >= cols
    eye = rows == cols
    gates = gt_ref[...] + gb_ref[...]
    log_f = jnp.minimum(gates, 0.0) - jnp.log1p(jnp.exp(-jnp.abs(gates)))
    f_hi, f_mid, f_lo = _split3(log_f)
    trilb = tril.astype(BF16)
    bsum = _dot(trilb, f_hi) + _dot(trilb, f_mid) + _dot(trilb, f_lo)
    ones_col = (lax.broadcasted_iota(jnp.int32, (L, LANES), 1) == 0).astype(F32)

    for h in range(H):
        sl = slice(h * dh, (h + 1) * dh)
        q = q_ref[:, sl]
        k = k_ref[:, sl]
        v_ext = jnp.concatenate([v_ref[:, sl], ones_col], axis=1)
        ic = gates[:, h:h + 1]
        bc = bsum[:, H + h:H + h + 1]
        m_prev = m_ref[h][0:1, 0:1]
        rc = ic - bc
        rrow = jnp.sum(jnp.where(eye, rc, 0.0), axis=0, keepdims=True)
        dm = jnp.where(tril, bc + rrow, NEG_INF)
        m_inter = bc + m_prev
        m_t = jnp.maximum(m_inter, jnp.max(dm, axis=-1, keepdims=True))
        w = jnp.exp(dm - m_t)
        a_inter = jnp.exp(m_inter - m_t)
        sqk = _dot_nt(q, k) * w
        state = state_ref[h]
        num = a_inter * _dot(q, state.astype(BF16)) + _dot(sqk.astype(BF16), v_ext.astype(BF16))
        den = jnp.maximum(jnp.abs(num[:, dh:dh + 1]), jnp.exp(-m_t))
        hid = num[:, :dh] / den
        hid = jax.nn.sigmoid(o_ref[:, sl]) * hid
        hid = hid * lax.rsqrt(jnp.mean(hid * hid, axis=-1, keepdims=True) + EPS)
        out_ref[:, sl] = (hid * hn_ref[:, sl]).astype(out_ref.dtype)

        b_last = bc[L - 1:L, :]
        gk = b_last - bc + ic
        m_new = jnp.maximum(b_last + m_prev, jnp.max(gk, axis=0, keepdims=True))
        wk = jnp.exp(gk - m_new)
        decay = jnp.exp(b_last + m_prev - m_new)
        state_ref[h] = decay * state + _dot_tn(k, (wk * v_ext).astype(BF16))
        m_ref[h] = jnp.broadcast_to(m_new, m_ref.shape[1:])


def mlstm(qk3, u3, gate_b, head_norm, *, L=C_CHUNK):
    B, S, _ = u3.shape
    W = MIX_WIDTH
    H, dh = C_HEADS, C_HEAD_DIM
    L = min(L, S)
    gblk = 4 * W // LANES
    gb = jnp.zeros((1, LANES), F32).at[0, :2 * H].set(gate_b)
    return pl.pallas_call(
        functools.partial(_mlstm_kernel, L=L),
        out_shape=jax.ShapeDtypeStruct((B, S, W), BF16),
        grid=(B, S // L),
        in_specs=[pl.BlockSpec((None, L, W), lambda b, c: (b, c, 0)),
                  pl.BlockSpec((None, L, W), lambda b, c: (b, c, 1)),
                  pl.BlockSpec((None, L, W), lambda b, c: (b, c, 2)),
                  pl.BlockSpec((None, L, W), lambda b, c: (b, c, 3)),
                  pl.BlockSpec((None, L, LANES), lambda b, c: (b, c, gblk)),
                  pl.BlockSpec((1, LANES), lambda b, c: (0, 0)),
                  pl.BlockSpec((1, W), lambda b, c: (0, 0))],
        out_specs=pl.BlockSpec((None, L, W), lambda b, c: (b, c, 0)),
        scratch_shapes=[pltpu.VMEM((H, dh, dh + LANES), F32),
                        pltpu.VMEM((H, 8, LANES), F32)],
        compiler_params=_params("parallel", "arbitrary"),
        name="mlstm",
    )(qk3, qk3, u3, u3, u3, gb, head_norm.reshape(1, W))


def _compress_kernel(x_ref, pe_ref, w1_ref, w2_ref, o_ref):
    half = w1_ref.shape[0] // 2
    x = x_ref[...]
    n = x.shape[0]
    first = _dot(x, w1_ref[0:half, :])
    second = _dot(x, w1_ref[half:, :])
    pe = jnp.broadcast_to(pe_ref[...], (8, pe_ref.shape[1])).astype(BF16)
    bias = _dot(pe, w1_ref[...])[0:1, :]
    hid = first + pltpu.roll(second, n - 1, 0) + bias
    o_ref[...] = _dot(jax.nn.gelu(hid).astype(BF16), w2_ref[...]).astype(o_ref.dtype)


def compress(xc, pe, w1, w2):
    _, G, B, n, width = xc.shape
    hid = w1.shape[2]
    return pl.pallas_call(
        _compress_kernel,
        out_shape=jax.ShapeDtypeStruct((2, G, B, n, HEAD_DIM), BF16),
        grid=(2, G, B),
        in_specs=[pl.BlockSpec((None, None, None, n, width), lambda j, g, b: (j, g, b, 0, 0)),
                  pl.BlockSpec((None, 1, 2 * width), lambda j, g, b: (j, 0, 0)),
                  pl.BlockSpec((None, 2 * width, hid), lambda j, g, b: (j, 0, 0)),
                  pl.BlockSpec((None, hid, HEAD_DIM), lambda j, g, b: (j, 0, 0))],
        out_specs=pl.BlockSpec((None, None, None, n, HEAD_DIM), lambda j, g, b: (j, g, b, 0, 0)),
        compiler_params=_params("parallel", "parallel", "parallel"),
        name="nsa_compress",
    )(xc, pe, w1, w2)


def _cmp_select_kernel(q_ref, kc_ref, vc_ref, o_ref, sel_ref, *, tq, n_cmp, n_slc):
    i = pl.program_id(2)
    scale = HEAD_DIM ** -0.5
    npad = kc_ref.shape[0]
    kc = kc_ref[...]
    vc = vc_ref[...]
    pos = i * tq + lax.broadcasted_iota(jnp.int32, (tq, npad), 0)
    nidx = lax.broadcasted_iota(jnp.int32, (tq, npad), 1)
    cmask = (nidx * A_CMP_STRIDE + (A_CMP_LEN - 1) <= pos) & (nidx < n_cmp)
    psum = jnp.zeros((tq, npad), F32)
    for h in range(A_HPG):
        sl = slice(h * HEAD_DIM, (h + 1) * HEAD_DIM)
        s = jnp.where(cmask, _dot_nt(q_ref[:, sl], kc) * scale, NEG_INF)
        m = jnp.max(s, axis=-1, keepdims=True)
        e = jnp.where(cmask, jnp.exp(s - m), 0.0)
        p = e / jnp.maximum(jnp.sum(e, axis=-1, keepdims=True), 1e-30)
        o_ref[:, sl] = _dot(p.astype(BF16), vc)
        psum = psum + p

    on = lax.broadcasted_iota(jnp.int32, (npad, LANES), 0) * A_CMP_STRIDE
    oj = lax.broadcasted_iota(jnp.int32, (npad, LANES), 1)
    overlap = ((on < (oj + 1) * A_SEL_LEN) & (on + A_CMP_LEN > oj * A_SEL_LEN)
               & (on < n_cmp * A_CMP_STRIDE) & (oj < n_slc)).astype(BF16)
    p_hi, p_lo = _split2(psum)
    imp = _dot(p_hi, overlap) + _dot(p_lo, overlap)

    blk = lax.broadcasted_iota(jnp.int32, (tq, LANES), 1)
    cur = (i * tq + lax.broadcasted_iota(jnp.int32, (tq, LANES), 0)) // A_SEL_LEN
    valid = blk <= cur
    forced = (blk == 0) | (blk == cur) | (blk == cur - 1)
    score = jnp.where(valid, jnp.where(forced, A_FORCE, imp), A_EXCLUDE)
    score = jnp.where(blk < n_slc, score, 4.0 * A_EXCLUDE)
    rank = jnp.zeros((tq, LANES), jnp.int32)
    for j in range(n_slc):
        other = score[:, j:j + 1]
        ahead = (other > score) | ((other == score) & (blk > j))
        rank = rank + ahead.astype(jnp.int32)
    sel = (rank < min(A_TOP_N, n_slc)) & valid
    sel_ref[...] = sel.astype(sel_ref.dtype)


def cmp_select(u3, kvc, *, tq=128):
    B, S, _ = u3.shape
    G = A_KV
    npad = kvc.shape[3]
    n_cmp = (S - A_CMP_LEN) // A_CMP_STRIDE + 1
    n_slc = S // A_SEL_LEN
    tq = min(tq, S)
    gw = A_HPG * HEAD_DIM
    return pl.pallas_call(
        functools.partial(_cmp_select_kernel, tq=tq, n_cmp=n_cmp, n_slc=n_slc),
        out_shape=(jax.ShapeDtypeStruct((B, S, MIX_WIDTH), F32),
                   jax.ShapeDtypeStruct((B, G, S, LANES), BF16)),
        grid=(B, G, S // tq),
        in_specs=[pl.BlockSpec((None, tq, gw), lambda b, g, i: (b, i, g)),
                  pl.BlockSpec((None, None, None, npad, HEAD_DIM), lambda b, g, i: (0, g, b, 0, 0)),
                  pl.BlockSpec((None, None, None, npad, HEAD_DIM), lambda b, g, i: (1, g, b, 0, 0))],
        out_specs=(pl.BlockSpec((None, tq, gw), lambda b, g, i: (b, i, g)),
                   pl.BlockSpec((None, None, tq, LANES), lambda b, g, i: (b, g, i, 0))),
        compiler_params=_params("parallel", "parallel", "parallel"),
        name="nsa_cmp_select",
    )(u3, kvc, kvc)


def _masked_scores(qs, k, mask, scale):
    hp = qs.shape[0] // mask.shape[0]
    s = _dot_nt(qs, k) * scale
    return jnp.where(mask[None], s.reshape((hp,) + mask.shape), NEG_INF).reshape(s.shape)


def _slc_win_kernel(q_ref, ks_ref, vs_ref, kw_ref, vw_ref, sel_ref, oc_ref, gt_ref, o_ref, *, tq, tk, wk):
    i = pl.program_id(2)
    S = ks_ref.shape[0]
    scale = HEAD_DIM ** -0.5
    hp = A_HPG
    qs = jnp.concatenate([q_ref[:, h * HEAD_DIM:(h + 1) * HEAD_DIM] for h in range(hp)], axis=0)
    sel = sel_ref[...]
    qpos = i * tq + lax.broadcasted_iota(jnp.int32, (tq, tk), 0)
    c = lax.broadcasted_iota(jnp.int32, (tq, tk), 1)
    eb = lax.broadcasted_iota(jnp.int32, (LANES, tk), 0)
    ec = lax.broadcasted_iota(jnp.int32, (LANES, tk), 1) // A_SEL_LEN

    def slc_body(j, carry):
        m, l, acc = carry
        off = pl.multiple_of(j * tk, tk)
        expand = (eb == j * (tk // A_SEL_LEN) + ec).astype(BF16)
        mask = (_dot(sel, expand) > 0.5) & (off + c <= qpos)
        s = _masked_scores(qs, ks_ref[pl.ds(off, tk), :], mask, scale)
        m_new = jnp.maximum(m, jnp.max(s, axis=-1, keepdims=True))
        p = jnp.exp(s - m_new)
        alpha = jnp.exp(m - m_new)
        l = alpha * l + jnp.sum(p, axis=-1, keepdims=True)
        acc = alpha * acc + _dot(p.astype(BF16), vs_ref[pl.ds(off, tk), :])
        return m_new, l, acc

    init = (jnp.full((hp * tq, 1), NEG_INF, F32), jnp.zeros((hp * tq, 1), F32),
            jnp.zeros((hp * tq, HEAD_DIM), F32))
    _, l_s, acc_s = lax.fori_loop(0, (i * tq + tq + tk - 1) // tk, slc_body, init)
    o_slc = acc_s / l_s

    start = pl.multiple_of(jnp.clip(i * tq + tq - wk, 0, S - wk), tq)
    wq = i * tq + lax.broadcasted_iota(jnp.int32, (tq, wk), 0)
    rel = wq - (start + lax.broadcasted_iota(jnp.int32, (tq, wk), 1))
    s = _masked_scores(qs, kw_ref[pl.ds(start, wk), :], (rel >= 0) & (rel < A_WINDOW), scale)
    p = jnp.exp(s - jnp.max(s, axis=-1, keepdims=True))
    o_win = _dot(p.astype(BF16), vw_ref[pl.ds(start, wk), :]) / jnp.sum(p, axis=-1, keepdims=True)

    gate = jax.nn.sigmoid(gt_ref[...])
    for h in range(hp):
        sl = slice(h * HEAD_DIM, (h + 1) * HEAD_DIM)
        rs = slice(h * tq, (h + 1) * tq)
        o = (gate[:, h:h + 1] * oc_ref[:, sl]
             + gate[:, hp + h:hp + h + 1] * o_slc[rs]
             + gate[:, 2 * hp + h:2 * hp + h + 1] * o_win[rs])
        o_ref[:, sl] = o.astype(o_ref.dtype)


def slc_win(u3, sel, o_cmp, gate3, *, tq=128, tk=512):
    B, S, _ = u3.shape
    G = A_KV
    tq = min(tq, S)
    tk = min(tk, S)
    wk = min(A_WINDOW + tq, S)
    assert S % tk == 0 and tk % tq == 0 and S // A_SEL_LEN <= LANES
    gw = A_HPG * HEAD_DIM
    kv0 = A_HEADS

    def kv_spec(which):
        return pl.BlockSpec((None, S, HEAD_DIM), lambda b, g, i: (b, 0, kv0 + which * G + g))

    return pl.pallas_call(
        functools.partial(_slc_win_kernel, tq=tq, tk=tk, wk=wk),
        out_shape=jax.ShapeDtypeStruct((B, S, MIX_WIDTH), BF16),
        grid=(B, G, S // tq),
        in_specs=[pl.BlockSpec((None, tq, gw), lambda b, g, i: (b, i, g)),
                  kv_spec(0), kv_spec(1), kv_spec(2), kv_spec(3),
                  pl.BlockSpec((None, None, tq, LANES), lambda b, g, i: (b, g, i, 0)),
                  pl.BlockSpec((None, tq, gw), lambda b, g, i: (b, i, g)),
                  pl.BlockSpec((None, tq, LANES), lambda b, g, i: (b, i, g))],
        out_specs=pl.BlockSpec((None, tq, gw), lambda b, g, i: (b, i, g)),
        compiler_params=_params("parallel", "parallel", "arbitrary"),
        name="nsa_slc_win",
    )(u3, u3, u3, u3, u3, sel, o_cmp, gate3)


def _pad_cols(w, n):
    return jnp.pad(w, ((0, 0), (0, n - w.shape[1])))


def nsa_layer_mix(x, B, S, norm, w_in, pe_k, w1_k, w2_k, pe_v, w1_v, w2_v, mem_kv):
    H, G, dh = A_HEADS, A_KV, HEAD_DIM
    qw = H * dh
    kvw = 6 * G * dh
    gw = 3 * H
    w_main = jnp.concatenate([w_in[:, :qw], w_in[:, qw + 2 * G * dh:qw + kvw],
                              w_in[:, qw + kvw + gw:]], axis=1).astype(BF16)
    w_cmp = w_in[:, qw:qw + 2 * G * dh].astype(BF16)
    wg = w_in[:, qw + kvw:qw + kvw + gw].reshape(-1, G, A_HPG, 3).transpose(0, 1, 3, 2)
    wg = jnp.pad(wg.reshape(-1, G, 3 * A_HPG), ((0, 0), (0, 0), (0, LANES - 3 * A_HPG)))
    wg = wg.reshape(-1, G * LANES).astype(BF16)

    u = rms_matmul(x, norm, w_main, BF16).reshape(B, S, -1)
    gate3 = rms_matmul(x, norm, wg, F32, tn=G * LANES).reshape(B, S, -1)
    xc = rms_matmul(x, norm, w_cmp, BF16, tn=dh, split_cols=True)
    xc = xc.reshape(2, G, B, S // A_CMP_STRIDE, A_CMP_STRIDE * dh)
    pe = jnp.stack([pe_k.reshape(1, -1), pe_v.reshape(1, -1)])
    kvc = compress(xc, pe, jnp.stack([w1_k, w1_v]).astype(BF16), jnp.stack([w2_k, w2_v]).astype(BF16))
    o_cmp, sel = cmp_select(u, kvc)
    mixed = slc_win(u, sel, o_cmp, gate3)
    mem_out = mem_attention(u, (qw + 4 * G * dh) // MEM_WIDTH, mem_kv)
    return mixed, mem_out


def stick_layer_mix(x, B, S, norm, w_in, mem_kv):
    u = rms_matmul(x, norm, w_in.astype(BF16), BF16).reshape(B, S, -1)
    mixed = stick_breaking(u)
    mem_out = mem_attention(u, 3 * MIX_WIDTH // MEM_WIDTH, mem_kv)
    return mixed, mem_out


def mlstm_layer_mix(x, B, S, norm, w_in, conv_w, conv_b, gate_b, head_norm, mem_kv):
    W = MIX_WIDTH
    nf = 4 * W + 2 * C_HEADS
    w_f = _pad_cols(w_in[:, :nf], 4 * W + LANES).astype(BF16)
    w_m = w_in[:, nf:].astype(BF16)
    u = rms_matmul(x, norm, w_f, F32, tn=LANES * 7).reshape(B, S, -1)
    um = rms_matmul(x, norm, w_m, BF16).reshape(B, S, -1)
    col_scale = jnp.concatenate([jnp.ones((W,), F32), jnp.full((W,), C_HEAD_DIM ** -0.5, F32)])
    qk = conv_silu(u, conv_w, conv_b, col_scale)
    mixed = mlstm(qk, u, gate_b, head_norm)
    mem_out = mem_attention(um, 0, mem_kv)
    return mixed, mem_out


def swa_layer_mix(x, B, S, norm, w_in, sinks, mem_kv):
    W = MIX_WIDTH
    kw = D_KV * D_HEAD_DIM
    wk = w_in[:, W:W + kw].reshape(-1, D_KV, D_HEAD_DIM)
    wv = w_in[:, W + kw:W + 2 * kw].reshape(-1, D_KV, D_HEAD_DIM)
    wkv = jnp.concatenate([wk, wv], axis=2).reshape(-1, 2 * kw)
    w_main = jnp.concatenate([w_in[:, :W], w_in[:, W + 2 * kw:], wkv], axis=1)
    w_main = _pad_cols(w_main, W + MEM_WIDTH + 4 * LANES).astype(BF16)
    u = rms_matmul(x, norm, w_main, BF16).reshape(B, S, -1)
    mixed = sink_window(u, sinks)
    mem_out = mem_attention(u, W // MEM_WIDTH, mem_kv)
    return mixed, mem_out


def kernel(x, mem, mem_norm, mem_w_kv, l0_norm_mix, l0_w_in, l0_cmp_pe_k, l0_cmp_w1_k, l0_cmp_w2_k, l0_cmp_pe_v, l0_cmp_w1_v, l0_cmp_w2_v, l0_w_out, l0_norm_ffn, l0_w_gate, l0_w_up, l0_w_down, l1_norm_mix, l1_w_in, l1_w_out, l1_norm_ffn, l1_w_gate, l1_w_up, l1_w_down, l2_norm_mix, l2_w_in, l2_conv_w, l2_conv_b, l2_gate_b, l2_head_norm, l2_w_out, l2_norm_ffn, l2_w_gate, l2_w_up, l2_w_down, l3_norm_mix, l3_w_in, l3_sinks, l3_w_out, l3_norm_ffn, l3_w_gate, l3_w_up, l3_w_down, final_norm):
    B, S, D = x.shape
    M = mem.shape[1]
    mem_kv = rms_matmul(mem.reshape(B * M, D), mem_norm, mem_w_kv.astype(BF16), BF16).reshape(B, M, -1)
    xs = x.reshape(B * S, D)

    def finish(xs, mixed, mem_out, w_out, norm_ffn, w_gate, w_up, w_down):
        w_out = w_out.astype(BF16)
        xs = out_proj(xs, mixed.reshape(B * S, -1), mem_out.reshape(B * S, -1),
                      w_out[:MIX_WIDTH], w_out[MIX_WIDTH:])
        return ffn(xs, norm_ffn, w_gate.astype(BF16), w_up.astype(BF16), w_down.astype(BF16))

    mixed, mem_out = nsa_layer_mix(xs, B, S, l0_norm_mix, l0_w_in, l0_cmp_pe_k, l0_cmp_w1_k, l0_cmp_w2_k,
                                   l0_cmp_pe_v, l0_cmp_w1_v, l0_cmp_w2_v, mem_kv)
    xs = finish(xs, mixed, mem_out, l0_w_out, l0_norm_ffn, l0_w_gate, l0_w_up, l0_w_down)
    mixed, mem_out = stick_layer_mix(xs, B, S, l1_norm_mix, l1_w_in, mem_kv)
    xs = finish(xs, mixed, mem_out, l1_w_out, l1_norm_ffn, l1_w_gate, l1_w_up, l1_w_down)
    mixed, mem_out = mlstm_layer_mix(xs, B, S, l2_norm_mix, l2_w_in, l2_conv_w, l2_conv_b, l2_gate_b,
                                     l2_head_norm, mem_kv)
    xs = finish(xs, mixed, mem_out, l2_w_out, l2_norm_ffn, l2_w_gate, l2_w_up, l2_w_down)
    mixed, mem_out = swa_layer_mix(xs, B, S, l3_norm_mix, l3_w_in, l3_sinks, mem_kv)
    xs = finish(xs, mixed, mem_out, l3_w_out, l3_norm_ffn, l3_w_gate, l3_w_up, l3_w_down)
    return rms_only(xs, final_norm).reshape(B, S, D)
```

```python
import functools

import jax
import jax.numpy as jnp
from jax import lax
from jax.experimental import pallas as pl
from jax.experimental.pallas import tpu as pltpu

F32 = jnp.float32
BF16 = jnp.bfloat16

LANES = 128
VMEM_LIMIT = 56 * 1024 * 1024

D_MODEL = 2048
HEAD_DIM = 128
MEM_HEADS = 4
MEM_WIDTH = MEM_HEADS * HEAD_DIM
MIX_WIDTH = D_MODEL - MEM_WIDTH
EPS = 1e-6
NEG_INF = -1e30

A_HEADS = MIX_WIDTH // HEAD_DIM
A_KV = 2
A_HPG = A_HEADS // A_KV
A_CMP_LEN = 32
A_CMP_STRIDE = 16
A_SEL_LEN = 64
A_TOP_N = 16
A_WINDOW = 512
A_FORCE = 1e6
A_EXCLUDE = -1e9

B_HEADS = MIX_WIDTH // HEAD_DIM

C_HEADS = 4
C_HEAD_DIM = MIX_WIDTH // C_HEADS
C_CONV = 4
C_CHUNK = 64

D_HEAD_DIM = 64
D_HEADS = MIX_WIDTH // D_HEAD_DIM
D_KV = D_HEADS // 8
D_HPG = D_HEADS // D_KV
D_WINDOW = 128


def _params(*sem):
    return pltpu.CompilerParams(dimension_semantics=sem, vmem_limit_bytes=VMEM_LIMIT)


def _dot(a, b):
    return jnp.dot(a, b, preferred_element_type=F32)


def _dot_nt(a, b):
    return lax.dot_general(a, b, (((1,), (1,)), ((), ())), preferred_element_type=F32)


def _dot_tn(a, b):
    return lax.dot_general(a, b, (((0,), (0,)), ((), ())), preferred_element_type=F32)


def _split2(x):
    hi = x.astype(BF16)
    lo = (x - hi.astype(F32)).astype(BF16)
    return hi, lo


def _split3(x):
    hi = x.astype(BF16)
    r = x - hi.astype(F32)
    mid = r.astype(BF16)
    lo = (r - mid.astype(F32)).astype(BF16)
    return hi, mid, lo


PROJ_CHUNK = 1024


def _rms_proj_kernel(x_ref, g_ref, w_ref, *o_refs, outs):
    x = x_ref[...]
    r = lax.rsqrt(jnp.mean(x * x, axis=-1, keepdims=True) + EPS)
    h = (x * r * g_ref[...]).astype(BF16)
    for o_ref, (start, width, split) in zip(o_refs, outs):
        step = split if split else min(PROJ_CHUNK, width)
        for n, c0 in enumerate(range(0, width, step)):
            cw = min(step, width - c0)
            y = _dot(h, w_ref[:, start + c0:start + c0 + cw]).astype(o_ref.dtype)
            if split:
                o_ref[n] = y
            else:
                o_ref[:, c0:c0 + cw] = y


def rms_proj(x, g, w, outs, *, tm=512):
    T, D = x.shape
    N = w.shape[1]
    tm = min(tm, T)
    shapes, specs = [], []
    for start, width, dtype, split in outs:
        if split:
            shapes.append(jax.ShapeDtypeStruct((width // split, T, split), dtype))
            specs.append(pl.BlockSpec((width // split, tm, split), lambda i: (0, i, 0)))
        else:
            shapes.append(jax.ShapeDtypeStruct((T, width), dtype))
            specs.append(pl.BlockSpec((tm, width), lambda i: (i, 0)))
    return pl.pallas_call(
        functools.partial(_rms_proj_kernel, outs=tuple((s, wd, sp) for s, wd, _, sp in outs)),
        out_shape=tuple(shapes),
        grid=(T // tm,),
        in_specs=[pl.BlockSpec((tm, D), lambda i: (i, 0)),
                  pl.BlockSpec((1, D), lambda i: (0, 0)),
                  pl.BlockSpec((D, N), lambda i: (0, 0), pipeline_mode=pl.Buffered(1))],
        out_specs=tuple(specs),
        compiler_params=_params("parallel"),
        name="rms_proj",
    )(x, g.reshape(1, D), w)


def _ffn_kernel(x_ref, y1_ref, y2_ref, wo1_ref, wo2_ref, g_ref, wg_ref, wu_ref, wd_ref, o_ref, h_ref):
    @pl.when(pl.program_id(1) == 0)
    def _():
        x = x_ref[...] + _dot(y1_ref[...], wo1_ref[...]) + _dot(y2_ref[...], wo2_ref[...])
        r = lax.rsqrt(jnp.mean(x * x, axis=-1, keepdims=True) + EPS)
        h_ref[...] = (x * r * g_ref[...]).astype(BF16)
        o_ref[...] = x

    h = h_ref[...]
    a = _dot(h, wg_ref[...])
    u = _dot(h, wu_ref[...])
    act = (a * jax.nn.sigmoid(a) * u).astype(BF16)
    o_ref[...] += _dot(act, wd_ref[...])


def out_proj_ffn(x, y1, y2, wo1, wo2, g, wg, wu, wd, *, tm=512, tf=512):
    T, D = x.shape
    K1, K2 = y1.shape[1], y2.shape[1]
    FF = wg.shape[1]
    tm = min(tm, T)
    assert FF % tf == 0
    once = pl.Buffered(1)
    return pl.pallas_call(
        _ffn_kernel,
        out_shape=jax.ShapeDtypeStruct((T, D), F32),
        grid=(T // tm, FF // tf),
        in_specs=[pl.BlockSpec((tm, D), lambda i, f: (i, 0)),
                  pl.BlockSpec((tm, K1), lambda i, f: (i, 0)),
                  pl.BlockSpec((tm, K2), lambda i, f: (i, 0)),
                  pl.BlockSpec((K1, D), lambda i, f: (0, 0), pipeline_mode=once),
                  pl.BlockSpec((K2, D), lambda i, f: (0, 0), pipeline_mode=once),
                  pl.BlockSpec((1, D), lambda i, f: (0, 0)),
                  pl.BlockSpec((D, tf), lambda i, f: (0, f)),
                  pl.BlockSpec((D, tf), lambda i, f: (0, f)),
                  pl.BlockSpec((tf, D), lambda i, f: (f, 0))],
        out_specs=pl.BlockSpec((tm, D), lambda i, f: (i, 0)),
        scratch_shapes=[pltpu.VMEM((tm, D), BF16)],
        compiler_params=_params("parallel", "arbitrary"),
        name="out_proj_ffn",
    )(x, y1, y2, wo1, wo2, g.reshape(1, D), wg, wu, wd)


def _rms_kernel(x_ref, g_ref, o_ref):
    x = x_ref[...]
    r = lax.rsqrt(jnp.mean(x * x, axis=-1, keepdims=True) + EPS)
    o_ref[...] = x * r * g_ref[...]


def rms_only(x, g, *, tm=512):
    T, D = x.shape
    tm = min(tm, T)
    return pl.pallas_call(
        _rms_kernel,
        out_shape=jax.ShapeDtypeStruct((T, D), F32),
        grid=(T // tm,),
        in_specs=[pl.BlockSpec((tm, D), lambda i: (i, 0)),
                  pl.BlockSpec((1, D), lambda i: (0, 0))],
        out_specs=pl.BlockSpec((tm, D), lambda i: (i, 0)),
        compiler_params=_params("parallel"),
        name="final_norm",
    )(x, g.reshape(1, D))


def _mem_attn_kernel(q_ref, k_ref, v_ref, o_ref):
    scale = HEAD_DIM ** -0.5
    for h in range(MEM_HEADS):
        sl = slice(h * HEAD_DIM, (h + 1) * HEAD_DIM)
        s = _dot_nt(q_ref[:, sl], k_ref[:, sl]) * scale
        m = jnp.max(s, axis=-1, keepdims=True)
        e = jnp.exp(s - m)
        p = e / jnp.sum(e, axis=-1, keepdims=True)
        o_ref[:, sl] = _dot(p.astype(BF16), v_ref[:, sl]).astype(o_ref.dtype)


def mem_attention(u3, qblk, mem_kv, *, tq=512):
    B, S, _ = u3.shape
    M = mem_kv.shape[1]
    tq = min(tq, S)
    return pl.pallas_call(
        _mem_attn_kernel,
        out_shape=jax.ShapeDtypeStruct((B, S, MEM_WIDTH), BF16),
        grid=(B, S // tq),
        in_specs=[pl.BlockSpec((None, tq, MEM_WIDTH), lambda b, i: (b, i, qblk)),
                  pl.BlockSpec((None, M, MEM_WIDTH), lambda b, i: (b, 0, 0)),
                  pl.BlockSpec((None, M, MEM_WIDTH), lambda b, i: (b, 0, 1))],
        out_specs=pl.BlockSpec((None, tq, MEM_WIDTH), lambda b, i: (b, i, 0)),
        compiler_params=_params("parallel", "parallel"),
        name="mem_attention",
    )(u3, mem_kv, mem_kv)


def _stick_kernel(q_ref, k_ref, v_ref, o_ref, acc_ref, tail_ref, *, tq, tk):
    i = pl.program_id(2)
    scale = HEAD_DIM ** -0.5
    per_q = tq // tk
    nh = q_ref.shape[1] // HEAD_DIM
    lrow = lax.broadcasted_iota(jnp.int32, (2 * tk, tk), 0)
    lcol = lax.broadcasted_iota(jnp.int32, (2 * tk, tk), 1)
    later = ((lrow > lcol) & ((lrow < tk) | (lrow > lcol + tk))).astype(BF16)
    rows = lax.broadcasted_iota(jnp.int32, (tq, tk), 0)
    cols = lax.broadcasted_iota(jnp.int32, (tq, tk), 1)
    acc_ref[...] = jnp.zeros_like(acc_ref)
    tail_ref[...] = jnp.zeros_like(tail_ref)

    def step(j, diag):
        off = pl.multiple_of(j * tk, tk)
        for hh in range(nh):
            sl = slice(hh * HEAD_DIM, (hh + 1) * HEAD_DIM)
            z = _dot_nt(q_ref[:, sl], k_ref[pl.ds(off, tk), sl]) * scale
            log_keep = -(jnp.maximum(z, 0.0) + jnp.log(1.0 + jnp.exp(-jnp.abs(z))))
            if diag is not None:
                causal = cols + diag * tk < rows
                log_keep = jnp.where(causal, log_keep, 0.0)
            hi, lo = _split2(log_keep)
            after = _dot(jnp.concatenate([hi, lo], axis=1), later) + tail_ref[hh]
            a = jnp.exp(z + log_keep + after)
            if diag is not None:
                a = jnp.where(causal, a, 0.0)
            acc_ref[hh] += _dot(a.astype(BF16), v_ref[pl.ds(off, tk), sl])
            tail_ref[hh] = after[:, 0:1] + log_keep[:, 0:1]

    for d in reversed(range(per_q)):
        step(i * per_q + d, d)

    @pl.loop(0, i * per_q)
    def _(jj):
        step(i * per_q - 1 - jj, None)

    for hh in range(nh):
        o_ref[:, hh * HEAD_DIM:(hh + 1) * HEAD_DIM] = acc_ref[hh].astype(o_ref.dtype)


def stick_breaking(u3, *, tq=512, tk=256, nh=2):
    B, S, _ = u3.shape
    H = B_HEADS // nh
    tq = min(tq, S)
    tk = min(tk, tq)
    hw = nh * HEAD_DIM
    return pl.pallas_call(
        functools.partial(_stick_kernel, tq=tq, tk=tk),
        out_shape=jax.ShapeDtypeStruct((B, S, MIX_WIDTH), BF16),
        grid=(B, H, S // tq),
        in_specs=[pl.BlockSpec((None, tq, hw), lambda b, h, i: (b, i, h)),
                  pl.BlockSpec((None, S, hw), lambda b, h, i: (b, 0, H + h)),
                  pl.BlockSpec((None, S, hw), lambda b, h, i: (b, 0, 2 * H + h))],
        out_specs=pl.BlockSpec((None, tq, hw), lambda b, h, i: (b, i, h)),
        scratch_shapes=[pltpu.VMEM((nh, tq, HEAD_DIM), F32), pltpu.VMEM((nh, tq, 1), F32)],
        compiler_params=_params("parallel", "parallel", "arbitrary"),
        name="stick_breaking",
    )(u3, u3, u3)


def _swa_kernel(sink_ref, q_ref, kvp_ref, kvc_ref, o_ref, *, tq):
    g = pl.program_id(1)
    i = pl.program_id(2)
    dh = D_HEAD_DIM
    scale = dh ** -0.5
    kv = jnp.concatenate([kvp_ref[...], kvc_ref[...]], axis=0)
    k = kv[:, :dh]
    v = kv[:, dh:]
    r = lax.broadcasted_iota(jnp.int32, (tq, 2 * tq), 0)
    c = lax.broadcasted_iota(jnp.int32, (tq, 2 * tq), 1)
    rel = r - c + tq
    mask = (rel >= 0) & (rel < D_WINDOW) & ((c >= tq) | (i > 0))
    for hh in range(D_HPG):
        sk = sink_ref[g * D_HPG + hh]
        s = _dot_nt(q_ref[:, hh * dh:(hh + 1) * dh], k) * scale
        s = jnp.where(mask, s, NEG_INF)
        m = jnp.maximum(jnp.max(s, axis=-1, keepdims=True), sk)
        e = jnp.exp(s - m)
        p = e / (jnp.sum(e, axis=-1, keepdims=True) + jnp.exp(sk - m))
        o_ref[:, hh * dh:(hh + 1) * dh] = _dot(p.astype(BF16), v).astype(o_ref.dtype)


def sink_window(u3, sinks):
    B, S, _ = u3.shape
    tq = D_WINDOW
    gw = D_HPG * D_HEAD_DIM
    kv0 = (MIX_WIDTH + MEM_WIDTH) // (2 * D_HEAD_DIM)
    return pl.pallas_call(
        functools.partial(_swa_kernel, tq=tq),
        out_shape=jax.ShapeDtypeStruct((B, S, MIX_WIDTH), BF16),
        grid=(B, D_KV, S // tq),
        in_specs=[pl.BlockSpec(memory_space=pltpu.SMEM),
                  pl.BlockSpec((None, tq, gw), lambda b, g, i: (b, i, g)),
                  pl.BlockSpec((None, tq, 2 * D_HEAD_DIM),
                               lambda b, g, i: (b, jnp.maximum(i - 1, 0), kv0 + g)),
                  pl.BlockSpec((None, tq, 2 * D_HEAD_DIM), lambda b, g, i: (b, i, kv0 + g))],
        out_specs=pl.BlockSpec((None, tq, gw), lambda b, g, i: (b, i, g)),
        compiler_params=_params("parallel", "parallel", "arbitrary"),
        name="sink_window",
    )(sinks, u3, u3, u3)


def _conv_kernel(xp_ref, xc_ref, w_ref, b_ref, s_ref, o_ref, buf_ref, *, tc, halo):
    i = pl.program_id(1)
    prev = xp_ref[...]
    buf_ref[0:halo, :] = jnp.where(i > 0, prev, jnp.zeros_like(prev))
    buf_ref[halo:halo + tc, :] = xc_ref[...]
    acc = b_ref[...] + w_ref[C_CONV - 1:C_CONV, :] * xc_ref[...]
    for j in range(C_CONV - 1):
        sh = C_CONV - 1 - j
        acc = acc + w_ref[j:j + 1, :] * buf_ref[halo - sh:halo - sh + tc, :]
    y = acc * jax.nn.sigmoid(acc)
    o_ref[...] = (y * s_ref[...]).astype(o_ref.dtype)


def conv_silu(u3, conv_w, conv_b, col_scale, *, tc=256, tn=512):
    B, S, _ = u3.shape
    C = conv_w.shape[1]
    tc = min(tc, S)
    halo = 8
    return pl.pallas_call(
        functools.partial(_conv_kernel, tc=tc, halo=halo),
        out_shape=jax.ShapeDtypeStruct((B, S, C), BF16),
        grid=(B, S // tc, C // tn),
        in_specs=[pl.BlockSpec((None, halo, tn),
                               lambda b, i, j: (b, jnp.maximum(i * (tc // halo) - 1, 0), j)),
                  pl.BlockSpec((None, tc, tn), lambda b, i, j: (b, i, j)),
                  pl.BlockSpec((C_CONV, tn), lambda b, i, j: (0, j)),
                  pl.BlockSpec((1, tn), lambda b, i, j: (0, j)),
                  pl.BlockSpec((1, tn), lambda b, i, j: (0, j))],
        out_specs=pl.BlockSpec((None, tc, tn), lambda b, i, j: (b, i, j)),
        scratch_shapes=[pltpu.VMEM((tc + halo, tn), F32)],
        compiler_params=_params("parallel", "parallel", "parallel"),
        name="conv_silu",
    )(u3, u3, conv_w, conv_b.reshape(1, C), col_scale.reshape(1, C))


def _mlstm_kernel(q_ref, k_ref, v_ref, o_ref, gt_ref, gb_ref, hn_ref, out_ref,
                  state_ref, m_ref, *, L):
    H, dh = C_HEADS, C_HEAD_DIM
    ext = dh + LANES

    @pl.when(pl.program_id(1) == 0)
    def _():
        state_ref[...] = jnp.zeros_like(state_ref)
        m_ref[...] = jnp.zeros_like(m_ref)

    rows = lax.broadcasted_iota(jnp.int32, (L, L), 0)
    cols = lax.broadcasted_iota(jnp.int32, (L, L), 1)
    tril = rows >= cols
    eye = rows == cols
    gates = gt_ref[...] + gb_ref[...]
    log_f = jnp.minimum(gates, 0.0) - jnp.log1p(jnp.exp(-jnp.abs(gates)))
    f_hi, f_mid, f_lo = _split3(log_f)
    trilb = tril.astype(BF16)
    bsum = _dot(trilb, f_hi) + _dot(trilb, f_mid) + _dot(trilb, f_lo)
    ones_col = (lax.broadcasted_iota(jnp.int32, (L, LANES), 1) == 0).astype(F32)

    for h in range(H):
        sl = slice(h * dh, (h + 1) * dh)
        q = q_ref[:, sl]
        k = k_ref[:, sl]
        v_ext = jnp.concatenate([v_ref[:, sl], ones_col], axis=1)
        ic = gates[:, h:h + 1]
        bc = bsum[:, H + h:H + h + 1]
        m_prev = m_ref[h][0:1, 0:1]
        rc = ic - bc
        rrow = jnp.sum(jnp.where(eye, rc, 0.0), axis=0, keepdims=True)
        dm = jnp.where(tril, bc + rrow, NEG_INF)
        m_inter = bc + m_prev
        m_t = jnp.maximum(m_inter, jnp.max(dm, axis=-1, keepdims=True))
        w = jnp.exp(dm - m_t)
        a_inter = jnp.exp(m_inter - m_t)
        sqk = _dot_nt(q, k) * w
        state = state_ref[h]
        num = a_inter * _dot(q, state.astype(BF16)) + _dot(sqk.astype(BF16), v_ext.astype(BF16))
        den = jnp.maximum(jnp.abs(num[:, dh:dh + 1]), jnp.exp(-m_t))
        hid = num[:, :dh] / den
        hid = jax.nn.sigmoid(o_ref[:, sl]) * hid
        hid = hid * lax.rsqrt(jnp.mean(hid * hid, axis=-1, keepdims=True) + EPS)
        out_ref[:, sl] = (hid * hn_ref[:, sl]).astype(out_ref.dtype)

        b_last = bc[L - 1:L, :]
        gk = b_last - bc + ic
        m_new = jnp.maximum(b_last + m_prev, jnp.max(gk, axis=0, keepdims=True))
        wk = jnp.exp(gk - m_new)
        decay = jnp.exp(b_last + m_prev - m_new)
        state_ref[h] = decay * state + _dot_tn(k, (wk * v_ext).astype(BF16))
        m_ref[h] = jnp.broadcast_to(m_new, m_ref.shape[1:])


def mlstm(qk3, u3, gate_b, head_norm, *, L=C_CHUNK):
    B, S, _ = u3.shape
    W = MIX_WIDTH
    H, dh = C_HEADS, C_HEAD_DIM
    L = min(L, S)
    gblk = 4 * W // LANES
    gb = jnp.zeros((1, LANES), F32).at[0, :2 * H].set(gate_b)
    return pl.pallas_call(
        functools.partial(_mlstm_kernel, L=L),
        out_shape=jax.ShapeDtypeStruct((B, S, W), BF16),
        grid=(B, S // L),
        in_specs=[pl.BlockSpec((None, L, W), lambda b, c: (b, c, 0)),
                  pl.BlockSpec((None, L, W), lambda b, c: (b, c, 1)),
                  pl.BlockSpec((None, L, W), lambda b, c: (b, c, 2)),
                  pl.BlockSpec((None, L, W), lambda b, c: (b, c, 3)),
                  pl.BlockSpec((None, L, LANES), lambda b, c: (b, c, gblk)),
                  pl.BlockSpec((1, LANES), lambda b, c: (0, 0)),
                  pl.BlockSpec((1, W), lambda b, c: (0, 0))],
        out_specs=pl.BlockSpec((None, L, W), lambda b, c: (b, c, 0)),
        scratch_shapes=[pltpu.VMEM((H, dh, dh + LANES), F32),
                        pltpu.VMEM((H, 8, LANES), F32)],
        compiler_params=_params("parallel", "arbitrary"),
        name="mlstm",
    )(qk3, qk3, u3, u3, u3, gb, head_norm.reshape(1, W))


def _compress_kernel(x_ref, pe_ref, w1_ref, w2_ref, o_ref):
    half = w1_ref.shape[0] // 2
    x = x_ref[...]
    n = x.shape[0]
    first = _dot(x, w1_ref[0:half, :])
    second = _dot(x, w1_ref[half:, :])
    pe = jnp.broadcast_to(pe_ref[...], (8, pe_ref.shape[1])).astype(BF16)
    bias = _dot(pe, w1_ref[...])[0:1, :]
    hid = first + pltpu.roll(second, n - 1, 0) + bias
    o_ref[...] = _dot(jax.nn.gelu(hid).astype(BF16), w2_ref[...]).astype(o_ref.dtype)


def compress(xc, pe, w1, w2):
    _, G, B, n, width = xc.shape
    hid = w1.shape[2]
    return pl.pallas_call(
        _compress_kernel,
        out_shape=jax.ShapeDtypeStruct((2, G, B, n, HEAD_DIM), BF16),
        grid=(2, G, B),
        in_specs=[pl.BlockSpec((None, None, None, n, width), lambda j, g, b: (j, g, b, 0, 0)),
                  pl.BlockSpec((None, 1, 2 * width), lambda j, g, b: (j, 0, 0)),
                  pl.BlockSpec((None, 2 * width, hid), lambda j, g, b: (j, 0, 0)),
                  pl.BlockSpec((None, hid, HEAD_DIM), lambda j, g, b: (j, 0, 0))],
        out_specs=pl.BlockSpec((None, None, None, n, HEAD_DIM), lambda j, g, b: (j, g, b, 0, 0)),
        compiler_params=_params("parallel", "parallel", "parallel"),
        name="nsa_compress",
    )(xc, pe, w1, w2)


def _cmp_select_kernel(q_ref, kc_ref, vc_ref, o_ref, sel_ref, *, tq, n_cmp, n_slc):
    i = pl.program_id(2)
    scale = HEAD_DIM ** -0.5
    npad = kc_ref.shape[0]
    kc = kc_ref[...]
    vc = vc_ref[...]
    pos = i * tq + lax.broadcasted_iota(jnp.int32, (tq, npad), 0)
    nidx = lax.broadcasted_iota(jnp.int32, (tq, npad), 1)
    cmask = (nidx * A_CMP_STRIDE + (A_CMP_LEN - 1) <= pos) & (nidx < n_cmp)
    psum = jnp.zeros((tq, npad), F32)
    for h in range(A_HPG):
        sl = slice(h * HEAD_DIM, (h + 1) * HEAD_DIM)
        s = jnp.where(cmask, _dot_nt(q_ref[:, sl], kc) * scale, NEG_INF)
        m = jnp.max(s, axis=-1, keepdims=True)
        e = jnp.where(cmask, jnp.exp(s - m), 0.0)
        p = e / jnp.maximum(jnp.sum(e, axis=-1, keepdims=True), 1e-30)
        o_ref[:, sl] = _dot(p.astype(BF16), vc)
        psum = psum + p

    on = lax.broadcasted_iota(jnp.int32, (npad, LANES), 0) * A_CMP_STRIDE
    oj = lax.broadcasted_iota(jnp.int32, (npad, LANES), 1)
    overlap = ((on < (oj + 1) * A_SEL_LEN) & (on + A_CMP_LEN > oj * A_SEL_LEN)
               & (on < n_cmp * A_CMP_STRIDE) & (oj < n_slc)).astype(BF16)
    p_hi, p_lo = _split2(psum)
    imp = _dot(p_hi, overlap) + _dot(p_lo, overlap)

    sub = 8
    blk = lax.broadcasted_iota(jnp.int32, (LANES, tq), 0)
    cur = (i * tq + lax.broadcasted_iota(jnp.int32, (LANES, tq), 1)) // A_SEL_LEN
    valid = blk <= cur
    forced = (blk == 0) | (blk == cur) | (blk == cur - 1)
    score = jnp.where(valid, jnp.where(forced, A_FORCE, imp.T), A_EXCLUDE)
    row8 = lax.broadcasted_iota(jnp.int32, (sub, tq), 0)
    n_top = min(A_TOP_N, n_slc)
    chosen = []
    for v in range(0, n_slc, sub):
        mine = score[v:v + sub, :]
        rank = jnp.zeros((sub, tq), jnp.int32)
        for j in range(n_slc):
            other = score[j:j + 1, :]
            if v > j:
                ahead = other >= mine
            elif v + sub - 1 <= j:
                ahead = other > mine
            else:
                ahead = (other > mine) | ((other == mine) & (row8 + v > j))
            rank = rank + ahead.astype(jnp.int32)
        chosen.append(((rank < n_top) & valid[v:v + sub, :]).astype(F32))
    if n_slc < LANES:
        chosen.append(jnp.zeros((LANES - n_slc, tq), F32))
    sel_ref[...] = jnp.concatenate(chosen, axis=0).T.astype(sel_ref.dtype)


def cmp_select(u3, kvc, *, tq=128):
    B, S, _ = u3.shape
    G = A_KV
    npad = kvc.shape[3]
    n_cmp = (S - A_CMP_LEN) // A_CMP_STRIDE + 1
    n_slc = S // A_SEL_LEN
    tq = min(tq, S)
    gw = A_HPG * HEAD_DIM
    return pl.pallas_call(
        functools.partial(_cmp_select_kernel, tq=tq, n_cmp=n_cmp, n_slc=n_slc),
        out_shape=(jax.ShapeDtypeStruct((B, S, MIX_WIDTH), F32),
                   jax.ShapeDtypeStruct((B, G, S, LANES), BF16)),
        grid=(B, G, S // tq),
        in_specs=[pl.BlockSpec((None, tq, gw), lambda b, g, i: (b, i, g)),
                  pl.BlockSpec((None, None, None, npad, HEAD_DIM), lambda b, g, i: (0, g, b, 0, 0)),
                  pl.BlockSpec((None, None, None, npad, HEAD_DIM), lambda b, g, i: (1, g, b, 0, 0))],
        out_specs=(pl.BlockSpec((None, tq, gw), lambda b, g, i: (b, i, g)),
                   pl.BlockSpec((None, None, tq, LANES), lambda b, g, i: (b, g, i, 0))),
        compiler_params=_params("parallel", "parallel", "parallel"),
        name="nsa_cmp_select",
    )(u3, kvc, kvc)


def _masked_scores(qs, k, mask, scale):
    hp = qs.shape[0] // mask.shape[0]
    s = _dot_nt(qs, k) * scale
    return jnp.where(mask[None], s.reshape((hp,) + mask.shape), NEG_INF).reshape(s.shape)


def _slc_win_kernel(q_ref, ks_ref, vs_ref, kw_ref, vw_ref, sel_ref, oc_ref, gt_ref, o_ref, *, tq, tk, wk):
    i = pl.program_id(2)
    S = ks_ref.shape[0]
    scale = HEAD_DIM ** -0.5
    hp = A_HPG
    qs = jnp.concatenate([q_ref[:, h * HEAD_DIM:(h + 1) * HEAD_DIM] for h in range(hp)], axis=0)
    sel = sel_ref[...]
    qpos = i * tq + lax.broadcasted_iota(jnp.int32, (tq, tk), 0)
    c = lax.broadcasted_iota(jnp.int32, (tq, tk), 1)
    eb = lax.broadcasted_iota(jnp.int32, (LANES, tk), 0)
    ec = lax.broadcasted_iota(jnp.int32, (LANES, tk), 1) // A_SEL_LEN

    def slc_body(j, carry):
        m, l, acc = carry
        off = pl.multiple_of(j * tk, tk)
        expand = (eb == j * (tk // A_SEL_LEN) + ec).astype(BF16)
        mask = (_dot(sel, expand) > 0.5) & (off + c <= qpos)
        s = _masked_scores(qs, ks_ref[pl.ds(off, tk), :], mask, scale)
        m_new = jnp.maximum(m, jnp.max(s, axis=-1, keepdims=True))
        p = jnp.exp(s - m_new)
        alpha = jnp.exp(m - m_new)
        l = alpha * l + jnp.sum(p, axis=-1, keepdims=True)
        acc = alpha * acc + _dot(p.astype(BF16), vs_ref[pl.ds(off, tk), :])
        return m_new, l, acc

    init = (jnp.full((hp * tq, 1), NEG_INF, F32), jnp.zeros((hp * tq, 1), F32),
            jnp.zeros((hp * tq, HEAD_DIM), F32))
    _, l_s, acc_s = lax.fori_loop(0, (i * tq + tq + tk - 1) // tk, slc_body, init)
    o_slc = acc_s / l_s

    start = pl.multiple_of(jnp.clip(i * tq + tq - wk, 0, S - wk), tq)
    wq = i * tq + lax.broadcasted_iota(jnp.int32, (tq, wk), 0)
    rel = wq - (start + lax.broadcasted_iota(jnp.int32, (tq, wk), 1))
    s = _masked_scores(qs, kw_ref[pl.ds(start, wk), :], (rel >= 0) & (rel < A_WINDOW), scale)
    p = jnp.exp(s - jnp.max(s, axis=-1, keepdims=True))
    o_win = _dot(p.astype(BF16), vw_ref[pl.ds(start, wk), :]) / jnp.sum(p, axis=-1, keepdims=True)

    gate = jax.nn.sigmoid(gt_ref[...])
    for h in range(hp):
        sl = slice(h * HEAD_DIM, (h + 1) * HEAD_DIM)
        rs = slice(h * tq, (h + 1) * tq)
        o = (gate[:, h:h + 1] * oc_ref[:, sl]
             + gate[:, hp + h:hp + h + 1] * o_slc[rs]
             + gate[:, 2 * hp + h:2 * hp + h + 1] * o_win[rs])
        o_ref[:, sl] = o.astype(o_ref.dtype)


def slc_win(u3, sel, o_cmp, gate3, *, tq=128, tk=512):
    B, S, _ = u3.shape
    G = A_KV
    tq = min(tq, S)
    tk = min(tk, S)
    wk = min(A_WINDOW + tq, S)
    assert S % tk == 0 and tk % tq == 0 and S // A_SEL_LEN <= LANES
    gw = A_HPG * HEAD_DIM
    kv0 = A_HEADS

    def kv_spec(which):
        return pl.BlockSpec((None, S, HEAD_DIM), lambda b, g, i: (b, 0, kv0 + which * G + g))

    return pl.pallas_call(
        functools.partial(_slc_win_kernel, tq=tq, tk=tk, wk=wk),
        out_shape=jax.ShapeDtypeStruct((B, S, MIX_WIDTH), BF16),
        grid=(B, G, S // tq),
        in_specs=[pl.BlockSpec((None, tq, gw), lambda b, g, i: (b, i, g)),
                  kv_spec(0), kv_spec(1), kv_spec(2), kv_spec(3),
                  pl.BlockSpec((None, None, tq, LANES), lambda b, g, i: (b, g, i, 0)),
                  pl.BlockSpec((None, tq, gw), lambda b, g, i: (b, i, g)),
                  pl.BlockSpec((None, tq, LANES), lambda b, g, i: (b, i, g))],
        out_specs=pl.BlockSpec((None, tq, gw), lambda b, g, i: (b, i, g)),
        compiler_params=_params("parallel", "parallel", "arbitrary"),
        name="nsa_slc_win",
    )(u3, u3, u3, u3, u3, sel, o_cmp, gate3)


def _pad_cols(w, n):
    return jnp.pad(w, ((0, 0), (0, n - w.shape[1])))


def nsa_layer_mix(x, B, S, norm, w_in, pe_k, w1_k, w2_k, pe_v, w1_v, w2_v, mem_kv):
    H, G, dh = A_HEADS, A_KV, HEAD_DIM
    qw = H * dh
    kvw = 6 * G * dh
    gw = 3 * H
    wg = w_in[:, qw + kvw:qw + kvw + gw].reshape(-1, G, A_HPG, 3).transpose(0, 1, 3, 2)
    wg = jnp.pad(wg.reshape(-1, G, 3 * A_HPG), ((0, 0), (0, 0), (0, LANES - 3 * A_HPG)))
    n_main = qw + 4 * G * dh + MEM_WIDTH
    n_cmp = 2 * G * dh
    w_all = jnp.concatenate([w_in[:, :qw], w_in[:, qw + n_cmp:qw + kvw], w_in[:, qw + kvw + gw:],
                             w_in[:, qw:qw + n_cmp], wg.reshape(-1, G * LANES)], axis=1).astype(BF16)

    u, xc, gate3 = rms_proj(x, norm, w_all, [(0, n_main, BF16, None), (n_main, n_cmp, BF16, dh),
                                             (n_main + n_cmp, G * LANES, F32, None)])
    u = u.reshape(B, S, -1)
    gate3 = gate3.reshape(B, S, -1)
    xc = xc.reshape(2, G, B, S // A_CMP_STRIDE, A_CMP_STRIDE * dh)
    pe = jnp.stack([pe_k.reshape(1, -1), pe_v.reshape(1, -1)])
    kvc = compress(xc, pe, jnp.stack([w1_k, w1_v]).astype(BF16), jnp.stack([w2_k, w2_v]).astype(BF16))
    o_cmp, sel = cmp_select(u, kvc)
    mixed = slc_win(u, sel, o_cmp, gate3)
    mem_out = mem_attention(u, (qw + 4 * G * dh) // MEM_WIDTH, mem_kv)
    return mixed, mem_out


def stick_layer_mix(x, B, S, norm, w_in, mem_kv):
    (u,) = rms_proj(x, norm, w_in.astype(BF16), [(0, w_in.shape[1], BF16, None)])
    u = u.reshape(B, S, -1)
    mixed = stick_breaking(u)
    mem_out = mem_attention(u, 3 * MIX_WIDTH // MEM_WIDTH, mem_kv)
    return mixed, mem_out


def mlstm_layer_mix(x, B, S, norm, w_in, conv_w, conv_b, gate_b, head_norm, mem_kv):
    W = MIX_WIDTH
    nf = 4 * W + 2 * C_HEADS
    n_f = 4 * W + LANES
    w_all = jnp.concatenate([_pad_cols(w_in[:, :nf], n_f), w_in[:, nf:]], axis=1).astype(BF16)
    u, um = rms_proj(x, norm, w_all, [(0, n_f, F32, None), (n_f, MEM_WIDTH, BF16, None)], tm=256)
    u = u.reshape(B, S, -1)
    um = um.reshape(B, S, -1)
    col_scale = jnp.concatenate([jnp.ones((W,), F32), jnp.full((W,), C_HEAD_DIM ** -0.5, F32)])
    qk = conv_silu(u, conv_w, conv_b, col_scale)
    mixed = mlstm(qk, u, gate_b, head_norm)
    mem_out = mem_attention(um, 0, mem_kv)
    return mixed, mem_out


def swa_layer_mix(x, B, S, norm, w_in, sinks, mem_kv):
    W = MIX_WIDTH
    kw = D_KV * D_HEAD_DIM
    wk = w_in[:, W:W + kw].reshape(-1, D_KV, D_HEAD_DIM)
    wv = w_in[:, W + kw:W + 2 * kw].reshape(-1, D_KV, D_HEAD_DIM)
    wkv = jnp.concatenate([wk, wv], axis=2).reshape(-1, 2 * kw)
    w_main = jnp.concatenate([w_in[:, :W], w_in[:, W + 2 * kw:], wkv], axis=1)
    w_main = _pad_cols(w_main, W + MEM_WIDTH + 4 * LANES).astype(BF16)
    (u,) = rms_proj(x, norm, w_main, [(0, w_main.shape[1], BF16, None)])
    u = u.reshape(B, S, -1)
    mixed = sink_window(u, sinks)
    mem_out = mem_attention(u, W // MEM_WIDTH, mem_kv)
    return mixed, mem_out


def kernel(x, mem, mem_norm, mem_w_kv, l0_norm_mix, l0_w_in, l0_cmp_pe_k, l0_cmp_w1_k, l0_cmp_w2_k, l0_cmp_pe_v, l0_cmp_w1_v, l0_cmp_w2_v, l0_w_out, l0_norm_ffn, l0_w_gate, l0_w_up, l0_w_down, l1_norm_mix, l1_w_in, l1_w_out, l1_norm_ffn, l1_w_gate, l1_w_up, l1_w_down, l2_norm_mix, l2_w_in, l2_conv_w, l2_conv_b, l2_gate_b, l2_head_norm, l2_w_out, l2_norm_ffn, l2_w_gate, l2_w_up, l2_w_down, l3_norm_mix, l3_w_in, l3_sinks, l3_w_out, l3_norm_ffn, l3_w_gate, l3_w_up, l3_w_down, final_norm):
    B, S, D = x.shape
    M = mem.shape[1]
    (mem_kv,) = rms_proj(mem.reshape(B * M, D), mem_norm, mem_w_kv.astype(BF16),
                         [(0, mem_w_kv.shape[1], BF16, None)])
    mem_kv = mem_kv.reshape(B, M, -1)
    xs = x.reshape(B * S, D)

    def finish(xs, mixed, mem_out, w_out, norm_ffn, w_gate, w_up, w_down):
        w_out = w_out.astype(BF16)
        return out_proj_ffn(xs, mixed.reshape(B * S, -1), mem_out.reshape(B * S, -1),
                            w_out[:MIX_WIDTH], w_out[MIX_WIDTH:], norm_ffn,
                            w_gate.astype(BF16), w_up.astype(BF16), w_down.astype(BF16))

    mixed, mem_out = nsa_layer_mix(xs, B, S, l0_norm_mix, l0_w_in, l0_cmp_pe_k, l0_cmp_w1_k, l0_cmp_w2_k,
                                   l0_cmp_pe_v, l0_cmp_w1_v, l0_cmp_w2_v, mem_kv)
    xs = finish(xs, mixed, mem_out, l0_w_out, l0_norm_ffn, l0_w_gate, l0_w_up, l0_w_down)
    mixed, mem_out = stick_layer_mix(xs, B, S, l1_norm_mix, l1_w_in, mem_kv)
    xs = finish(xs, mixed, mem_out, l1_w_out, l1_norm_ffn, l1_w_gate, l1_w_up, l1_w_down)
    mixed, mem_out = mlstm_layer_mix(xs, B, S, l2_norm_mix, l2_w_in, l2_conv_w, l2_conv_b, l2_gate_b,
                                     l2_head_norm, mem_kv)
    xs = finish(xs, mixed, mem_out, l2_w_out, l2_norm_ffn, l2_w_gate, l2_w_up, l2_w_down)
    mixed, mem_out = swa_layer_mix(xs, B, S, l3_norm_mix, l3_w_in, l3_sinks, mem_kv)
    xs = finish(xs, mixed, mem_out, l3_w_out, l3_norm_ffn, l3_w_gate, l3_w_up, l3_w_down)
    return rms_only(xs, final_norm).reshape(B, S, D)
```

```python
import functools

import jax
import jax.numpy as jnp
from jax import lax
from jax.experimental import pallas as pl
from jax.experimental.pallas import tpu as pltpu

F32 = jnp.float32
BF16 = jnp.bfloat16

LANES = 128
VMEM_LIMIT = 56 * 1024 * 1024

D_MODEL = 2048
HEAD_DIM = 128
MEM_HEADS = 4
MEM_WIDTH = MEM_HEADS * HEAD_DIM
MIX_WIDTH = D_MODEL - MEM_WIDTH
EPS = 1e-6
NEG_INF = -1e30

A_HEADS = MIX_WIDTH // HEAD_DIM
A_KV = 2
A_HPG = A_HEADS // A_KV
A_CMP_LEN = 32
A_CMP_STRIDE = 16
A_SEL_LEN = 64
A_TOP_N = 16
A_WINDOW = 512
A_FORCE = 1e6
A_EXCLUDE = -1e9

B_HEADS = MIX_WIDTH // HEAD_DIM

C_HEADS = 4
C_HEAD_DIM = MIX_WIDTH // C_HEADS
C_CONV = 4
C_CHUNK = 256

D_HEAD_DIM = 64
D_HEADS = MIX_WIDTH // D_HEAD_DIM
D_KV = D_HEADS // 8
D_HPG = D_HEADS // D_KV
D_WINDOW = 128


def _params(*sem):
    return pltpu.CompilerParams(dimension_semantics=sem, vmem_limit_bytes=VMEM_LIMIT)


def _dot(a, b):
    return jnp.dot(a, b, preferred_element_type=F32)


def _dot_nt(a, b):
    return lax.dot_general(a, b, (((1,), (1,)), ((), ())), preferred_element_type=F32)


def _dot_tn(a, b):
    return lax.dot_general(a, b, (((0,), (0,)), ((), ())), preferred_element_type=F32)


def _split2(x):
    hi = x.astype(BF16)
    lo = (x - hi.astype(F32)).astype(BF16)
    return hi, lo


def _split3(x):
    hi = x.astype(BF16)
    r = x - hi.astype(F32)
    mid = r.astype(BF16)
    lo = (r - mid.astype(F32)).astype(BF16)
    return hi, mid, lo


PROJ_CHUNK = 1024


def _rms_proj_kernel(x_ref, g_ref, w_ref, *o_refs, outs):
    x = x_ref[...]
    r = lax.rsqrt(jnp.mean(x * x, axis=-1, keepdims=True) + EPS)
    h = (x * r * g_ref[...]).astype(BF16)
    for o_ref, (start, width, split) in zip(o_refs, outs):
        step = split if split else min(PROJ_CHUNK, width)
        for n, c0 in enumerate(range(0, width, step)):
            cw = min(step, width - c0)
            y = _dot(h, w_ref[:, start + c0:start + c0 + cw]).astype(o_ref.dtype)
            if split:
                o_ref[n] = y
            else:
                o_ref[:, c0:c0 + cw] = y


def rms_proj(x, g, w, outs, *, tm=512):
    T, D = x.shape
    N = w.shape[1]
    tm = min(tm, T)
    shapes, specs = [], []
    for start, width, dtype, split in outs:
        if split:
            shapes.append(jax.ShapeDtypeStruct((width // split, T, split), dtype))
            specs.append(pl.BlockSpec((width // split, tm, split), lambda i: (0, i, 0)))
        else:
            shapes.append(jax.ShapeDtypeStruct((T, width), dtype))
            specs.append(pl.BlockSpec((tm, width), lambda i: (i, 0)))
    return pl.pallas_call(
        functools.partial(_rms_proj_kernel, outs=tuple((s, wd, sp) for s, wd, _, sp in outs)),
        out_shape=tuple(shapes),
        grid=(T // tm,),
        in_specs=[pl.BlockSpec((tm, D), lambda i: (i, 0)),
                  pl.BlockSpec((1, D), lambda i: (0, 0)),
                  pl.BlockSpec((D, N), lambda i: (0, 0), pipeline_mode=pl.Buffered(1))],
        out_specs=tuple(specs),
        compiler_params=_params("parallel"),
        name="rms_proj",
    )(x, g.reshape(1, D), w)


def _ffn_kernel(x_ref, y1_ref, y2_ref, wo1_ref, wo2_ref, g_ref, wg_ref, wu_ref, wd_ref, o_ref, h_ref):
    @pl.when(pl.program_id(1) == 0)
    def _():
        x = x_ref[...] + _dot(y1_ref[...], wo1_ref[...]) + _dot(y2_ref[...], wo2_ref[...])
        r = lax.rsqrt(jnp.mean(x * x, axis=-1, keepdims=True) + EPS)
        h_ref[...] = (x * r * g_ref[...]).astype(BF16)
        o_ref[...] = x

    h = h_ref[...]
    a = _dot(h, wg_ref[...])
    u = _dot(h, wu_ref[...])
    act = (a * jax.nn.sigmoid(a) * u).astype(BF16)
    o_ref[...] += _dot(act, wd_ref[...])


def out_proj_ffn(x, y1, y2, wo1, wo2, g, wg, wu, wd, *, tm=512, tf=512):
    T, D = x.shape
    K1, K2 = y1.shape[1], y2.shape[1]
    FF = wg.shape[1]
    tm = min(tm, T)
    assert FF % tf == 0
    once = pl.Buffered(1)
    return pl.pallas_call(
        _ffn_kernel,
        out_shape=jax.ShapeDtypeStruct((T, D), F32),
        grid=(T // tm, FF // tf),
        in_specs=[pl.BlockSpec((tm, D), lambda i, f: (i, 0)),
                  pl.BlockSpec((tm, K1), lambda i, f: (i, 0)),
                  pl.BlockSpec((tm, K2), lambda i, f: (i, 0)),
                  pl.BlockSpec((K1, D), lambda i, f: (0, 0), pipeline_mode=once),
                  pl.BlockSpec((K2, D), lambda i, f: (0, 0), pipeline_mode=once),
                  pl.BlockSpec((1, D), lambda i, f: (0, 0)),
                  pl.BlockSpec((D, tf), lambda i, f: (0, f)),
                  pl.BlockSpec((D, tf), lambda i, f: (0, f)),
                  pl.BlockSpec((tf, D), lambda i, f: (f, 0))],
        out_specs=pl.BlockSpec((tm, D), lambda i, f: (i, 0)),
        scratch_shapes=[pltpu.VMEM((tm, D), BF16)],
        compiler_params=_params("parallel", "arbitrary"),
        name="out_proj_ffn",
    )(x, y1, y2, wo1, wo2, g.reshape(1, D), wg, wu, wd)


def _rms_kernel(x_ref, g_ref, o_ref):
    x = x_ref[...]
    r = lax.rsqrt(jnp.mean(x * x, axis=-1, keepdims=True) + EPS)
    o_ref[...] = x * r * g_ref[...]


def rms_only(x, g, *, tm=512):
    T, D = x.shape
    tm = min(tm, T)
    return pl.pallas_call(
        _rms_kernel,
        out_shape=jax.ShapeDtypeStruct((T, D), F32),
        grid=(T // tm,),
        in_specs=[pl.BlockSpec((tm, D), lambda i: (i, 0)),
                  pl.BlockSpec((1, D), lambda i: (0, 0))],
        out_specs=pl.BlockSpec((tm, D), lambda i: (i, 0)),
        compiler_params=_params("parallel"),
        name="final_norm",
    )(x, g.reshape(1, D))


def _mem_attn_kernel(q_ref, k_ref, v_ref, o_ref):
    scale = HEAD_DIM ** -0.5
    for h in range(MEM_HEADS):
        sl = slice(h * HEAD_DIM, (h + 1) * HEAD_DIM)
        s = _dot_nt(q_ref[:, sl], k_ref[:, sl]) * scale
        m = jnp.max(s, axis=-1, keepdims=True)
        e = jnp.exp(s - m)
        p = e / jnp.sum(e, axis=-1, keepdims=True)
        o_ref[:, sl] = _dot(p.astype(BF16), v_ref[:, sl]).astype(o_ref.dtype)


def mem_attention(u3, qblk, mem_kv, *, tq=512):
    B, S, _ = u3.shape
    M = mem_kv.shape[1]
    tq = min(tq, S)
    return pl.pallas_call(
        _mem_attn_kernel,
        out_shape=jax.ShapeDtypeStruct((B, S, MEM_WIDTH), BF16),
        grid=(B, S // tq),
        in_specs=[pl.BlockSpec((None, tq, MEM_WIDTH), lambda b, i: (b, i, qblk)),
                  pl.BlockSpec((None, M, MEM_WIDTH), lambda b, i: (b, 0, 0)),
                  pl.BlockSpec((None, M, MEM_WIDTH), lambda b, i: (b, 0, 1))],
        out_specs=pl.BlockSpec((None, tq, MEM_WIDTH), lambda b, i: (b, i, 0)),
        compiler_params=_params("parallel", "parallel"),
        name="mem_attention",
    )(u3, mem_kv, mem_kv)


LOG2E = 1.4426950408889634


def _stick_kernel(q_ref, k_ref, v_ref, o_ref, acc_ref, tail_ref, z0_ref, z1_ref, *, tq, tk):
    i = pl.program_id(2)
    z_refs = (z0_ref, z1_ref)
    c2 = HEAD_DIM ** -0.5 * LOG2E
    per_q = tq // tk
    nh = q_ref.shape[1] // HEAD_DIM
    lrow = lax.broadcasted_iota(jnp.int32, (2 * tk, tk), 0)
    lcol = lax.broadcasted_iota(jnp.int32, (2 * tk, tk), 1)
    later = ((lrow > lcol) & ((lrow < tk) | (lrow > lcol + tk))).astype(BF16)
    rows = lax.broadcasted_iota(jnp.int32, (tq, tk), 0)
    cols = lax.broadcasted_iota(jnp.int32, (tq, tk), 1)
    acc_ref[...] = jnp.zeros_like(acc_ref)
    tail_ref[...] = jnp.zeros_like(tail_ref)

    def scores(j, slot):
        off = pl.multiple_of(j * tk, tk)
        for hh in range(nh):
            sl = slice(hh * HEAD_DIM, (hh + 1) * HEAD_DIM)
            z_refs[slot][hh] = _dot_nt(q_ref[:, sl], k_ref[pl.ds(off, tk), sl])

    def step(j, slot, diag):
        off = pl.multiple_of(j * tk, tk)
        if diag is not None:
            causal = cols + diag * tk < rows
        scores(jnp.maximum(j - 1, 0), 1 - slot)
        part = []
        for hh in range(nh):
            ns = z_refs[slot][hh] * (-c2)
            log_keep = jnp.minimum(ns, 0.0) - jnp.log2(1.0 + jnp.exp2(-jnp.abs(ns)))
            if diag is not None:
                log_keep = jnp.where(causal, log_keep, 0.0)
            hi, lo = _split2(log_keep)
            after = _dot(jnp.concatenate([hi, lo], axis=1), later) + tail_ref[hh]
            part.append((ns, log_keep, after))
        for hh in range(nh):
            sl = slice(hh * HEAD_DIM, (hh + 1) * HEAD_DIM)
            ns, log_keep, after = part[hh]
            a = jnp.exp2(log_keep + after - ns)
            if diag is not None:
                a = jnp.where(causal, a, 0.0)
            acc_ref[hh] += _dot(a.astype(BF16), v_ref[pl.ds(off, tk), sl])
            tail_ref[hh] = after[:, 0:1] + log_keep[:, 0:1]

    assert per_q % 2 == 0
    scores((i + 1) * per_q - 1, 1)
    for d in reversed(range(per_q)):
        step(i * per_q + d, d % 2, d)

    @pl.loop(0, i * (per_q // 2))
    def _(jj):
        step(i * per_q - 1 - 2 * jj, 1, None)
        step(i * per_q - 2 - 2 * jj, 0, None)

    for hh in range(nh):
        o_ref[:, hh * HEAD_DIM:(hh + 1) * HEAD_DIM] = acc_ref[hh].astype(o_ref.dtype)


def stick_breaking(u3, *, tq=512, tk=256, nh=2):
    B, S, _ = u3.shape
    H = B_HEADS // nh
    tq = min(tq, S)
    tk = min(tk, tq)
    hw = nh * HEAD_DIM
    return pl.pallas_call(
        functools.partial(_stick_kernel, tq=tq, tk=tk),
        out_shape=jax.ShapeDtypeStruct((B, S, MIX_WIDTH), BF16),
        grid=(B, H, S // tq),
        in_specs=[pl.BlockSpec((None, tq, hw), lambda b, h, i: (b, i, h)),
                  pl.BlockSpec((None, S, hw), lambda b, h, i: (b, 0, H + h)),
                  pl.BlockSpec((None, S, hw), lambda b, h, i: (b, 0, 2 * H + h))],
        out_specs=pl.BlockSpec((None, tq, hw), lambda b, h, i: (b, i, h)),
        scratch_shapes=[pltpu.VMEM((nh, tq, HEAD_DIM), F32), pltpu.VMEM((nh, tq, 1), F32),
                        pltpu.VMEM((nh, tq, tk), F32), pltpu.VMEM((nh, tq, tk), F32)],
        compiler_params=_params("parallel", "parallel", "arbitrary"),
        name="stick_breaking",
    )(u3, u3, u3)


def _swa_kernel(sink_ref, q_ref, kvp_ref, kvc_ref, o_ref, *, tq):
    g = pl.program_id(1)
    i = pl.program_id(2)
    dh = D_HEAD_DIM
    scale = dh ** -0.5
    kv = jnp.concatenate([kvp_ref[...], kvc_ref[...]], axis=0)
    k = kv[:, :dh]
    v = kv[:, dh:]
    r = lax.broadcasted_iota(jnp.int32, (tq, 2 * tq), 0)
    c = lax.broadcasted_iota(jnp.int32, (tq, 2 * tq), 1)
    rel = r - c + tq
    mask = (rel >= 0) & (rel < D_WINDOW) & ((c >= tq) | (i > 0))
    for hh in range(D_HPG):
        sk = sink_ref[g * D_HPG + hh]
        s = _dot_nt(q_ref[:, hh * dh:(hh + 1) * dh], k) * scale
        s = jnp.where(mask, s, NEG_INF)
        m = jnp.maximum(jnp.max(s, axis=-1, keepdims=True), sk)
        e = jnp.exp(s - m)
        p = e / (jnp.sum(e, axis=-1, keepdims=True) + jnp.exp(sk - m))
        o_ref[:, hh * dh:(hh + 1) * dh] = _dot(p.astype(BF16), v).astype(o_ref.dtype)


def sink_window(u3, sinks):
    B, S, _ = u3.shape
    tq = D_WINDOW
    gw = D_HPG * D_HEAD_DIM
    kv0 = (MIX_WIDTH + MEM_WIDTH) // (2 * D_HEAD_DIM)
    return pl.pallas_call(
        functools.partial(_swa_kernel, tq=tq),
        out_shape=jax.ShapeDtypeStruct((B, S, MIX_WIDTH), BF16),
        grid=(B, D_KV, S // tq),
        in_specs=[pl.BlockSpec(memory_space=pltpu.SMEM),
                  pl.BlockSpec((None, tq, gw), lambda b, g, i: (b, i, g)),
                  pl.BlockSpec((None, tq, 2 * D_HEAD_DIM),
                               lambda b, g, i: (b, jnp.maximum(i - 1, 0), kv0 + g)),
                  pl.BlockSpec((None, tq, 2 * D_HEAD_DIM), lambda b, g, i: (b, i, kv0 + g))],
        out_specs=pl.BlockSpec((None, tq, gw), lambda b, g, i: (b, i, g)),
        compiler_params=_params("parallel", "parallel", "arbitrary"),
        name="sink_window",
    )(sinks, u3, u3, u3)


def _conv_kernel(xp_ref, xc_ref, w_ref, b_ref, s_ref, o_ref, buf_ref, *, tc, halo):
    i = pl.program_id(1)
    prev = xp_ref[...]
    buf_ref[0:halo, :] = jnp.where(i > 0, prev, jnp.zeros_like(prev))
    buf_ref[halo:halo + tc, :] = xc_ref[...]
    acc = b_ref[...] + w_ref[C_CONV - 1:C_CONV, :] * xc_ref[...]
    for j in range(C_CONV - 1):
        sh = C_CONV - 1 - j
        acc = acc + w_ref[j:j + 1, :] * buf_ref[halo - sh:halo - sh + tc, :]
    y = acc * jax.nn.sigmoid(acc)
    o_ref[...] = (y * s_ref[...]).astype(o_ref.dtype)


def conv_silu(u3, conv_w, conv_b, col_scale, *, tc=256, tn=512):
    B, S, _ = u3.shape
    C = conv_w.shape[1]
    tc = min(tc, S)
    halo = 8
    return pl.pallas_call(
        functools.partial(_conv_kernel, tc=tc, halo=halo),
        out_shape=jax.ShapeDtypeStruct((B, S, C), BF16),
        grid=(B, S // tc, C // tn),
        in_specs=[pl.BlockSpec((None, halo, tn),
                               lambda b, i, j: (b, jnp.maximum(i * (tc // halo) - 1, 0), j)),
                  pl.BlockSpec((None, tc, tn), lambda b, i, j: (b, i, j)),
                  pl.BlockSpec((C_CONV, tn), lambda b, i, j: (0, j)),
                  pl.BlockSpec((1, tn), lambda b, i, j: (0, j)),
                  pl.BlockSpec((1, tn), lambda b, i, j: (0, j))],
        out_specs=pl.BlockSpec((None, tc, tn), lambda b, i, j: (b, i, j)),
        scratch_shapes=[pltpu.VMEM((tc + halo, tn), F32)],
        compiler_params=_params("parallel", "parallel", "parallel"),
        name="conv_silu",
    )(u3, u3, conv_w, conv_b.reshape(1, C), col_scale.reshape(1, C))


def _mlstm_kernel(q_ref, k_ref, v_ref, o_ref, gt_ref, gb_ref, hn_ref, out_ref,
                  state_ref, m_ref, *, L):
    H, dh = C_HEADS, C_HEAD_DIM
    ext = dh + LANES

    @pl.when(pl.program_id(1) == 0)
    def _():
        state_ref[...] = jnp.zeros_like(state_ref)
        m_ref[...] = jnp.zeros_like(m_ref)

    rows = lax.broadcasted_iota(jnp.int32, (L, L), 0)
    cols = lax.broadcasted_iota(jnp.int32, (L, L), 1)
    tril = rows >= cols
    eye = rows == cols
    gates = gt_ref[...] + gb_ref[...]
    log_f = jnp.minimum(gates, 0.0) - jnp.log1p(jnp.exp(-jnp.abs(gates)))
    f_hi, f_mid, f_lo = _split3(log_f)
    trilb = tril.astype(BF16)
    bsum = _dot(trilb, f_hi) + _dot(trilb, f_mid) + _dot(trilb, f_lo)
    ones_col = (lax.broadcasted_iota(jnp.int32, (L, LANES), 1) == 0).astype(F32)

    for h in range(H):
        sl = slice(h * dh, (h + 1) * dh)
        q = q_ref[:, sl]
        k = k_ref[:, sl]
        v_ext = jnp.concatenate([v_ref[:, sl], ones_col], axis=1)
        ic = gates[:, h:h + 1]
        bc = bsum[:, H + h:H + h + 1]
        m_prev = m_ref[h][0:1, 0:1]
        rc = ic - bc
        rrow = jnp.sum(jnp.where(eye, rc, 0.0), axis=0, keepdims=True)
        dm = jnp.where(tril, bc + rrow, NEG_INF)
        m_inter = bc + m_prev
        m_t = jnp.maximum(m_inter, jnp.max(dm, axis=-1, keepdims=True))
        w = jnp.exp(dm - m_t)
        a_inter = jnp.exp(m_inter - m_t)
        sqk = _dot_nt(q, k) * w
        state = state_ref[h]
        num = a_inter * _dot(q, state.astype(BF16)) + _dot(sqk.astype(BF16), v_ext.astype(BF16))
        den = jnp.maximum(jnp.abs(num[:, dh:dh + 1]), jnp.exp(-m_t))
        hid = num[:, :dh] / den
        hid = jax.nn.sigmoid(o_ref[:, sl]) * hid
        hid = hid * lax.rsqrt(jnp.mean(hid * hid, axis=-1, keepdims=True) + EPS)
        out_ref[:, sl] = (hid * hn_ref[:, sl]).astype(out_ref.dtype)

        b_last = bc[L - 1:L, :]
        gk = b_last - bc + ic
        m_new = jnp.maximum(b_last + m_prev, jnp.max(gk, axis=0, keepdims=True))
        wk = jnp.exp(gk - m_new)
        decay = jnp.exp(b_last + m_prev - m_new)
        state_ref[h] = decay * state + _dot_tn(k, (wk * v_ext).astype(BF16))
        m_ref[h] = jnp.broadcast_to(m_new, m_ref.shape[1:])


def mlstm(qk3, u3, gate_b, head_norm, *, L=C_CHUNK):
    B, S, _ = u3.shape
    W = MIX_WIDTH
    H, dh = C_HEADS, C_HEAD_DIM
    L = min(L, S)
    gblk = 4 * W // LANES
    gb = jnp.zeros((1, LANES), F32).at[0, :2 * H].set(gate_b)
    return pl.pallas_call(
        functools.partial(_mlstm_kernel, L=L),
        out_shape=jax.ShapeDtypeStruct((B, S, W), BF16),
        grid=(B, S // L),
        in_specs=[pl.BlockSpec((None, L, W), lambda b, c: (b, c, 0)),
                  pl.BlockSpec((None, L, W), lambda b, c: (b, c, 1)),
                  pl.BlockSpec((None, L, W), lambda b, c: (b, c, 2)),
                  pl.BlockSpec((None, L, W), lambda b, c: (b, c, 3)),
                  pl.BlockSpec((None, L, LANES), lambda b, c: (b, c, gblk)),
                  pl.BlockSpec((1, LANES), lambda b, c: (0, 0)),
                  pl.BlockSpec((1, W), lambda b, c: (0, 0))],
        out_specs=pl.BlockSpec((None, L, W), lambda b, c: (b, c, 0)),
        scratch_shapes=[pltpu.VMEM((H, dh, dh + LANES), F32),
                        pltpu.VMEM((H, 8, LANES), F32)],
        compiler_params=_params("parallel", "arbitrary"),
        name="mlstm",
    )(qk3, qk3, u3, u3, u3, gb, head_norm.reshape(1, W))


def _compress_kernel(x_ref, pe_ref, w1_ref, w2_ref, o_ref):
    half = w1_ref.shape[0] // 2
    x = x_ref[...]
    n = x.shape[0]
    first = _dot(x, w1_ref[0:half, :])
    second = _dot(x, w1_ref[half:, :])
    pe = jnp.broadcast_to(pe_ref[...], (8, pe_ref.shape[1])).astype(BF16)
    bias = _dot(pe, w1_ref[...])[0:1, :]
    hid = first + pltpu.roll(second, n - 1, 0) + bias
    o_ref[...] = _dot(jax.nn.gelu(hid).astype(BF16), w2_ref[...]).astype(o_ref.dtype)


def compress(xc, pe, w1, w2):
    _, G, B, n, width = xc.shape
    hid = w1.shape[2]
    return pl.pallas_call(
        _compress_kernel,
        out_shape=jax.ShapeDtypeStruct((2, G, B, n, HEAD_DIM), BF16),
        grid=(2, G, B),
        in_specs=[pl.BlockSpec((None, None, None, n, width), lambda j, g, b: (j, g, b, 0, 0)),
                  pl.BlockSpec((None, 1, 2 * width), lambda j, g, b: (j, 0, 0)),
                  pl.BlockSpec((None, 2 * width, hid), lambda j, g, b: (j, 0, 0)),
                  pl.BlockSpec((None, hid, HEAD_DIM), lambda j, g, b: (j, 0, 0))],
        out_specs=pl.BlockSpec((None, None, None, n, HEAD_DIM), lambda j, g, b: (j, g, b, 0, 0)),
        compiler_params=_params("parallel", "parallel", "parallel"),
        name="nsa_compress",
    )(xc, pe, w1, w2)


def _cmp_select_kernel(q_ref, kc_ref, vc_ref, o_ref, sel_ref, *, tq, n_cmp, n_slc):
    i = pl.program_id(2)
    scale = HEAD_DIM ** -0.5
    npad = kc_ref.shape[0]
    kc = kc_ref[...]
    vc = vc_ref[...]
    pos = i * tq + lax.broadcasted_iota(jnp.int32, (tq, npad), 0)
    nidx = lax.broadcasted_iota(jnp.int32, (tq, npad), 1)
    cmask = (nidx * A_CMP_STRIDE + (A_CMP_LEN - 1) <= pos) & (nidx < n_cmp)
    psum = jnp.zeros((tq, npad), F32)
    for h in range(A_HPG):
        sl = slice(h * HEAD_DIM, (h + 1) * HEAD_DIM)
        s = jnp.where(cmask, _dot_nt(q_ref[:, sl], kc) * scale, NEG_INF)
        m = jnp.max(s, axis=-1, keepdims=True)
        e = jnp.where(cmask, jnp.exp(s - m), 0.0)
        p = e / jnp.maximum(jnp.sum(e, axis=-1, keepdims=True), 1e-30)
        o_ref[:, sl] = _dot(p.astype(BF16), vc)
        psum = psum + p

    on = lax.broadcasted_iota(jnp.int32, (npad, LANES), 0) * A_CMP_STRIDE
    oj = lax.broadcasted_iota(jnp.int32, (npad, LANES), 1)
    overlap = ((on < (oj + 1) * A_SEL_LEN) & (on + A_CMP_LEN > oj * A_SEL_LEN)
               & (on < n_cmp * A_CMP_STRIDE) & (oj < n_slc)).astype(BF16)
    p_hi, p_lo = _split2(psum)
    imp = _dot(p_hi, overlap) + _dot(p_lo, overlap)

    sub = 8
    blk = lax.broadcasted_iota(jnp.int32, (LANES, tq), 0)
    cur = (i * tq + lax.broadcasted_iota(jnp.int32, (LANES, tq), 1)) // A_SEL_LEN
    valid = blk <= cur
    forced = (blk == 0) | (blk == cur) | (blk == cur - 1)
    score = jnp.where(valid, jnp.where(forced, A_FORCE, imp.T), A_EXCLUDE)
    row8 = lax.broadcasted_iota(jnp.int32, (sub, tq), 0)
    n_top = min(A_TOP_N, n_slc)
    chosen = []
    for v in range(0, n_slc, sub):
        mine = score[v:v + sub, :]
        rank = jnp.zeros((sub, tq), jnp.int32)
        for j in range(n_slc):
            other = score[j:j + 1, :]
            if v > j:
                ahead = other >= mine
            elif v + sub - 1 <= j:
                ahead = other > mine
            else:
                ahead = (other > mine) | ((other == mine) & (row8 + v > j))
            rank = rank + ahead.astype(jnp.int32)
        chosen.append(((rank < n_top) & valid[v:v + sub, :]).astype(F32))
    if n_slc < LANES:
        chosen.append(jnp.zeros((LANES - n_slc, tq), F32))
    sel_ref[...] = jnp.concatenate(chosen, axis=0).T.astype(sel_ref.dtype)


def cmp_select(u3, kvc, *, tq=128):
    B, S, _ = u3.shape
    G = A_KV
    npad = kvc.shape[3]
    n_cmp = (S - A_CMP_LEN) // A_CMP_STRIDE + 1
    n_slc = S // A_SEL_LEN
    tq = min(tq, S)
    gw = A_HPG * HEAD_DIM
    return pl.pallas_call(
        functools.partial(_cmp_select_kernel, tq=tq, n_cmp=n_cmp, n_slc=n_slc),
        out_shape=(jax.ShapeDtypeStruct((B, S, MIX_WIDTH), F32),
                   jax.ShapeDtypeStruct((B, G, S, LANES), BF16)),
        grid=(B, G, S // tq),
        in_specs=[pl.BlockSpec((None, tq, gw), lambda b, g, i: (b, i, g)),
                  pl.BlockSpec((None, None, None, npad, HEAD_DIM), lambda b, g, i: (0, g, b, 0, 0)),
                  pl.BlockSpec((None, None, None, npad, HEAD_DIM), lambda b, g, i: (1, g, b, 0, 0))],
        out_specs=(pl.BlockSpec((None, tq, gw), lambda b, g, i: (b, i, g)),
                   pl.BlockSpec((None, None, tq, LANES), lambda b, g, i: (b, g, i, 0))),
        compiler_params=_params("parallel", "parallel", "parallel"),
        name="nsa_cmp_select",
    )(u3, kvc, kvc)


def _masked_scores(qs, k, mask, scale):
    hp = qs.shape[0] // mask.shape[0]
    s = _dot_nt(qs, k) * scale
    return jnp.where(mask[None], s.reshape((hp,) + mask.shape), NEG_INF).reshape(s.shape)


def _slc_win_kernel(q_ref, ks_ref, vs_ref, kw_ref, vw_ref, sel_ref, oc_ref, gt_ref, o_ref, *, tq, tk, wk):
    i = pl.program_id(2)
    S = ks_ref.shape[0]
    scale = HEAD_DIM ** -0.5 * LOG2E
    hp = A_HPG
    qs = jnp.concatenate([q_ref[:, h * HEAD_DIM:(h + 1) * HEAD_DIM] for h in range(hp)], axis=0)
    sel = sel_ref[...]
    qpos = i * tq + lax.broadcasted_iota(jnp.int32, (tq, tk), 0)
    c = lax.broadcasted_iota(jnp.int32, (tq, tk), 1)
    eb = lax.broadcasted_iota(jnp.int32, (LANES, tk), 0)
    ec = lax.broadcasted_iota(jnp.int32, (LANES, tk), 1) // A_SEL_LEN

    def slc_body(j, carry):
        m, l, acc = carry
        off = pl.multiple_of(j * tk, tk)
        expand = (eb == j * (tk // A_SEL_LEN) + ec).astype(BF16)
        mask = (_dot(sel, expand) > 0.5) & (off + c <= qpos)
        s = _masked_scores(qs, ks_ref[pl.ds(off, tk), :], mask, scale)
        m_new = jnp.maximum(m, jnp.max(s, axis=-1, keepdims=True))
        p = jnp.exp2(s - m_new)
        alpha = jnp.exp2(m - m_new)
        l = alpha * l + jnp.sum(p, axis=-1, keepdims=True)
        acc = alpha * acc + _dot(p.astype(BF16), vs_ref[pl.ds(off, tk), :])
        return m_new, l, acc

    init = (jnp.full((hp * tq, 1), NEG_INF, F32), jnp.zeros((hp * tq, 1), F32),
            jnp.zeros((hp * tq, HEAD_DIM), F32))
    _, l_s, acc_s = lax.fori_loop(0, (i * tq + tq + tk - 1) // tk, slc_body, init)
    o_slc = acc_s / l_s

    start = pl.multiple_of(jnp.clip(i * tq + tq - wk, 0, S - wk), tq)
    wq = i * tq + lax.broadcasted_iota(jnp.int32, (tq, wk), 0)
    rel = wq - (start + lax.broadcasted_iota(jnp.int32, (tq, wk), 1))
    s = _masked_scores(qs, kw_ref[pl.ds(start, wk), :], (rel >= 0) & (rel < A_WINDOW), scale)
    p = jnp.exp2(s - jnp.max(s, axis=-1, keepdims=True))
    o_win = _dot(p.astype(BF16), vw_ref[pl.ds(start, wk), :]) / jnp.sum(p, axis=-1, keepdims=True)

    gate = jax.nn.sigmoid(gt_ref[...])
    for h in range(hp):
        sl = slice(h * HEAD_DIM, (h + 1) * HEAD_DIM)
        rs = slice(h * tq, (h + 1) * tq)
        o = (gate[:, h:h + 1] * oc_ref[:, sl]
             + gate[:, hp + h:hp + h + 1] * o_slc[rs]
             + gate[:, 2 * hp + h:2 * hp + h + 1] * o_win[rs])
        o_ref[:, sl] = o.astype(o_ref.dtype)


def slc_win(u3, sel, o_cmp, gate3, *, tq=128, tk=512):
    B, S, _ = u3.shape
    G = A_KV
    tq = min(tq, S)
    tk = min(tk, S)
    wk = min(A_WINDOW + tq, S)
    assert S % tk == 0 and tk % tq == 0 and S // A_SEL_LEN <= LANES
    gw = A_HPG * HEAD_DIM
    kv0 = A_HEADS

    def kv_spec(which):
        return pl.BlockSpec((None, S, HEAD_DIM), lambda b, g, i: (b, 0, kv0 + which * G + g))

    return pl.pallas_call(
        functools.partial(_slc_win_kernel, tq=tq, tk=tk, wk=wk),
        out_shape=jax.ShapeDtypeStruct((B, S, MIX_WIDTH), BF16),
        grid=(B, G, S // tq),
        in_specs=[pl.BlockSpec((None, tq, gw), lambda b, g, i: (b, i, g)),
                  kv_spec(0), kv_spec(1), kv_spec(2), kv_spec(3),
                  pl.BlockSpec((None, None, tq, LANES), lambda b, g, i: (b, g, i, 0)),
                  pl.BlockSpec((None, tq, gw), lambda b, g, i: (b, i, g)),
                  pl.BlockSpec((None, tq, LANES), lambda b, g, i: (b, i, g))],
        out_specs=pl.BlockSpec((None, tq, gw), lambda b, g, i: (b, i, g)),
        compiler_params=_params("parallel", "parallel", "arbitrary"),
        name="nsa_slc_win",
    )(u3, u3, u3, u3, u3, sel, o_cmp, gate3)


def _pad_cols(w, n):
    return jnp.pad(w, ((0, 0), (0, n - w.shape[1])))


def nsa_layer_mix(x, B, S, norm, w_in, pe_k, w1_k, w2_k, pe_v, w1_v, w2_v, mem_kv):
    H, G, dh = A_HEADS, A_KV, HEAD_DIM
    qw = H * dh
    kvw = 6 * G * dh
    gw = 3 * H
    wg = w_in[:, qw + kvw:qw + kvw + gw].reshape(-1, G, A_HPG, 3).transpose(0, 1, 3, 2)
    wg = jnp.pad(wg.reshape(-1, G, 3 * A_HPG), ((0, 0), (0, 0), (0, LANES - 3 * A_HPG)))
    n_main = qw + 4 * G * dh + MEM_WIDTH
    n_cmp = 2 * G * dh
    w_all = jnp.concatenate([w_in[:, :qw], w_in[:, qw + n_cmp:qw + kvw], w_in[:, qw + kvw + gw:],
                             w_in[:, qw:qw + n_cmp], wg.reshape(-1, G * LANES)], axis=1).astype(BF16)

    u, xc, gate3 = rms_proj(x, norm, w_all, [(0, n_main, BF16, None), (n_main, n_cmp, BF16, dh),
                                             (n_main + n_cmp, G * LANES, F32, None)])
    u = u.reshape(B, S, -1)
    gate3 = gate3.reshape(B, S, -1)
    xc = xc.reshape(2, G, B, S // A_CMP_STRIDE, A_CMP_STRIDE * dh)
    pe = jnp.stack([pe_k.reshape(1, -1), pe_v.reshape(1, -1)])
    kvc = compress(xc, pe, jnp.stack([w1_k, w1_v]).astype(BF16), jnp.stack([w2_k, w2_v]).astype(BF16))
    o_cmp, sel = cmp_select(u, kvc)
    mixed = slc_win(u, sel, o_cmp, gate3)
    mem_out = mem_attention(u, (qw + 4 * G * dh) // MEM_WIDTH, mem_kv)
    return mixed, mem_out


def stick_layer_mix(x, B, S, norm, w_in, mem_kv):
    (u,) = rms_proj(x, norm, w_in.astype(BF16), [(0, w_in.shape[1], BF16, None)])
    u = u.reshape(B, S, -1)
    mixed = stick_breaking(u)
    mem_out = mem_attention(u, 3 * MIX_WIDTH // MEM_WIDTH, mem_kv)
    return mixed, mem_out


def mlstm_layer_mix(x, B, S, norm, w_in, conv_w, conv_b, gate_b, head_norm, mem_kv):
    W = MIX_WIDTH
    nf = 4 * W + 2 * C_HEADS
    n_f = 4 * W + LANES
    w_all = jnp.concatenate([_pad_cols(w_in[:, :nf], n_f), w_in[:, nf:]], axis=1).astype(BF16)
    u, um = rms_proj(x, norm, w_all, [(0, n_f, F32, None), (n_f, MEM_WIDTH, BF16, None)], tm=256)
    u = u.reshape(B, S, -1)
    um = um.reshape(B, S, -1)
    col_scale = jnp.concatenate([jnp.ones((W,), F32), jnp.full((W,), C_HEAD_DIM ** -0.5, F32)])
    qk = conv_silu(u, conv_w, conv_b, col_scale)
    mixed = mlstm(qk, u, gate_b, head_norm)
    mem_out = mem_attention(um, 0, mem_kv)
    return mixed, mem_out


def swa_layer_mix(x, B, S, norm, w_in, sinks, mem_kv):
    W = MIX_WIDTH
    kw = D_KV * D_HEAD_DIM
    wk = w_in[:, W:W + kw].reshape(-1, D_KV, D_HEAD_DIM)
    wv = w_in[:, W + kw:W + 2 * kw].reshape(-1, D_KV, D_HEAD_DIM)
    wkv = jnp.concatenate([wk, wv], axis=2).reshape(-1, 2 * kw)
    w_main = jnp.concatenate([w_in[:, :W], w_in[:, W + 2 * kw:], wkv], axis=1)
    w_main = _pad_cols(w_main, W + MEM_WIDTH + 4 * LANES).astype(BF16)
    (u,) = rms_proj(x, norm, w_main, [(0, w_main.shape[1], BF16, None)])
    u = u.reshape(B, S, -1)
    mixed = sink_window(u, sinks)
    mem_out = mem_attention(u, W // MEM_WIDTH, mem_kv)
    return mixed, mem_out


def kernel(x, mem, mem_norm, mem_w_kv, l0_norm_mix, l0_w_in, l0_cmp_pe_k, l0_cmp_w1_k, l0_cmp_w2_k, l0_cmp_pe_v, l0_cmp_w1_v, l0_cmp_w2_v, l0_w_out, l0_norm_ffn, l0_w_gate, l0_w_up, l0_w_down, l1_norm_mix, l1_w_in, l1_w_out, l1_norm_ffn, l1_w_gate, l1_w_up, l1_w_down, l2_norm_mix, l2_w_in, l2_conv_w, l2_conv_b, l2_gate_b, l2_head_norm, l2_w_out, l2_norm_ffn, l2_w_gate, l2_w_up, l2_w_down, l3_norm_mix, l3_w_in, l3_sinks, l3_w_out, l3_norm_ffn, l3_w_gate, l3_w_up, l3_w_down, final_norm):
    B, S, D = x.shape
    M = mem.shape[1]
    (mem_kv,) = rms_proj(mem.reshape(B * M, D), mem_norm, mem_w_kv.astype(BF16),
                         [(0, mem_w_kv.shape[1], BF16, None)])
    mem_kv = mem_kv.reshape(B, M, -1)
    xs = x.reshape(B * S, D)

    def finish(xs, mixed, mem_out, w_out, norm_ffn, w_gate, w_up, w_down):
        w_out = w_out.astype(BF16)
        return out_proj_ffn(xs, mixed.reshape(B * S, -1), mem_out.reshape(B * S, -1),
                            w_out[:MIX_WIDTH], w_out[MIX_WIDTH:], norm_ffn,
                            w_gate.astype(BF16), w_up.astype(BF16), w_down.astype(BF16))

    mixed, mem_out = nsa_layer_mix(xs, B, S, l0_norm_mix, l0_w_in, l0_cmp_pe_k, l0_cmp_w1_k, l0_cmp_w2_k,
                                   l0_cmp_pe_v, l0_cmp_w1_v, l0_cmp_w2_v, mem_kv)
    xs = finish(xs, mixed, mem_out, l0_w_out, l0_norm_ffn, l0_w_gate, l0_w_up, l0_w_down)
    mixed, mem_out = stick_layer_mix(xs, B, S, l1_norm_mix, l1_w_in, mem_kv)
    xs = finish(xs, mixed, mem_out, l1_w_out, l1_norm_ffn, l1_w_gate, l1_w_up, l1_w_down)
    mixed, mem_out = mlstm_layer_mix(xs, B, S, l2_norm_mix, l2_w_in, l2_conv_w, l2_conv_b, l2_gate_b,
                                     l2_head_norm, mem_kv)
    xs = finish(xs, mixed, mem_out, l2_w_out, l2_norm_ffn, l2_w_gate, l2_w_up, l2_w_down)
    mixed, mem_out = swa_layer_mix(xs, B, S, l3_norm_mix, l3_w_in, l3_sinks, mem_kv)
    xs = finish(xs, mixed, mem_out, l3_w_out, l3_norm_ffn, l3_w_gate, l3_w_up, l3_w_down)
    return rms_only(xs, final_norm).reshape(B, S, D)
```

```python
import functools

import jax
import jax.numpy as jnp
from jax import lax
from jax.experimental import pallas as pl
from jax.experimental.pallas import tpu as pltpu

F32 = jnp.float32
BF16 = jnp.bfloat16

LANES = 128
VMEM_LIMIT = 56 * 1024 * 1024

D_MODEL = 2048
HEAD_DIM = 128
MEM_HEADS = 4
MEM_WIDTH = MEM_HEADS * HEAD_DIM
MIX_WIDTH = D_MODEL - MEM_WIDTH
EPS = 1e-6
NEG_INF = -1e30

A_HEADS = MIX_WIDTH // HEAD_DIM
A_KV = 2
A_HPG = A_HEADS // A_KV
A_CMP_LEN = 32
A_CMP_STRIDE = 16
A_SEL_LEN = 64
A_TOP_N = 16
A_WINDOW = 512
A_FORCE = 1e6
A_EXCLUDE = -1e9

B_HEADS = MIX_WIDTH // HEAD_DIM

C_HEADS = 4
C_HEAD_DIM = MIX_WIDTH // C_HEADS
C_CONV = 4
C_CHUNK = 256

D_HEAD_DIM = 64
D_HEADS = MIX_WIDTH // D_HEAD_DIM
D_KV = D_HEADS // 8
D_HPG = D_HEADS // D_KV
D_WINDOW = 128


def _params(*sem):
    return pltpu.CompilerParams(dimension_semantics=sem, vmem_limit_bytes=VMEM_LIMIT)


def _dot(a, b):
    return jnp.dot(a, b, preferred_element_type=F32)


def _dot_nt(a, b):
    return lax.dot_general(a, b, (((1,), (1,)), ((), ())), preferred_element_type=F32)


def _dot_tn(a, b):
    return lax.dot_general(a, b, (((0,), (0,)), ((), ())), preferred_element_type=F32)


def _split2(x):
    hi = x.astype(BF16)
    lo = (x - hi.astype(F32)).astype(BF16)
    return hi, lo


def _split3(x):
    hi = x.astype(BF16)
    r = x - hi.astype(F32)
    mid = r.astype(BF16)
    lo = (r - mid.astype(F32)).astype(BF16)
    return hi, mid, lo


PROJ_CHUNK = 1024


def _rms_proj_kernel(x_ref, g_ref, w_ref, *o_refs, outs):
    x = x_ref[...]
    r = lax.rsqrt(jnp.mean(x * x, axis=-1, keepdims=True) + EPS)
    h = (x * r * g_ref[...]).astype(BF16)
    for o_ref, (start, width, split) in zip(o_refs, outs):
        step = split if split else min(PROJ_CHUNK, width)
        for n, c0 in enumerate(range(0, width, step)):
            cw = min(step, width - c0)
            y = _dot(h, w_ref[:, start + c0:start + c0 + cw]).astype(o_ref.dtype)
            if split:
                o_ref[n] = y
            else:
                o_ref[:, c0:c0 + cw] = y


def rms_proj(x, g, w, outs, *, tm=512):
    T, D = x.shape
    N = w.shape[1]
    tm = min(tm, T)
    shapes, specs = [], []
    for start, width, dtype, split in outs:
        if split:
            shapes.append(jax.ShapeDtypeStruct((width // split, T, split), dtype))
            specs.append(pl.BlockSpec((width // split, tm, split), lambda i: (0, i, 0)))
        else:
            shapes.append(jax.ShapeDtypeStruct((T, width), dtype))
            specs.append(pl.BlockSpec((tm, width), lambda i: (i, 0)))
    return pl.pallas_call(
        functools.partial(_rms_proj_kernel, outs=tuple((s, wd, sp) for s, wd, _, sp in outs)),
        out_shape=tuple(shapes),
        grid=(T // tm,),
        in_specs=[pl.BlockSpec((tm, D), lambda i: (i, 0)),
                  pl.BlockSpec((1, D), lambda i: (0, 0)),
                  pl.BlockSpec((D, N), lambda i: (0, 0), pipeline_mode=pl.Buffered(1))],
        out_specs=tuple(specs),
        compiler_params=_params("parallel"),
        name="rms_proj",
    )(x, g.reshape(1, D), w)


def _ffn_kernel(x_ref, y1_ref, y2_ref, wo1_ref, wo2_ref, g_ref, wg_ref, wu_ref, wd_ref, fin_ref, o_ref, h_ref,
                *, final_norm):
    @pl.when(pl.program_id(1) == 0)
    def _():
        x = x_ref[...] + _dot(y1_ref[...], wo1_ref[...]) + _dot(y2_ref[...], wo2_ref[...])
        r = lax.rsqrt(jnp.mean(x * x, axis=-1, keepdims=True) + EPS)
        h_ref[...] = (x * r * g_ref[...]).astype(BF16)
        o_ref[...] = x

    h = h_ref[...]
    a = _dot(h, wg_ref[...])
    u = _dot(h, wu_ref[...])
    act = (a * jax.nn.sigmoid(a) * u).astype(BF16)
    o_ref[...] += _dot(act, wd_ref[...])

    if final_norm:
        @pl.when(pl.program_id(1) == pl.num_programs(1) - 1)
        def _():
            y = o_ref[...]
            r = lax.rsqrt(jnp.mean(y * y, axis=-1, keepdims=True) + EPS)
            o_ref[...] = y * r * fin_ref[...]


def out_proj_ffn(x, y1, y2, wo1, wo2, g, wg, wu, wd, fin, *, final_norm, tm=512, tf=512):
    T, D = x.shape
    K1, K2 = y1.shape[1], y2.shape[1]
    FF = wg.shape[1]
    tm = min(tm, T)
    assert FF % tf == 0
    once = pl.Buffered(1)
    return pl.pallas_call(
        functools.partial(_ffn_kernel, final_norm=final_norm),
        out_shape=jax.ShapeDtypeStruct((T, D), F32),
        grid=(T // tm, FF // tf),
        in_specs=[pl.BlockSpec((tm, D), lambda i, f: (i, 0)),
                  pl.BlockSpec((tm, K1), lambda i, f: (i, 0)),
                  pl.BlockSpec((tm, K2), lambda i, f: (i, 0)),
                  pl.BlockSpec((K1, D), lambda i, f: (0, 0), pipeline_mode=once),
                  pl.BlockSpec((K2, D), lambda i, f: (0, 0), pipeline_mode=once),
                  pl.BlockSpec((1, D), lambda i, f: (0, 0)),
                  pl.BlockSpec((D, tf), lambda i, f: (0, f)),
                  pl.BlockSpec((D, tf), lambda i, f: (0, f)),
                  pl.BlockSpec((tf, D), lambda i, f: (f, 0)),
                  pl.BlockSpec((1, D), lambda i, f: (0, 0))],
        out_specs=pl.BlockSpec((tm, D), lambda i, f: (i, 0)),
        scratch_shapes=[pltpu.VMEM((tm, D), BF16)],
        compiler_params=_params("parallel", "arbitrary"),
        name="out_proj_ffn",
    )(x, y1, y2, wo1, wo2, g.reshape(1, D), wg, wu, wd, fin.reshape(1, D))


def _mem_attn_kernel(q_ref, k_ref, v_ref, o_ref):
    scale = HEAD_DIM ** -0.5
    sls = [slice(h * HEAD_DIM, (h + 1) * HEAD_DIM) for h in range(MEM_HEADS)]
    ss = [_dot_nt(q_ref[:, sl], k_ref[:, sl]) * scale for sl in sls]
    es = [jnp.exp(s - jnp.max(s, axis=-1, keepdims=True)) for s in ss]
    ps = [e / jnp.sum(e, axis=-1, keepdims=True) for e in es]
    for sl, p in zip(sls, ps):
        o_ref[:, sl] = _dot(p.astype(BF16), v_ref[:, sl]).astype(o_ref.dtype)


def mem_attention(u3, qblk, mem_kv, *, tq=512):
    B, S, _ = u3.shape
    M = mem_kv.shape[1]
    tq = min(tq, S)
    return pl.pallas_call(
        _mem_attn_kernel,
        out_shape=jax.ShapeDtypeStruct((B, S, MEM_WIDTH), BF16),
        grid=(B, S // tq),
        in_specs=[pl.BlockSpec((None, tq, MEM_WIDTH), lambda b, i: (b, i, qblk)),
                  pl.BlockSpec((None, M, MEM_WIDTH), lambda b, i: (b, 0, 0)),
                  pl.BlockSpec((None, M, MEM_WIDTH), lambda b, i: (b, 0, 1))],
        out_specs=pl.BlockSpec((None, tq, MEM_WIDTH), lambda b, i: (b, i, 0)),
        compiler_params=_params("parallel", "parallel"),
        name="mem_attention",
    )(u3, mem_kv, mem_kv)


LOG2E = 1.4426950408889634


def _stick_kernel(q_ref, k_ref, v_ref, o_ref, acc_ref, tail_ref, z0_ref, z1_ref, *, tq, tk):
    i = pl.program_id(2)
    z_refs = (z0_ref, z1_ref)
    c2 = HEAD_DIM ** -0.5 * LOG2E
    per_q = tq // tk
    nh = q_ref.shape[1] // HEAD_DIM
    lrow = lax.broadcasted_iota(jnp.int32, (2 * tk, tk), 0)
    lcol = lax.broadcasted_iota(jnp.int32, (2 * tk, tk), 1)
    later = ((lrow > lcol) & ((lrow < tk) | (lrow > lcol + tk))).astype(BF16)
    rows = lax.broadcasted_iota(jnp.int32, (tq, tk), 0)
    cols = lax.broadcasted_iota(jnp.int32, (tq, tk), 1)
    acc_ref[...] = jnp.zeros_like(acc_ref)
    tail_ref[...] = jnp.zeros_like(tail_ref)

    def scores(j, slot):
        off = pl.multiple_of(j * tk, tk)
        for hh in range(nh):
            sl = slice(hh * HEAD_DIM, (hh + 1) * HEAD_DIM)
            z_refs[slot][hh] = _dot_nt(q_ref[:, sl], k_ref[pl.ds(off, tk), sl])

    def step(j, slot, diag):
        off = pl.multiple_of(j * tk, tk)
        if diag is not None:
            causal = cols + diag * tk < rows
        scores(jnp.maximum(j - 1, 0), 1 - slot)
        part = []
        for hh in range(nh):
            ns = z_refs[slot][hh] * (-c2)
            log_keep = jnp.minimum(ns, 0.0) - jnp.log2(1.0 + jnp.exp2(-jnp.abs(ns)))
            if diag is not None:
                log_keep = jnp.where(causal, log_keep, 0.0)
            hi, lo = _split2(log_keep)
            after = _dot(jnp.concatenate([hi, lo], axis=1), later) + tail_ref[hh]
            part.append((ns, log_keep, after))
        for hh in range(nh):
            sl = slice(hh * HEAD_DIM, (hh + 1) * HEAD_DIM)
            ns, log_keep, after = part[hh]
            a = jnp.exp2(log_keep + after - ns)
            if diag is not None:
                a = jnp.where(causal, a, 0.0)
            acc_ref[hh] += _dot(a.astype(BF16), v_ref[pl.ds(off, tk), sl])
            tail_ref[hh] = after[:, 0:1] + log_keep[:, 0:1]

    assert per_q % 2 == 0
    scores((i + 1) * per_q - 1, 1)
    for d in reversed(range(per_q)):
        step(i * per_q + d, d % 2, d)

    @pl.loop(0, i * (per_q // 2))
    def _(jj):
        step(i * per_q - 1 - 2 * jj, 1, None)
        step(i * per_q - 2 - 2 * jj, 0, None)

    for hh in range(nh):
        o_ref[:, hh * HEAD_DIM:(hh + 1) * HEAD_DIM] = acc_ref[hh].astype(o_ref.dtype)


def stick_breaking(u3, *, tq=512, tk=256, nh=2):
    B, S, _ = u3.shape
    H = B_HEADS // nh
    tq = min(tq, S)
    tk = min(tk, tq)
    hw = nh * HEAD_DIM
    return pl.pallas_call(
        functools.partial(_stick_kernel, tq=tq, tk=tk),
        out_shape=jax.ShapeDtypeStruct((B, S, MIX_WIDTH), BF16),
        grid=(B, H, S // tq),
        in_specs=[pl.BlockSpec((None, tq, hw), lambda b, h, i: (b, i, h)),
                  pl.BlockSpec((None, S, hw), lambda b, h, i: (b, 0, H + h)),
                  pl.BlockSpec((None, S, hw), lambda b, h, i: (b, 0, 2 * H + h))],
        out_specs=pl.BlockSpec((None, tq, hw), lambda b, h, i: (b, i, h)),
        scratch_shapes=[pltpu.VMEM((nh, tq, HEAD_DIM), F32), pltpu.VMEM((nh, tq, 1), F32),
                        pltpu.VMEM((nh, tq, tk), F32), pltpu.VMEM((nh, tq, tk), F32)],
        compiler_params=_params("parallel", "parallel", "arbitrary"),
        name="stick_breaking",
    )(u3, u3, u3)


def _swa_kernel(sink_ref, q_ref, kvp_ref, kvc_ref, o_ref, *, tq):
    g = pl.program_id(1)
    i = pl.program_id(2)
    dh = D_HEAD_DIM
    scale = dh ** -0.5
    kv = jnp.concatenate([kvp_ref[...], kvc_ref[...]], axis=0)
    k = kv[:, :dh]
    v = kv[:, dh:]
    r = lax.broadcasted_iota(jnp.int32, (tq, 2 * tq), 0)
    c = lax.broadcasted_iota(jnp.int32, (tq, 2 * tq), 1)
    rel = r - c + tq
    mask = (rel >= 0) & (rel < D_WINDOW) & ((c >= tq) | (i > 0))
    heads = range(D_HPG)
    sks = [sink_ref[g * D_HPG + hh] for hh in heads]
    ss = [jnp.where(mask, _dot_nt(q_ref[:, hh * dh:(hh + 1) * dh], k) * scale, NEG_INF) for hh in heads]
    ms = [jnp.maximum(jnp.max(ss[hh], axis=-1, keepdims=True), sks[hh]) for hh in heads]
    es = [jnp.exp(ss[hh] - ms[hh]) for hh in heads]
    ps = [es[hh] / (jnp.sum(es[hh], axis=-1, keepdims=True) + jnp.exp(sks[hh] - ms[hh])) for hh in heads]
    for hh in heads:
        o_ref[:, hh * dh:(hh + 1) * dh] = _dot(ps[hh].astype(BF16), v).astype(o_ref.dtype)


def sink_window(u3, sinks):
    B, S, _ = u3.shape
    tq = D_WINDOW
    gw = D_HPG * D_HEAD_DIM
    kv0 = (MIX_WIDTH + MEM_WIDTH) // (2 * D_HEAD_DIM)
    return pl.pallas_call(
        functools.partial(_swa_kernel, tq=tq),
        out_shape=jax.ShapeDtypeStruct((B, S, MIX_WIDTH), BF16),
        grid=(B, D_KV, S // tq),
        in_specs=[pl.BlockSpec(memory_space=pltpu.SMEM),
                  pl.BlockSpec((None, tq, gw), lambda b, g, i: (b, i, g)),
                  pl.BlockSpec((None, tq, 2 * D_HEAD_DIM),
                               lambda b, g, i: (b, jnp.maximum(i - 1, 0), kv0 + g)),
                  pl.BlockSpec((None, tq, 2 * D_HEAD_DIM), lambda b, g, i: (b, i, kv0 + g))],
        out_specs=pl.BlockSpec((None, tq, gw), lambda b, g, i: (b, i, g)),
        compiler_params=_params("parallel", "parallel", "arbitrary"),
        name="sink_window",
    )(sinks, u3, u3, u3)


CONV_HALO = 8
CONV_CHUNK = 512


def _proj_conv_kernel(x_ref, g_ref, w_ref, cw_ref, cb_ref, cs_ref, qk_ref, u_ref, um_ref,
                      halo_ref, buf_ref, *, tiles_per_seq, n_conv, n_f32):
    tm = x_ref.shape[0]
    hl = CONV_HALO
    x = x_ref[...]
    r = lax.rsqrt(jnp.mean(x * x, axis=-1, keepdims=True) + EPS)
    h = (x * r * g_ref[...]).astype(BF16)
    seq_start = pl.program_id(0) % tiles_per_seq == 0

    @pl.when(pl.program_id(0) == 0)
    def _():
        halo_ref[...] = jnp.zeros_like(halo_ref)

    for c0 in range(0, n_conv, CONV_CHUNK):
        sl = slice(c0, c0 + CONV_CHUNK)
        y = _dot(h, w_ref[:, sl])
        prev = halo_ref[:, sl]
        buf_ref[0:hl, :] = jnp.where(seq_start, jnp.zeros_like(prev), prev)
        buf_ref[hl:hl + tm, :] = y
        halo_ref[:, sl] = y[tm - hl:tm, :]
        acc = cb_ref[:, sl] + cw_ref[C_CONV - 1:C_CONV, sl] * y
        for j in range(C_CONV - 1):
            sh = C_CONV - 1 - j
            acc = acc + cw_ref[j:j + 1, sl] * buf_ref[hl - sh:hl - sh + tm, :]
        qk_ref[:, sl] = (acc * jax.nn.sigmoid(acc) * cs_ref[:, sl]).astype(qk_ref.dtype)
    for c0 in range(0, n_f32, PROJ_CHUNK):
        cw = min(PROJ_CHUNK, n_f32 - c0)
        u_ref[:, c0:c0 + cw] = _dot(h, w_ref[:, n_conv + c0:n_conv + c0 + cw])
    um_ref[...] = _dot(h, w_ref[:, n_conv + n_f32:]).astype(um_ref.dtype)


def proj_conv(x, g, w, conv_w, conv_b, col_scale, S, n_f32, *, tm=256):
    T, D = x.shape
    N = w.shape[1]
    n_conv = conv_w.shape[1]
    tm = min(tm, S)
    assert S % tm == 0 and n_conv % CONV_CHUNK == 0 and tm >= CONV_HALO
    n_mem = N - n_conv - n_f32
    row = lambda i: (i, 0)
    fixed = lambda i: (0, 0)
    return pl.pallas_call(
        functools.partial(_proj_conv_kernel, tiles_per_seq=S // tm, n_conv=n_conv, n_f32=n_f32),
        out_shape=(jax.ShapeDtypeStruct((T, n_conv), BF16), jax.ShapeDtypeStruct((T, n_f32), F32),
                   jax.ShapeDtypeStruct((T, n_mem), BF16)),
        grid=(T // tm,),
        in_specs=[pl.BlockSpec((tm, D), row),
                  pl.BlockSpec((1, D), fixed),
                  pl.BlockSpec((D, N), fixed, pipeline_mode=pl.Buffered(1)),
                  pl.BlockSpec((C_CONV, n_conv), fixed),
                  pl.BlockSpec((1, n_conv), fixed),
                  pl.BlockSpec((1, n_conv), fixed)],
        out_specs=(pl.BlockSpec((tm, n_conv), row), pl.BlockSpec((tm, n_f32), row),
                   pl.BlockSpec((tm, n_mem), row)),
        scratch_shapes=[pltpu.VMEM((CONV_HALO, n_conv), F32), pltpu.VMEM((tm + CONV_HALO, CONV_CHUNK), F32)],
        compiler_params=_params("arbitrary"),
        name="proj_conv",
    )(x, g.reshape(1, D), w, conv_w, conv_b.reshape(1, n_conv), col_scale.reshape(1, n_conv))


def _mlstm_kernel(q_ref, k_ref, v_ref, o_ref, gt_ref, gb_ref, hn_ref, out_ref,
                  state_ref, m_ref, *, L):
    H, dh = C_HEADS, C_HEAD_DIM
    ext = dh + LANES

    @pl.when(pl.program_id(1) == 0)
    def _():
        state_ref[...] = jnp.zeros_like(state_ref)
        m_ref[...] = jnp.zeros_like(m_ref)

    rows = lax.broadcasted_iota(jnp.int32, (L, L), 0)
    cols = lax.broadcasted_iota(jnp.int32, (L, L), 1)
    tril = rows >= cols
    eye = rows == cols
    gates = gt_ref[...] + gb_ref[...]
    log_f = jnp.minimum(gates, 0.0) - jnp.log1p(jnp.exp(-jnp.abs(gates)))
    f_hi, f_mid, f_lo = _split3(log_f)
    trilb = tril.astype(BF16)
    bsum = _dot(trilb, f_hi) + _dot(trilb, f_mid) + _dot(trilb, f_lo)
    ones_col = (lax.broadcasted_iota(jnp.int32, (L, LANES), 1) == 0).astype(F32)

    for h in range(H):
        sl = slice(h * dh, (h + 1) * dh)
        q = q_ref[:, sl]
        k = k_ref[:, sl]
        v_ext = jnp.concatenate([v_ref[:, sl], ones_col], axis=1)
        ic = gates[:, h:h + 1]
        bc = bsum[:, H + h:H + h + 1]
        m_prev = m_ref[h][0:1, 0:1]
        rc = ic - bc
        rrow = jnp.sum(jnp.where(eye, rc, 0.0), axis=0, keepdims=True)
        dm = jnp.where(tril, bc + rrow, NEG_INF)
        m_inter = bc + m_prev
        m_t = jnp.maximum(m_inter, jnp.max(dm, axis=-1, keepdims=True))
        w = jnp.exp(dm - m_t)
        a_inter = jnp.exp(m_inter - m_t)
        sqk = _dot_nt(q, k) * w
        state = state_ref[h]
        num = a_inter * _dot(q, state.astype(BF16)) + _dot(sqk.astype(BF16), v_ext.astype(BF16))
        den = jnp.maximum(jnp.abs(num[:, dh:dh + 1]), jnp.exp(-m_t))
        hid = num[:, :dh] / den
        hid = jax.nn.sigmoid(o_ref[:, sl]) * hid
        hid = hid * lax.rsqrt(jnp.mean(hid * hid, axis=-1, keepdims=True) + EPS)
        out_ref[:, sl] = (hid * hn_ref[:, sl]).astype(out_ref.dtype)

        b_last = bc[L - 1:L, :]
        gk = b_last - bc + ic
        m_new = jnp.maximum(b_last + m_prev, jnp.max(gk, axis=0, keepdims=True))
        wk = jnp.exp(gk - m_new)
        decay = jnp.exp(b_last + m_prev - m_new)
        state_ref[h] = decay * state + _dot_tn(k, (wk * v_ext).astype(BF16))
        m_ref[h] = jnp.broadcast_to(m_new, m_ref.shape[1:])


def mlstm(qk3, u3, gate_b, head_norm, *, L=C_CHUNK):
    B, S, _ = u3.shape
    W = MIX_WIDTH
    H, dh = C_HEADS, C_HEAD_DIM
    L = min(L, S)
    gblk = 2 * W // LANES
    gb = jnp.zeros((1, LANES), F32).at[0, :2 * H].set(gate_b)
    return pl.pallas_call(
        functools.partial(_mlstm_kernel, L=L),
        out_shape=jax.ShapeDtypeStruct((B, S, W), BF16),
        grid=(B, S // L),
        in_specs=[pl.BlockSpec((None, L, W), lambda b, c: (b, c, 0)),
                  pl.BlockSpec((None, L, W), lambda b, c: (b, c, 1)),
                  pl.BlockSpec((None, L, W), lambda b, c: (b, c, 0)),
                  pl.BlockSpec((None, L, W), lambda b, c: (b, c, 1)),
                  pl.BlockSpec((None, L, LANES), lambda b, c: (b, c, gblk)),
                  pl.BlockSpec((1, LANES), lambda b, c: (0, 0)),
                  pl.BlockSpec((1, W), lambda b, c: (0, 0))],
        out_specs=pl.BlockSpec((None, L, W), lambda b, c: (b, c, 0)),
        scratch_shapes=[pltpu.VMEM((H, dh, dh + LANES), F32),
                        pltpu.VMEM((H, 8, LANES), F32)],
        compiler_params=_params("parallel", "arbitrary"),
        name="mlstm",
    )(qk3, qk3, u3, u3, u3, gb, head_norm.reshape(1, W))


def _compress_kernel(x_ref, pe_ref, w1_ref, w2_ref, o_ref):
    half = w1_ref.shape[0] // 2
    x = x_ref[...]
    n = x.shape[0]
    first = _dot(x, w1_ref[0:half, :])
    second = _dot(x, w1_ref[half:, :])
    pe = jnp.broadcast_to(pe_ref[...], (8, pe_ref.shape[1])).astype(BF16)
    bias = _dot(pe, w1_ref[...])[0:1, :]
    hid = first + pltpu.roll(second, n - 1, 0) + bias
    o_ref[...] = _dot(jax.nn.gelu(hid).astype(BF16), w2_ref[...]).astype(o_ref.dtype)


def compress(xc, pe, w1, w2):
    _, G, B, n, width = xc.shape
    hid = w1.shape[2]
    return pl.pallas_call(
        _compress_kernel,
        out_shape=jax.ShapeDtypeStruct((2, G, B, n, HEAD_DIM), BF16),
        grid=(2, G, B),
        in_specs=[pl.BlockSpec((None, None, None, n, width), lambda j, g, b: (j, g, b, 0, 0)),
                  pl.BlockSpec((None, 1, 2 * width), lambda j, g, b: (j, 0, 0)),
                  pl.BlockSpec((None, 2 * width, hid), lambda j, g, b: (j, 0, 0)),
                  pl.BlockSpec((None, hid, HEAD_DIM), lambda j, g, b: (j, 0, 0))],
        out_specs=pl.BlockSpec((None, None, None, n, HEAD_DIM), lambda j, g, b: (j, g, b, 0, 0)),
        compiler_params=_params("parallel", "parallel", "parallel"),
        name="nsa_compress",
    )(xc, pe, w1, w2)


def _cmp_select_kernel(q_ref, kc_ref, vc_ref, o_ref, sel_ref, *, tq, n_cmp, n_slc):
    i = pl.program_id(2)
    scale = HEAD_DIM ** -0.5
    npad = kc_ref.shape[0]
    kc = kc_ref[...]
    vc = vc_ref[...]
    pos = i * tq + lax.broadcasted_iota(jnp.int32, (tq, npad), 0)
    nidx = lax.broadcasted_iota(jnp.int32, (tq, npad), 1)
    cmask = (nidx * A_CMP_STRIDE + (A_CMP_LEN - 1) <= pos) & (nidx < n_cmp)
    heads = range(A_HPG)
    ss = [jnp.where(cmask, _dot_nt(q_ref[:, h * HEAD_DIM:(h + 1) * HEAD_DIM], kc) * scale, NEG_INF)
          for h in heads]
    ms = [jnp.max(s, axis=-1, keepdims=True) for s in ss]
    es = [jnp.where(cmask, jnp.exp(ss[h] - ms[h]), 0.0) for h in heads]
    ps = [es[h] / jnp.maximum(jnp.sum(es[h], axis=-1, keepdims=True), 1e-30) for h in heads]
    psum = ps[0]
    for h in heads:
        o_ref[:, h * HEAD_DIM:(h + 1) * HEAD_DIM] = _dot(ps[h].astype(BF16), vc)
        if h:
            psum = psum + ps[h]

    on = lax.broadcasted_iota(jnp.int32, (npad, LANES), 0) * A_CMP_STRIDE
    oj = lax.broadcasted_iota(jnp.int32, (npad, LANES), 1)
    overlap = ((on < (oj + 1) * A_SEL_LEN) & (on + A_CMP_LEN > oj * A_SEL_LEN)
               & (on < n_cmp * A_CMP_STRIDE) & (oj < n_slc)).astype(BF16)
    p_hi, p_lo = _split2(psum)
    imp = _dot(p_hi, overlap) + _dot(p_lo, overlap)

    sub = 8
    blk = lax.broadcasted_iota(jnp.int32, (LANES, tq), 0)
    cur = (i * tq + lax.broadcasted_iota(jnp.int32, (LANES, tq), 1)) // A_SEL_LEN
    valid = blk <= cur
    forced = (blk == 0) | (blk == cur) | (blk == cur - 1)
    score = jnp.where(valid, jnp.where(forced, A_FORCE, imp.T), A_EXCLUDE)
    row8 = lax.broadcasted_iota(jnp.int32, (sub, tq), 0)
    n_top = min(A_TOP_N, n_slc)
    chosen = []
    for v in range(0, n_slc, sub):
        mine = score[v:v + sub, :]
        rank = jnp.zeros((sub, tq), jnp.int32)
        for j in range(n_slc):
            other = score[j:j + 1, :]
            if v > j:
                ahead = other >= mine
            elif v + sub - 1 <= j:
                ahead = other > mine
            else:
                ahead = (other > mine) | ((other == mine) & (row8 + v > j))
            rank = rank + ahead.astype(jnp.int32)
        chosen.append(((rank < n_top) & valid[v:v + sub, :]).astype(F32))
    if n_slc < LANES:
        chosen.append(jnp.zeros((LANES - n_slc, tq), F32))
    sel_ref[...] = jnp.concatenate(chosen, axis=0).T.astype(sel_ref.dtype)


def cmp_select(u3, kvc, *, tq=128):
    B, S, _ = u3.shape
    G = A_KV
    npad = kvc.shape[3]
    n_cmp = (S - A_CMP_LEN) // A_CMP_STRIDE + 1
    n_slc = S // A_SEL_LEN
    tq = min(tq, S)
    gw = A_HPG * HEAD_DIM
    return pl.pallas_call(
        functools.partial(_cmp_select_kernel, tq=tq, n_cmp=n_cmp, n_slc=n_slc),
        out_shape=(jax.ShapeDtypeStruct((B, S, MIX_WIDTH), F32),
                   jax.ShapeDtypeStruct((B, G, S, LANES), BF16)),
        grid=(B, G, S // tq),
        in_specs=[pl.BlockSpec((None, tq, gw), lambda b, g, i: (b, i, g)),
                  pl.BlockSpec((None, None, None, npad, HEAD_DIM), lambda b, g, i: (0, g, b, 0, 0)),
                  pl.BlockSpec((None, None, None, npad, HEAD_DIM), lambda b, g, i: (1, g, b, 0, 0))],
        out_specs=(pl.BlockSpec((None, tq, gw), lambda b, g, i: (b, i, g)),
                   pl.BlockSpec((None, None, tq, LANES), lambda b, g, i: (b, g, i, 0))),
        compiler_params=_params("parallel", "parallel", "parallel"),
        name="nsa_cmp_select",
    )(u3, kvc, kvc)


def _masked_scores(qs, k, mask, scale):
    hp = qs.shape[0] // mask.shape[0]
    s = _dot_nt(qs, k) * scale
    return jnp.where(mask[None], s.reshape((hp,) + mask.shape), NEG_INF).reshape(s.shape)


def _slc_win_kernel(q_ref, ks_ref, vs_ref, kw_ref, vw_ref, sel_ref, oc_ref, gt_ref, o_ref, *, tq, tk, wk):
    i = pl.program_id(2)
    S = ks_ref.shape[0]
    scale = HEAD_DIM ** -0.5 * LOG2E
    hp = A_HPG
    qs = jnp.concatenate([q_ref[:, h * HEAD_DIM:(h + 1) * HEAD_DIM] for h in range(hp)], axis=0)
    sel = sel_ref[...]
    qpos = i * tq + lax.broadcasted_iota(jnp.int32, (tq, tk), 0)
    c = lax.broadcasted_iota(jnp.int32, (tq, tk), 1)
    eb = lax.broadcasted_iota(jnp.int32, (LANES, tk), 0)
    ec = lax.broadcasted_iota(jnp.int32, (LANES, tk), 1) // A_SEL_LEN

    def slc_body(j, carry):
        m, l, acc = carry
        off = pl.multiple_of(j * tk, tk)
        expand = (eb == j * (tk // A_SEL_LEN) + ec).astype(BF16)
        mask = (_dot(sel, expand) > 0.5) & (off + c <= qpos)
        s = _masked_scores(qs, ks_ref[pl.ds(off, tk), :], mask, scale)
        m_new = jnp.maximum(m, jnp.max(s, axis=-1, keepdims=True))
        p = jnp.exp2(s - m_new)
        alpha = jnp.exp2(m - m_new)
        l = alpha * l + jnp.sum(p, axis=-1, keepdims=True)
        acc = alpha * acc + _dot(p.astype(BF16), vs_ref[pl.ds(off, tk), :])
        return m_new, l, acc

    init = (jnp.full((hp * tq, 1), NEG_INF, F32), jnp.zeros((hp * tq, 1), F32),
            jnp.zeros((hp * tq, HEAD_DIM), F32))
    _, l_s, acc_s = lax.fori_loop(0, (i * tq + tq + tk - 1) // tk, slc_body, init)
    o_slc = acc_s / l_s

    start = pl.multiple_of(jnp.clip(i * tq + tq - wk, 0, S - wk), tq)
    wq = i * tq + lax.broadcasted_iota(jnp.int32, (tq, wk), 0)
    rel = wq - (start + lax.broadcasted_iota(jnp.int32, (tq, wk), 1))
    s = _masked_scores(qs, kw_ref[pl.ds(start, wk), :], (rel >= 0) & (rel < A_WINDOW), scale)
    p = jnp.exp2(s - jnp.max(s, axis=-1, keepdims=True))
    o_win = _dot(p.astype(BF16), vw_ref[pl.ds(start, wk), :]) / jnp.sum(p, axis=-1, keepdims=True)

    gate = jax.nn.sigmoid(gt_ref[...])
    for h in range(hp):
        sl = slice(h * HEAD_DIM, (h + 1) * HEAD_DIM)
        rs = slice(h * tq, (h + 1) * tq)
        o = (gate[:, h:h + 1] * oc_ref[:, sl]
             + gate[:, hp + h:hp + h + 1] * o_slc[rs]
             + gate[:, 2 * hp + h:2 * hp + h + 1] * o_win[rs])
        o_ref[:, sl] = o.astype(o_ref.dtype)


def slc_win(u3, sel, o_cmp, gate3, *, tq=128, tk=512):
    B, S, _ = u3.shape
    G = A_KV
    tq = min(tq, S)
    tk = min(tk, S)
    wk = min(A_WINDOW + tq, S)
    assert S % tk == 0 and tk % tq == 0 and S // A_SEL_LEN <= LANES
    gw = A_HPG * HEAD_DIM
    kv0 = A_HEADS

    def kv_spec(which):
        return pl.BlockSpec((None, S, HEAD_DIM), lambda b, g, i: (b, 0, kv0 + which * G + g))

    return pl.pallas_call(
        functools.partial(_slc_win_kernel, tq=tq, tk=tk, wk=wk),
        out_shape=jax.ShapeDtypeStruct((B, S, MIX_WIDTH), BF16),
        grid=(B, G, S // tq),
        in_specs=[pl.BlockSpec((None, tq, gw), lambda b, g, i: (b, i, g)),
                  kv_spec(0), kv_spec(1), kv_spec(2), kv_spec(3),
                  pl.BlockSpec((None, None, tq, LANES), lambda b, g, i: (b, g, i, 0)),
                  pl.BlockSpec((None, tq, gw), lambda b, g, i: (b, i, g)),
                  pl.BlockSpec((None, tq, LANES), lambda b, g, i: (b, i, g))],
        out_specs=pl.BlockSpec((None, tq, gw), lambda b, g, i: (b, i, g)),
        compiler_params=_params("parallel", "parallel", "arbitrary"),
        name="nsa_slc_win",
    )(u3, u3, u3, u3, u3, sel, o_cmp, gate3)


def _pad_cols(w, n):
    return jnp.pad(w, ((0, 0), (0, n - w.shape[1])))


def nsa_layer_mix(x, B, S, norm, w_in, pe_k, w1_k, w2_k, pe_v, w1_v, w2_v, mem_kv):
    H, G, dh = A_HEADS, A_KV, HEAD_DIM
    qw = H * dh
    kvw = 6 * G * dh
    gw = 3 * H
    wg = w_in[:, qw + kvw:qw + kvw + gw].reshape(-1, G, A_HPG, 3).transpose(0, 1, 3, 2)
    wg = jnp.pad(wg.reshape(-1, G, 3 * A_HPG), ((0, 0), (0, 0), (0, LANES - 3 * A_HPG)))
    n_main = qw + 4 * G * dh + MEM_WIDTH
    n_cmp = 2 * G * dh
    w_all = jnp.concatenate([w_in[:, :qw], w_in[:, qw + n_cmp:qw + kvw], w_in[:, qw + kvw + gw:],
                             w_in[:, qw:qw + n_cmp], wg.reshape(-1, G * LANES)], axis=1).astype(BF16)

    u, xc, gate3 = rms_proj(x, norm, w_all, [(0, n_main, BF16, None), (n_main, n_cmp, BF16, dh),
                                             (n_main + n_cmp, G * LANES, F32, None)])
    u = u.reshape(B, S, -1)
    gate3 = gate3.reshape(B, S, -1)
    xc = xc.reshape(2, G, B, S // A_CMP_STRIDE, A_CMP_STRIDE * dh)
    pe = jnp.stack([pe_k.reshape(1, -1), pe_v.reshape(1, -1)])
    kvc = compress(xc, pe, jnp.stack([w1_k, w1_v]).astype(BF16), jnp.stack([w2_k, w2_v]).astype(BF16))
    o_cmp, sel = cmp_select(u, kvc)
    mixed = slc_win(u, sel, o_cmp, gate3)
    mem_out = mem_attention(u, (qw + 4 * G * dh) // MEM_WIDTH, mem_kv)
    return mixed, mem_out


def stick_layer_mix(x, B, S, norm, w_in, mem_kv):
    (u,) = rms_proj(x, norm, w_in.astype(BF16), [(0, w_in.shape[1], BF16, None)])
    u = u.reshape(B, S, -1)
    mixed = stick_breaking(u)
    mem_out = mem_attention(u, 3 * MIX_WIDTH // MEM_WIDTH, mem_kv)
    return mixed, mem_out


def mlstm_layer_mix(x, B, S, norm, w_in, conv_w, conv_b, gate_b, head_norm, mem_kv):
    W = MIX_WIDTH
    nf = 4 * W + 2 * C_HEADS
    n_f = 4 * W + LANES
    w_all = jnp.concatenate([_pad_cols(w_in[:, :nf], n_f), w_in[:, nf:]], axis=1).astype(BF16)
    col_scale = jnp.concatenate([jnp.ones((W,), F32), jnp.full((W,), C_HEAD_DIM ** -0.5, F32)])
    qk, u, um = proj_conv(x, norm, w_all, conv_w, conv_b, col_scale, S, n_f - 2 * W)
    qk = qk.reshape(B, S, -1)
    u = u.reshape(B, S, -1)
    um = um.reshape(B, S, -1)
    mixed = mlstm(qk, u, gate_b, head_norm)
    mem_out = mem_attention(um, 0, mem_kv)
    return mixed, mem_out


def swa_layer_mix(x, B, S, norm, w_in, sinks, mem_kv):
    W = MIX_WIDTH
    kw = D_KV * D_HEAD_DIM
    wk = w_in[:, W:W + kw].reshape(-1, D_KV, D_HEAD_DIM)
    wv = w_in[:, W + kw:W + 2 * kw].reshape(-1, D_KV, D_HEAD_DIM)
    wkv = jnp.concatenate([wk, wv], axis=2).reshape(-1, 2 * kw)
    w_main = jnp.concatenate([w_in[:, :W], w_in[:, W + 2 * kw:], wkv], axis=1)
    w_main = _pad_cols(w_main, W + MEM_WIDTH + 4 * LANES).astype(BF16)
    (u,) = rms_proj(x, norm, w_main, [(0, w_main.shape[1], BF16, None)])
    u = u.reshape(B, S, -1)
    mixed = sink_window(u, sinks)
    mem_out = mem_attention(u, W // MEM_WIDTH, mem_kv)
    return mixed, mem_out


def kernel(x, mem, mem_norm, mem_w_kv, l0_norm_mix, l0_w_in, l0_cmp_pe_k, l0_cmp_w1_k, l0_cmp_w2_k, l0_cmp_pe_v, l0_cmp_w1_v, l0_cmp_w2_v, l0_w_out, l0_norm_ffn, l0_w_gate, l0_w_up, l0_w_down, l1_norm_mix, l1_w_in, l1_w_out, l1_norm_ffn, l1_w_gate, l1_w_up, l1_w_down, l2_norm_mix, l2_w_in, l2_conv_w, l2_conv_b, l2_gate_b, l2_head_norm, l2_w_out, l2_norm_ffn, l2_w_gate, l2_w_up, l2_w_down, l3_norm_mix, l3_w_in, l3_sinks, l3_w_out, l3_norm_ffn, l3_w_gate, l3_w_up, l3_w_down, final_norm):
    B, S, D = x.shape
    M = mem.shape[1]
    (mem_kv,) = rms_proj(mem.reshape(B * M, D), mem_norm, mem_w_kv.astype(BF16),
                         [(0, mem_w_kv.shape[1], BF16, None)])
    mem_kv = mem_kv.reshape(B, M, -1)
    xs = x.reshape(B * S, D)

    def finish(xs, mixed, mem_out, w_out, norm_ffn, w_gate, w_up, w_down, last=False):
        w_out = w_out.astype(BF16)
        return out_proj_ffn(xs, mixed.reshape(B * S, -1), mem_out.reshape(B * S, -1),
                            w_out[:MIX_WIDTH], w_out[MIX_WIDTH:], norm_ffn,
                            w_gate.astype(BF16), w_up.astype(BF16), w_down.astype(BF16),
                            final_norm, final_norm=last)

    mixed, mem_out = nsa_layer_mix(xs, B, S, l0_norm_mix, l0_w_in, l0_cmp_pe_k, l0_cmp_w1_k, l0_cmp_w2_k,
                                   l0_cmp_pe_v, l0_cmp_w1_v, l0_cmp_w2_v, mem_kv)
    xs = finish(xs, mixed, mem_out, l0_w_out, l0_norm_ffn, l0_w_gate, l0_w_up, l0_w_down)
    mixed, mem_out = stick_layer_mix(xs, B, S, l1_norm_mix, l1_w_in, mem_kv)
    xs = finish(xs, mixed, mem_out, l1_w_out, l1_norm_ffn, l1_w_gate, l1_w_up, l1_w_down)
    mixed, mem_out = mlstm_layer_mix(xs, B, S, l2_norm_mix, l2_w_in, l2_conv_w, l2_conv_b, l2_gate_b,
                                     l2_head_norm, mem_kv)
    xs = finish(xs, mixed, mem_out, l2_w_out, l2_norm_ffn, l2_w_gate, l2_w_up, l2_w_down)
    mixed, mem_out = swa_layer_mix(xs, B, S, l3_norm_mix, l3_w_in, l3_sinks, mem_kv)
    xs = finish(xs, mixed, mem_out, l3_w_out, l3_norm_ffn, l3_w_gate, l3_w_up, l3_w_down, last=True)
    return xs.reshape(B, S, D)
```

```python
import functools

import jax
import jax.numpy as jnp
from jax import lax
from jax.experimental import pallas as pl
from jax.experimental.pallas import tpu as pltpu

F32 = jnp.float32
BF16 = jnp.bfloat16

LANES = 128
VMEM_LIMIT = 56 * 1024 * 1024

D_MODEL = 2048
HEAD_DIM = 128
MEM_HEADS = 4
MEM_WIDTH = MEM_HEADS * HEAD_DIM
MIX_WIDTH = D_MODEL - MEM_WIDTH
EPS = 1e-6
NEG_INF = -1e30

A_HEADS = MIX_WIDTH // HEAD_DIM
A_KV = 2
A_HPG = A_HEADS // A_KV
A_CMP_LEN = 32
A_CMP_STRIDE = 16
A_SEL_LEN = 64
A_TOP_N = 16
A_WINDOW = 512
A_FORCE = 1e6
A_EXCLUDE = -1e9

B_HEADS = MIX_WIDTH // HEAD_DIM

C_HEADS = 4
C_HEAD_DIM = MIX_WIDTH // C_HEADS
C_CONV = 4
C_CHUNK = 256

D_HEAD_DIM = 64
D_HEADS = MIX_WIDTH // D_HEAD_DIM
D_KV = D_HEADS // 8
D_HPG = D_HEADS // D_KV
D_WINDOW = 128


def _params(*sem):
    return pltpu.CompilerParams(dimension_semantics=sem, vmem_limit_bytes=VMEM_LIMIT)


def _dot(a, b):
    return jnp.dot(a, b, preferred_element_type=F32)


def _dot_nt(a, b):
    return lax.dot_general(a, b, (((1,), (1,)), ((), ())), preferred_element_type=F32)


def _dot_tn(a, b):
    return lax.dot_general(a, b, (((0,), (0,)), ((), ())), preferred_element_type=F32)


def _split2(x):
    hi = x.astype(BF16)
    lo = (x - hi.astype(F32)).astype(BF16)
    return hi, lo


def _split3(x):
    hi = x.astype(BF16)
    r = x - hi.astype(F32)
    mid = r.astype(BF16)
    lo = (r - mid.astype(F32)).astype(BF16)
    return hi, mid, lo


PROJ_CHUNK = 1024


def _rms_proj_kernel(x_ref, g_ref, w_ref, *o_refs, outs):
    x = x_ref[...]
    r = lax.rsqrt(jnp.mean(x * x, axis=-1, keepdims=True) + EPS)
    h = (x * r * g_ref[...]).astype(BF16)
    for o_ref, (start, width, split) in zip(o_refs, outs):
        step = split if split else min(PROJ_CHUNK, width)
        for n, c0 in enumerate(range(0, width, step)):
            cw = min(step, width - c0)
            y = _dot(h, w_ref[:, start + c0:start + c0 + cw]).astype(o_ref.dtype)
            if split:
                o_ref[n] = y
            else:
                o_ref[:, c0:c0 + cw] = y


def rms_proj(x, g, w, outs, *, tm=512):
    T, D = x.shape
    N = w.shape[1]
    tm = min(tm, T)
    shapes, specs = [], []
    for start, width, dtype, split in outs:
        if split:
            shapes.append(jax.ShapeDtypeStruct((width // split, T, split), dtype))
            specs.append(pl.BlockSpec((width // split, tm, split), lambda i: (0, i, 0)))
        else:
            shapes.append(jax.ShapeDtypeStruct((T, width), dtype))
            specs.append(pl.BlockSpec((tm, width), lambda i: (i, 0)))
    return pl.pallas_call(
        functools.partial(_rms_proj_kernel, outs=tuple((s, wd, sp) for s, wd, _, sp in outs)),
        out_shape=tuple(shapes),
        grid=(T // tm,),
        in_specs=[pl.BlockSpec((tm, D), lambda i: (i, 0)),
                  pl.BlockSpec((1, D), lambda i: (0, 0)),
                  pl.BlockSpec((D, N), lambda i: (0, 0), pipeline_mode=pl.Buffered(1))],
        out_specs=tuple(specs),
        compiler_params=_params("parallel"),
        name="rms_proj",
    )(x, g.reshape(1, D), w)


def _ffn_kernel(x_ref, y1_ref, y2_ref, wo1_ref, wo2_ref, g_ref, wg_ref, wu_ref, wd_ref, fin_ref, o_ref, h_ref,
                *, final_norm):
    @pl.when(pl.program_id(1) == 0)
    def _():
        x = x_ref[...] + _dot(y1_ref[...], wo1_ref[...]) + _dot(y2_ref[...], wo2_ref[...])
        r = lax.rsqrt(jnp.mean(x * x, axis=-1, keepdims=True) + EPS)
        h_ref[...] = (x * r * g_ref[...]).astype(BF16)
        o_ref[...] = x

    h = h_ref[...]
    a = _dot(h, wg_ref[...])
    u = _dot(h, wu_ref[...])
    act = (a * jax.nn.sigmoid(a) * u).astype(BF16)
    o_ref[...] += _dot(act, wd_ref[...])

    if final_norm:
        @pl.when(pl.program_id(1) == pl.num_programs(1) - 1)
        def _():
            y = o_ref[...]
            r = lax.rsqrt(jnp.mean(y * y, axis=-1, keepdims=True) + EPS)
            o_ref[...] = y * r * fin_ref[...]


def out_proj_ffn(x, y1, y2, wo, g, wg, wu, wd, fin, *, final_norm, tm=512, tf=512):
    T, D = x.shape
    K1, K2 = y1.shape[1], y2.shape[1]
    FF = wg.shape[1]
    tm = min(tm, T)
    assert FF % tf == 0 and K1 % K2 == 0
    once = pl.Buffered(1)
    return pl.pallas_call(
        functools.partial(_ffn_kernel, final_norm=final_norm),
        out_shape=jax.ShapeDtypeStruct((T, D), F32),
        grid=(T // tm, FF // tf),
        in_specs=[pl.BlockSpec((tm, D), lambda i, f: (i, 0)),
                  pl.BlockSpec((tm, K1), lambda i, f: (i, 0)),
                  pl.BlockSpec((tm, K2), lambda i, f: (i, 0)),
                  pl.BlockSpec((K1, D), lambda i, f: (0, 0), pipeline_mode=once),
                  pl.BlockSpec((K2, D), lambda i, f: (K1 // K2, 0), pipeline_mode=once),
                  pl.BlockSpec((1, D), lambda i, f: (0, 0)),
                  pl.BlockSpec((D, tf), lambda i, f: (0, f)),
                  pl.BlockSpec((D, tf), lambda i, f: (0, f)),
                  pl.BlockSpec((tf, D), lambda i, f: (f, 0)),
                  pl.BlockSpec((1, D), lambda i, f: (0, 0))],
        out_specs=pl.BlockSpec((tm, D), lambda i, f: (i, 0)),
        scratch_shapes=[pltpu.VMEM((tm, D), BF16)],
        compiler_params=_params("parallel", "arbitrary"),
        name="out_proj_ffn",
    )(x, y1, y2, wo, wo, g.reshape(1, D), wg, wu, wd, fin.reshape(1, D))


def _mem_attn_kernel(q_ref, k_ref, v_ref, o_ref):
    scale = HEAD_DIM ** -0.5
    sls = [slice(h * HEAD_DIM, (h + 1) * HEAD_DIM) for h in range(MEM_HEADS)]
    ss = [_dot_nt(q_ref[:, sl], k_ref[:, sl]) * scale for sl in sls]
    es = [jnp.exp(s - jnp.max(s, axis=-1, keepdims=True)) for s in ss]
    ps = [e / jnp.sum(e, axis=-1, keepdims=True) for e in es]
    for sl, p in zip(sls, ps):
        o_ref[:, sl] = _dot(p.astype(BF16), v_ref[:, sl]).astype(o_ref.dtype)


def mem_attention(u3, qblk, mem_kv, *, tq=512):
    B, S, _ = u3.shape
    M = mem_kv.shape[1]
    tq = min(tq, S)
    return pl.pallas_call(
        _mem_attn_kernel,
        out_shape=jax.ShapeDtypeStruct((B, S, MEM_WIDTH), BF16),
        grid=(B, S // tq),
        in_specs=[pl.BlockSpec((None, tq, MEM_WIDTH), lambda b, i: (b, i, qblk)),
                  pl.BlockSpec((None, M, MEM_WIDTH), lambda b, i: (b, 0, 0)),
                  pl.BlockSpec((None, M, MEM_WIDTH), lambda b, i: (b, 0, 1))],
        out_specs=pl.BlockSpec((None, tq, MEM_WIDTH), lambda b, i: (b, i, 0)),
        compiler_params=_params("parallel", "parallel"),
        name="mem_attention",
    )(u3, mem_kv, mem_kv)


LOG2E = 1.4426950408889634


def _stick_kernel(q_ref, k_ref, v_ref, o_ref, acc_ref, tail_ref, z0_ref, z1_ref, *, tq, tk):
    i = pl.program_id(2)
    z_refs = (z0_ref, z1_ref)
    c2 = HEAD_DIM ** -0.5 * LOG2E
    per_q = tq // tk
    nh = q_ref.shape[1] // HEAD_DIM
    lrow = lax.broadcasted_iota(jnp.int32, (2 * tk, tk), 0)
    lcol = lax.broadcasted_iota(jnp.int32, (2 * tk, tk), 1)
    later = ((lrow > lcol) & ((lrow < tk) | (lrow > lcol + tk))).astype(BF16)
    rows = lax.broadcasted_iota(jnp.int32, (tq, tk), 0)
    cols = lax.broadcasted_iota(jnp.int32, (tq, tk), 1)
    acc_ref[...] = jnp.zeros_like(acc_ref)
    tail_ref[...] = jnp.zeros_like(tail_ref)

    def scores(j, slot):
        off = pl.multiple_of(j * tk, tk)
        for hh in range(nh):
            sl = slice(hh * HEAD_DIM, (hh + 1) * HEAD_DIM)
            z_refs[slot][hh] = _dot_nt(q_ref[:, sl], k_ref[pl.ds(off, tk), sl])

    def step(j, slot, diag):
        off = pl.multiple_of(j * tk, tk)
        r0 = 0 if diag is None else diag * tk
        if diag is not None:
            causal = (cols + diag * tk < rows)[r0:]
        scores(jnp.maximum(j - 1, 0), 1 - slot)
        part = []
        for hh in range(nh):
            ns = z_refs[slot][hh, r0:, :] * (-c2)
            log_keep = jnp.minimum(ns, 0.0) - jnp.log2(1.0 + jnp.exp2(-jnp.abs(ns)))
            if diag is not None:
                log_keep = jnp.where(causal, log_keep, 0.0)
            hi, lo = _split2(log_keep)
            after = _dot(jnp.concatenate([hi, lo], axis=1), later) + tail_ref[hh, r0:, :]
            part.append((ns, log_keep, after))
        for hh in range(nh):
            sl = slice(hh * HEAD_DIM, (hh + 1) * HEAD_DIM)
            ns, log_keep, after = part[hh]
            a = jnp.exp2(log_keep + after - ns)
            if diag is not None:
                a = jnp.where(causal, a, 0.0)
            acc_ref[hh, r0:, :] += _dot(a.astype(BF16), v_ref[pl.ds(off, tk), sl])
            tail_ref[hh, r0:, :] = after[:, 0:1] + log_keep[:, 0:1]

    assert per_q % 2 == 0
    scores((i + 1) * per_q - 1, 1)
    for d in reversed(range(per_q)):
        step(i * per_q + d, d % 2, d)

    @pl.loop(0, i * (per_q // 2))
    def _(jj):
        step(i * per_q - 1 - 2 * jj, 1, None)
        step(i * per_q - 2 - 2 * jj, 0, None)

    for hh in range(nh):
        o_ref[:, hh * HEAD_DIM:(hh + 1) * HEAD_DIM] = acc_ref[hh].astype(o_ref.dtype)


def stick_breaking(u3, *, tq=512, tk=256, nh=3):
    B, S, _ = u3.shape
    H = B_HEADS // nh
    tq = min(tq, S)
    tk = min(tk, tq)
    hw = nh * HEAD_DIM
    return pl.pallas_call(
        functools.partial(_stick_kernel, tq=tq, tk=tk),
        out_shape=jax.ShapeDtypeStruct((B, S, MIX_WIDTH), BF16),
        grid=(B, H, S // tq),
        in_specs=[pl.BlockSpec((None, tq, hw), lambda b, h, i: (b, i, h)),
                  pl.BlockSpec((None, S, hw), lambda b, h, i: (b, 0, H + h)),
                  pl.BlockSpec((None, S, hw), lambda b, h, i: (b, 0, 2 * H + h))],
        out_specs=pl.BlockSpec((None, tq, hw), lambda b, h, i: (b, i, h)),
        scratch_shapes=[pltpu.VMEM((nh, tq, HEAD_DIM), F32), pltpu.VMEM((nh, tq, 1), F32),
                        pltpu.VMEM((nh, tq, tk), F32), pltpu.VMEM((nh, tq, tk), F32)],
        compiler_params=_params("parallel", "parallel", "arbitrary"),
        name="stick_breaking",
    )(u3, u3, u3)


def _swa_kernel(sink_ref, q_ref, kvp_ref, kvc_ref, o_ref, *, tq):
    g = pl.program_id(1)
    i = pl.program_id(2)
    dh = D_HEAD_DIM
    scale = dh ** -0.5
    kv = jnp.concatenate([kvp_ref[...], kvc_ref[...]], axis=0)
    k = kv[:, :dh]
    v = kv[:, dh:]
    r = lax.broadcasted_iota(jnp.int32, (tq, tq), 0)
    c = lax.broadcasted_iota(jnp.int32, (tq, tq), 1)
    from_prev = c > r
    visible = (c <= r) | (i > 0)
    heads = range(D_HPG)
    sks = [sink_ref[g * D_HPG + hh] for hh in heads]
    raw = [_dot_nt(q_ref[:, hh * dh:(hh + 1) * dh], k) for hh in heads]
    ss = [jnp.where(visible, jnp.where(from_prev, s[:, :tq], s[:, tq:]) * scale, NEG_INF) for s in raw]
    ms = [jnp.maximum(jnp.max(ss[hh], axis=-1, keepdims=True), sks[hh]) for hh in heads]
    es = [jnp.exp(ss[hh] - ms[hh]) for hh in heads]
    ps = [es[hh] / (jnp.sum(es[hh], axis=-1, keepdims=True) + jnp.exp(sks[hh] - ms[hh])) for hh in heads]
    for hh in heads:
        p2 = jnp.concatenate([jnp.where(from_prev, ps[hh], 0.0), jnp.where(from_prev, 0.0, ps[hh])], axis=1)
        o_ref[:, hh * dh:(hh + 1) * dh] = _dot(p2.astype(BF16), v).astype(o_ref.dtype)


def sink_window(u3, sinks):
    B, S, _ = u3.shape
    tq = D_WINDOW
    gw = D_HPG * D_HEAD_DIM
    kv0 = (MIX_WIDTH + MEM_WIDTH) // (2 * D_HEAD_DIM)
    return pl.pallas_call(
        functools.partial(_swa_kernel, tq=tq),
        out_shape=jax.ShapeDtypeStruct((B, S, MIX_WIDTH), BF16),
        grid=(B, D_KV, S // tq),
        in_specs=[pl.BlockSpec(memory_space=pltpu.SMEM),
                  pl.BlockSpec((None, tq, gw), lambda b, g, i: (b, i, g)),
                  pl.BlockSpec((None, tq, 2 * D_HEAD_DIM),
                               lambda b, g, i: (b, jnp.maximum(i - 1, 0), kv0 + g)),
                  pl.BlockSpec((None, tq, 2 * D_HEAD_DIM), lambda b, g, i: (b, i, kv0 + g))],
        out_specs=pl.BlockSpec((None, tq, gw), lambda b, g, i: (b, i, g)),
        compiler_params=_params("parallel", "parallel", "arbitrary"),
        name="sink_window",
    )(sinks, u3, u3, u3)


CONV_HALO = 8
CONV_CHUNK = 512


def _proj_conv_kernel(x_ref, g_ref, w_ref, cw_ref, cb_ref, cs_ref, qk_ref, u_ref, um_ref,
                      halo_ref, buf_ref, *, tiles_per_seq, n_conv, n_f32):
    tm = x_ref.shape[0]
    hl = CONV_HALO
    x = x_ref[...]
    r = lax.rsqrt(jnp.mean(x * x, axis=-1, keepdims=True) + EPS)
    h = (x * r * g_ref[...]).astype(BF16)
    seq_start = pl.program_id(0) % tiles_per_seq == 0

    @pl.when(pl.program_id(0) == 0)
    def _():
        halo_ref[...] = jnp.zeros_like(halo_ref)

    for c0 in range(0, n_conv, CONV_CHUNK):
        sl = slice(c0, c0 + CONV_CHUNK)
        y = _dot(h, w_ref[:, sl])
        prev = halo_ref[:, sl]
        buf_ref[0:hl, :] = jnp.where(seq_start, jnp.zeros_like(prev), prev)
        buf_ref[hl:hl + tm, :] = y
        halo_ref[:, sl] = y[tm - hl:tm, :]
        acc = cb_ref[:, sl] + cw_ref[C_CONV - 1:C_CONV, sl] * y
        for j in range(C_CONV - 1):
            sh = C_CONV - 1 - j
            acc = acc + cw_ref[j:j + 1, sl] * buf_ref[hl - sh:hl - sh + tm, :]
        qk_ref[:, sl] = (acc * jax.nn.sigmoid(acc) * cs_ref[:, sl]).astype(qk_ref.dtype)
    for c0 in range(0, n_f32, PROJ_CHUNK):
        cw = min(PROJ_CHUNK, n_f32 - c0)
        u_ref[:, c0:c0 + cw] = _dot(h, w_ref[:, n_conv + c0:n_conv + c0 + cw])
    um_ref[...] = _dot(h, w_ref[:, n_conv + n_f32:]).astype(um_ref.dtype)


def proj_conv(x, g, w, conv_w, conv_b, col_scale, S, n_f32, *, tm=256):
    T, D = x.shape
    N = w.shape[1]
    n_conv = conv_w.shape[1]
    tm = min(tm, S)
    assert S % tm == 0 and n_conv % CONV_CHUNK == 0 and tm >= CONV_HALO
    n_mem = N - n_conv - n_f32
    row = lambda i: (i, 0)
    fixed = lambda i: (0, 0)
    return pl.pallas_call(
        functools.partial(_proj_conv_kernel, tiles_per_seq=S // tm, n_conv=n_conv, n_f32=n_f32),
        out_shape=(jax.ShapeDtypeStruct((T, n_conv), BF16), jax.ShapeDtypeStruct((T, n_f32), F32),
                   jax.ShapeDtypeStruct((T, n_mem), BF16)),
        grid=(T // tm,),
        in_specs=[pl.BlockSpec((tm, D), row),
                  pl.BlockSpec((1, D), fixed),
                  pl.BlockSpec((D, N), fixed, pipeline_mode=pl.Buffered(1)),
                  pl.BlockSpec((C_CONV, n_conv), fixed),
                  pl.BlockSpec((1, n_conv), fixed),
                  pl.BlockSpec((1, n_conv), fixed)],
        out_specs=(pl.BlockSpec((tm, n_conv), row), pl.BlockSpec((tm, n_f32), row),
                   pl.BlockSpec((tm, n_mem), row)),
        scratch_shapes=[pltpu.VMEM((CONV_HALO, n_conv), F32), pltpu.VMEM((tm + CONV_HALO, CONV_CHUNK), F32)],
        compiler_params=_params("arbitrary"),
        name="proj_conv",
    )(x, g.reshape(1, D), w, conv_w, conv_b.reshape(1, n_conv), col_scale.reshape(1, n_conv))


def _mlstm_kernel(q_ref, k_ref, v_ref, o_ref, gt_ref, gb_ref, hn_ref, out_ref,
                  state_ref, m_ref, *, L):
    H, dh = C_HEADS, C_HEAD_DIM
    ext = dh + LANES

    @pl.when(pl.program_id(1) == 0)
    def _():
        state_ref[...] = jnp.zeros_like(state_ref)
        m_ref[...] = jnp.zeros_like(m_ref)

    rows = lax.broadcasted_iota(jnp.int32, (L, L), 0)
    cols = lax.broadcasted_iota(jnp.int32, (L, L), 1)
    tril = rows >= cols
    eye = rows == cols
    gates = gt_ref[...] + gb_ref[...]
    log_f = jnp.minimum(gates, 0.0) - jnp.log1p(jnp.exp(-jnp.abs(gates)))
    f_hi, f_mid, f_lo = _split3(log_f)
    trilb = tril.astype(BF16)
    bsum = _dot(trilb, f_hi) + _dot(trilb, f_mid) + _dot(trilb, f_lo)
    ones_col = (lax.broadcasted_iota(jnp.int32, (L, LANES), 1) == 0).astype(F32)

    for h in range(H):
        sl = slice(h * dh, (h + 1) * dh)
        q = q_ref[:, sl]
        k = k_ref[:, sl]
        v_ext = jnp.concatenate([v_ref[:, sl], ones_col], axis=1)
        ic = gates[:, h:h + 1]
        bc = bsum[:, H + h:H + h + 1]
        m_prev = m_ref[h][0:1, 0:1]
        rc = ic - bc
        rrow = jnp.sum(jnp.where(eye, rc, 0.0), axis=0, keepdims=True)
        dm = jnp.where(tril, bc + rrow, NEG_INF)
        m_inter = bc + m_prev
        m_t = jnp.maximum(m_inter, jnp.max(dm, axis=-1, keepdims=True))
        w = jnp.exp(dm - m_t)
        a_inter = jnp.exp(m_inter - m_t)
        sqk = _dot_nt(q, k) * w
        state = state_ref[h]
        num = a_inter * _dot(q, state.astype(BF16)) + _dot(sqk.astype(BF16), v_ext.astype(BF16))
        den = jnp.maximum(jnp.abs(num[:, dh:dh + 1]), jnp.exp(-m_t))
        hid = num[:, :dh] / den
        hid = jax.nn.sigmoid(o_ref[:, sl]) * hid
        hid = hid * lax.rsqrt(jnp.mean(hid * hid, axis=-1, keepdims=True) + EPS)
        out_ref[:, sl] = (hid * hn_ref[:, sl]).astype(out_ref.dtype)

        b_last = bc[L - 1:L, :]
        gk = b_last - bc + ic
        m_new = jnp.maximum(b_last + m_prev, jnp.max(gk, axis=0, keepdims=True))
        wk = jnp.exp(gk - m_new)
        decay = jnp.exp(b_last + m_prev - m_new)
        state_ref[h] = decay * state + _dot_tn(k, (wk * v_ext).astype(BF16))
        m_ref[h] = jnp.broadcast_to(m_new, m_ref.shape[1:])


def mlstm(qk3, u3, gate_b, head_norm, *, L=C_CHUNK):
    B, S, _ = u3.shape
    W = MIX_WIDTH
    H, dh = C_HEADS, C_HEAD_DIM
    L = min(L, S)
    gblk = 2 * W // LANES
    gb = jnp.zeros((1, LANES), F32).at[0, :2 * H].set(gate_b)
    return pl.pallas_call(
        functools.partial(_mlstm_kernel, L=L),
        out_shape=jax.ShapeDtypeStruct((B, S, W), BF16),
        grid=(B, S // L),
        in_specs=[pl.BlockSpec((None, L, W), lambda b, c: (b, c, 0)),
                  pl.BlockSpec((None, L, W), lambda b, c: (b, c, 1)),
                  pl.BlockSpec((None, L, W), lambda b, c: (b, c, 0)),
                  pl.BlockSpec((None, L, W), lambda b, c: (b, c, 1)),
                  pl.BlockSpec((None, L, LANES), lambda b, c: (b, c, gblk)),
                  pl.BlockSpec((1, LANES), lambda b, c: (0, 0)),
                  pl.BlockSpec((1, W), lambda b, c: (0, 0))],
        out_specs=pl.BlockSpec((None, L, W), lambda b, c: (b, c, 0)),
        scratch_shapes=[pltpu.VMEM((H, dh, dh + LANES), F32),
                        pltpu.VMEM((H, 8, LANES), F32)],
        compiler_params=_params("parallel", "arbitrary"),
        name="mlstm",
    )(qk3, qk3, u3, u3, u3, gb, head_norm.reshape(1, W))


def _compress_kernel(x_ref, pe_ref, w1_ref, w2_ref, o_ref):
    half = w1_ref.shape[0] // 2
    x = x_ref[...]
    n = x.shape[0]
    first = _dot(x, w1_ref[0:half, :])
    second = _dot(x, w1_ref[half:, :])
    pe = jnp.broadcast_to(pe_ref[...], (8, pe_ref.shape[1])).astype(BF16)
    bias = _dot(pe, w1_ref[...])[0:1, :]
    hid = first + pltpu.roll(second, n - 1, 0) + bias
    o_ref[...] = _dot(jax.nn.gelu(hid).astype(BF16), w2_ref[...]).astype(o_ref.dtype)


def compress(xc, pe, w1, w2):
    _, G, B, n, width = xc.shape
    hid = w1.shape[2]
    return pl.pallas_call(
        _compress_kernel,
        out_shape=jax.ShapeDtypeStruct((2, G, B, n, HEAD_DIM), BF16),
        grid=(2, G, B),
        in_specs=[pl.BlockSpec((None, None, None, n, width), lambda j, g, b: (j, g, b, 0, 0)),
                  pl.BlockSpec((None, 1, 2 * width), lambda j, g, b: (j, 0, 0)),
                  pl.BlockSpec((None, 2 * width, hid), lambda j, g, b: (j, 0, 0)),
                  pl.BlockSpec((None, hid, HEAD_DIM), lambda j, g, b: (j, 0, 0))],
        out_specs=pl.BlockSpec((None, None, None, n, HEAD_DIM), lambda j, g, b: (j, g, b, 0, 0)),
        compiler_params=_params("parallel", "parallel", "parallel"),
        name="nsa_compress",
    )(xc, pe, w1, w2)


def _cmp_select_kernel(q_ref, kc_ref, vc_ref, o_ref, sel_ref, *, tq, n_cmp, n_slc):
    i = pl.program_id(2)
    scale = HEAD_DIM ** -0.5
    npad = kc_ref.shape[0]
    kc = kc_ref[...]
    vc = vc_ref[...]
    pos = i * tq + lax.broadcasted_iota(jnp.int32, (tq, npad), 0)
    nidx = lax.broadcasted_iota(jnp.int32, (tq, npad), 1)
    cmask = (nidx * A_CMP_STRIDE + (A_CMP_LEN - 1) <= pos) & (nidx < n_cmp)
    heads = range(A_HPG)
    ss = [jnp.where(cmask, _dot_nt(q_ref[:, h * HEAD_DIM:(h + 1) * HEAD_DIM], kc) * scale, NEG_INF)
          for h in heads]
    ms = [jnp.max(s, axis=-1, keepdims=True) for s in ss]
    es = [jnp.where(cmask, jnp.exp(ss[h] - ms[h]), 0.0) for h in heads]
    ps = [es[h] / jnp.maximum(jnp.sum(es[h], axis=-1, keepdims=True), 1e-30) for h in heads]
    psum = ps[0]
    for h in heads:
        o_ref[:, h * HEAD_DIM:(h + 1) * HEAD_DIM] = _dot(ps[h].astype(BF16), vc)
        if h:
            psum = psum + ps[h]

    on = lax.broadcasted_iota(jnp.int32, (npad, LANES), 0) * A_CMP_STRIDE
    oj = lax.broadcasted_iota(jnp.int32, (npad, LANES), 1)
    overlap = ((on < (oj + 1) * A_SEL_LEN) & (on + A_CMP_LEN > oj * A_SEL_LEN)
               & (on < n_cmp * A_CMP_STRIDE) & (oj < n_slc)).astype(BF16)
    p_hi, p_lo = _split2(psum)
    imp = _dot(p_hi, overlap) + _dot(p_lo, overlap)

    sub = 8
    blk = lax.broadcasted_iota(jnp.int32, (LANES, tq), 0)
    cur = (i * tq + lax.broadcasted_iota(jnp.int32, (LANES, tq), 1)) // A_SEL_LEN
    valid = blk <= cur
    forced = (blk == 0) | (blk == cur) | (blk == cur - 1)
    score = jnp.where(valid, jnp.where(forced, A_FORCE, imp.T), A_EXCLUDE)
    row8 = lax.broadcasted_iota(jnp.int32, (sub, tq), 0)
    n_top = min(A_TOP_N, n_slc)
    chosen = []
    for v in range(0, n_slc, sub):
        mine = score[v:v + sub, :]
        rank = jnp.zeros((sub, tq), jnp.int32)
        for j in range(n_slc):
            other = score[j:j + 1, :]
            if v > j:
                ahead = other >= mine
            elif v + sub - 1 <= j:
                ahead = other > mine
            else:
                ahead = (other > mine) | ((other == mine) & (row8 + v > j))
            rank = rank + ahead.astype(jnp.int32)
        chosen.append(((rank < n_top) & valid[v:v + sub, :]).astype(F32))
    if n_slc < LANES:
        chosen.append(jnp.zeros((LANES - n_slc, tq), F32))
    sel_ref[...] = jnp.concatenate(chosen, axis=0).T.astype(sel_ref.dtype)


def cmp_select(u3, kvc, *, tq=128):
    B, S, _ = u3.shape
    G = A_KV
    npad = kvc.shape[3]
    n_cmp = (S - A_CMP_LEN) // A_CMP_STRIDE + 1
    n_slc = S // A_SEL_LEN
    tq = min(tq, S)
    gw = A_HPG * HEAD_DIM
    return pl.pallas_call(
        functools.partial(_cmp_select_kernel, tq=tq, n_cmp=n_cmp, n_slc=n_slc),
        out_shape=(jax.ShapeDtypeStruct((B, S, MIX_WIDTH), F32),
                   jax.ShapeDtypeStruct((B, G, S, LANES), BF16)),
        grid=(B, G, S // tq),
        in_specs=[pl.BlockSpec((None, tq, gw), lambda b, g, i: (b, i, g)),
                  pl.BlockSpec((None, None, None, npad, HEAD_DIM), lambda b, g, i: (0, g, b, 0, 0)),
                  pl.BlockSpec((None, None, None, npad, HEAD_DIM), lambda b, g, i: (1, g, b, 0, 0))],
        out_specs=(pl.BlockSpec((None, tq, gw), lambda b, g, i: (b, i, g)),
                   pl.BlockSpec((None, None, tq, LANES), lambda b, g, i: (b, g, i, 0))),
        compiler_params=_params("parallel", "parallel", "parallel"),
        name="nsa_cmp_select",
    )(u3, kvc, kvc)


def _masked_scores(qs, k, mask, scale):
    hp = qs.shape[0] // mask.shape[0]
    s = _dot_nt(qs, k) * scale
    return jnp.where(mask[None], s.reshape((hp,) + mask.shape), NEG_INF).reshape(s.shape)


def _slc_win_kernel(q_ref, ks_ref, vs_ref, kw_ref, vw_ref, sel_ref, oc_ref, gt_ref, o_ref, *, tq, tk, wk):
    i = pl.program_id(2)
    S = ks_ref.shape[0]
    scale = HEAD_DIM ** -0.5 * LOG2E
    hp = A_HPG
    qs = jnp.concatenate([q_ref[:, h * HEAD_DIM:(h + 1) * HEAD_DIM] for h in range(hp)], axis=0)
    sel = sel_ref[...]
    qpos = i * tq + lax.broadcasted_iota(jnp.int32, (tq, tk), 0)
    c = lax.broadcasted_iota(jnp.int32, (tq, tk), 1)
    eb = lax.broadcasted_iota(jnp.int32, (LANES, tk), 0)
    ec = lax.broadcasted_iota(jnp.int32, (LANES, tk), 1) // A_SEL_LEN

    def slc_body(j, carry):
        m, l, acc = carry
        off = pl.multiple_of(j * tk, tk)
        expand = (eb == j * (tk // A_SEL_LEN) + ec).astype(BF16)
        mask = (_dot(sel, expand) > 0.5) & (off + c <= qpos)
        s = _masked_scores(qs, ks_ref[pl.ds(off, tk), :], mask, scale)
        m_new = jnp.maximum(m, jnp.max(s, axis=-1, keepdims=True))
        p = jnp.exp2(s - m_new)
        alpha = jnp.exp2(m - m_new)
        l = alpha * l + jnp.sum(p, axis=-1, keepdims=True)
        acc = alpha * acc + _dot(p.astype(BF16), vs_ref[pl.ds(off, tk), :])
        return m_new, l, acc

    init = (jnp.full((hp * tq, 1), NEG_INF, F32), jnp.zeros((hp * tq, 1), F32),
            jnp.zeros((hp * tq, HEAD_DIM), F32))
    _, l_s, acc_s = lax.fori_loop(0, (i * tq + tq + tk - 1) // tk, slc_body, init)
    o_slc = acc_s / l_s

    start = pl.multiple_of(jnp.clip(i * tq + tq - wk, 0, S - wk), tq)
    wq = i * tq + lax.broadcasted_iota(jnp.int32, (tq, wk), 0)
    rel = wq - (start + lax.broadcasted_iota(jnp.int32, (tq, wk), 1))
    s = _masked_scores(qs, kw_ref[pl.ds(start, wk), :], (rel >= 0) & (rel < A_WINDOW), scale)
    p = jnp.exp2(s - jnp.max(s, axis=-1, keepdims=True))
    o_win = _dot(p.astype(BF16), vw_ref[pl.ds(start, wk), :]) / jnp.sum(p, axis=-1, keepdims=True)

    gate = jax.nn.sigmoid(gt_ref[...])
    for h in range(hp):
        sl = slice(h * HEAD_DIM, (h + 1) * HEAD_DIM)
        rs = slice(h * tq, (h + 1) * tq)
        o = (gate[:, h:h + 1] * oc_ref[:, sl]
             + gate[:, hp + h:hp + h + 1] * o_slc[rs]
             + gate[:, 2 * hp + h:2 * hp + h + 1] * o_win[rs])
        o_ref[:, sl] = o.astype(o_ref.dtype)


def slc_win(u3, sel, o_cmp, gate3, *, tq=128, tk=512):
    B, S, _ = u3.shape
    G = A_KV
    tq = min(tq, S)
    tk = min(tk, S)
    wk = min(A_WINDOW + tq, S)
    assert S % tk == 0 and tk % tq == 0 and S // A_SEL_LEN <= LANES
    gw = A_HPG * HEAD_DIM
    kv0 = A_HEADS

    def kv_spec(which):
        return pl.BlockSpec((None, S, HEAD_DIM), lambda b, g, i: (b, 0, kv0 + which * G + g))

    return pl.pallas_call(
        functools.partial(_slc_win_kernel, tq=tq, tk=tk, wk=wk),
        out_shape=jax.ShapeDtypeStruct((B, S, MIX_WIDTH), BF16),
        grid=(B, G, S // tq),
        in_specs=[pl.BlockSpec((None, tq, gw), lambda b, g, i: (b, i, g)),
                  kv_spec(0), kv_spec(1), kv_spec(2), kv_spec(3),
                  pl.BlockSpec((None, None, tq, LANES), lambda b, g, i: (b, g, i, 0)),
                  pl.BlockSpec((None, tq, gw), lambda b, g, i: (b, i, g)),
                  pl.BlockSpec((None, tq, LANES), lambda b, g, i: (b, i, g))],
        out_specs=pl.BlockSpec((None, tq, gw), lambda b, g, i: (b, i, g)),
        compiler_params=_params("parallel", "parallel", "arbitrary"),
        name="nsa_slc_win",
    )(u3, u3, u3, u3, u3, sel, o_cmp, gate3)


def _pad_cols(w, n):
    return jnp.pad(w, ((0, 0), (0, n - w.shape[1])))


def nsa_layer_mix(x, B, S, norm, w_in, pe_k, w1_k, w2_k, pe_v, w1_v, w2_v, mem_kv):
    H, G, dh = A_HEADS, A_KV, HEAD_DIM
    qw = H * dh
    kvw = 6 * G * dh
    gw = 3 * H
    wg = w_in[:, qw + kvw:qw + kvw + gw].reshape(-1, G, A_HPG, 3).transpose(0, 1, 3, 2)
    wg = jnp.pad(wg.reshape(-1, G, 3 * A_HPG), ((0, 0), (0, 0), (0, LANES - 3 * A_HPG)))
    n_main = qw + 4 * G * dh + MEM_WIDTH
    n_cmp = 2 * G * dh
    w_all = jnp.concatenate([w_in[:, :qw], w_in[:, qw + n_cmp:qw + kvw], w_in[:, qw + kvw + gw:],
                             w_in[:, qw:qw + n_cmp], wg.reshape(-1, G * LANES)], axis=1).astype(BF16)

    u, xc, gate3 = rms_proj(x, norm, w_all, [(0, n_main, BF16, None), (n_main, n_cmp, BF16, dh),
                                             (n_main + n_cmp, G * LANES, F32, None)])
    u = u.reshape(B, S, -1)
    gate3 = gate3.reshape(B, S, -1)
    xc = xc.reshape(2, G, B, S // A_CMP_STRIDE, A_CMP_STRIDE * dh)
    pe = jnp.stack([pe_k.reshape(1, -1), pe_v.reshape(1, -1)])
    kvc = compress(xc, pe, jnp.stack([w1_k, w1_v]).astype(BF16), jnp.stack([w2_k, w2_v]).astype(BF16))
    o_cmp, sel = cmp_select(u, kvc)
    mixed = slc_win(u, sel, o_cmp, gate3)
    mem_out = mem_attention(u, (qw + 4 * G * dh) // MEM_WIDTH, mem_kv)
    return mixed, mem_out


def stick_layer_mix(x, B, S, norm, w_in, mem_kv):
    (u,) = rms_proj(x, norm, w_in.astype(BF16), [(0, w_in.shape[1], BF16, None)])
    u = u.reshape(B, S, -1)
    mixed = stick_breaking(u)
    mem_out = mem_attention(u, 3 * MIX_WIDTH // MEM_WIDTH, mem_kv)
    return mixed, mem_out


def mlstm_layer_mix(x, B, S, norm, w_in, conv_w, conv_b, gate_b, head_norm, mem_kv):
    W = MIX_WIDTH
    nf = 4 * W + 2 * C_HEADS
    n_f = 4 * W + LANES
    w_all = jnp.concatenate([_pad_cols(w_in[:, :nf], n_f), w_in[:, nf:]], axis=1).astype(BF16)
    col_scale = jnp.concatenate([jnp.ones((W,), F32), jnp.full((W,), C_HEAD_DIM ** -0.5, F32)])
    qk, u, um = proj_conv(x, norm, w_all, conv_w, conv_b, col_scale, S, n_f - 2 * W)
    qk = qk.reshape(B, S, -1)
    u = u.reshape(B, S, -1)
    um = um.reshape(B, S, -1)
    mixed = mlstm(qk, u, gate_b, head_norm)
    mem_out = mem_attention(um, 0, mem_kv)
    return mixed, mem_out


def swa_layer_mix(x, B, S, norm, w_in, sinks, mem_kv):
    W = MIX_WIDTH
    kw = D_KV * D_HEAD_DIM
    wk = w_in[:, W:W + kw].reshape(-1, D_KV, D_HEAD_DIM)
    wv = w_in[:, W + kw:W + 2 * kw].reshape(-1, D_KV, D_HEAD_DIM)
    wkv = jnp.concatenate([wk, wv], axis=2).reshape(-1, 2 * kw)
    w_main = jnp.concatenate([w_in[:, :W], w_in[:, W + 2 * kw:], wkv], axis=1)
    w_main = _pad_cols(w_main, W + MEM_WIDTH + 4 * LANES).astype(BF16)
    (u,) = rms_proj(x, norm, w_main, [(0, w_main.shape[1], BF16, None)])
    u = u.reshape(B, S, -1)
    mixed = sink_window(u, sinks)
    mem_out = mem_attention(u, W // MEM_WIDTH, mem_kv)
    return mixed, mem_out


def kernel(x, mem, mem_norm, mem_w_kv, l0_norm_mix, l0_w_in, l0_cmp_pe_k, l0_cmp_w1_k, l0_cmp_w2_k, l0_cmp_pe_v, l0_cmp_w1_v, l0_cmp_w2_v, l0_w_out, l0_norm_ffn, l0_w_gate, l0_w_up, l0_w_down, l1_norm_mix, l1_w_in, l1_w_out, l1_norm_ffn, l1_w_gate, l1_w_up, l1_w_down, l2_norm_mix, l2_w_in, l2_conv_w, l2_conv_b, l2_gate_b, l2_head_norm, l2_w_out, l2_norm_ffn, l2_w_gate, l2_w_up, l2_w_down, l3_norm_mix, l3_w_in, l3_sinks, l3_w_out, l3_norm_ffn, l3_w_gate, l3_w_up, l3_w_down, final_norm):
    B, S, D = x.shape
    M = mem.shape[1]
    (mem_kv,) = rms_proj(mem.reshape(B * M, D), mem_norm, mem_w_kv.astype(BF16),
                         [(0, mem_w_kv.shape[1], BF16, None)])
    mem_kv = mem_kv.reshape(B, M, -1)
    xs = x.reshape(B * S, D)

    def finish(xs, mixed, mem_out, w_out, norm_ffn, w_gate, w_up, w_down, last=False):
        return out_proj_ffn(xs, mixed.reshape(B * S, -1), mem_out.reshape(B * S, -1),
                            w_out.astype(BF16), norm_ffn,
                            w_gate.astype(BF16), w_up.astype(BF16), w_down.astype(BF16),
                            final_norm, final_norm=last)

    mixed, mem_out = nsa_layer_mix(xs, B, S, l0_norm_mix, l0_w_in, l0_cmp_pe_k, l0_cmp_w1_k, l0_cmp_w2_k,
                                   l0_cmp_pe_v, l0_cmp_w1_v, l0_cmp_w2_v, mem_kv)
    xs = finish(xs, mixed, mem_out, l0_w_out, l0_norm_ffn, l0_w_gate, l0_w_up, l0_w_down)
    mixed, mem_out = stick_layer_mix(xs, B, S, l1_norm_mix, l1_w_in, mem_kv)
    xs = finish(xs, mixed, mem_out, l1_w_out, l1_norm_ffn, l1_w_gate, l1_w_up, l1_w_down)
    mixed, mem_out = mlstm_layer_mix(xs, B, S, l2_norm_mix, l2_w_in, l2_conv_w, l2_conv_b, l2_gate_b,
                                     l2_head_norm, mem_kv)
    xs = finish(xs, mixed, mem_out, l2_w_out, l2_norm_ffn, l2_w_gate, l2_w_up, l2_w_down)
    mixed, mem_out = swa_layer_mix(xs, B, S, l3_norm_mix, l3_w_in, l3_sinks, mem_kv)
    xs = finish(xs, mixed, mem_out, l3_w_out, l3_norm_ffn, l3_w_gate, l3_w_up, l3_w_down, last=True)
    return xs.reshape(B, S, D)
```

```python
import functools

import jax
import jax.numpy as jnp
from jax import lax
from jax.experimental import pallas as pl
from jax.experimental.pallas import tpu as pltpu

F32 = jnp.float32
BF16 = jnp.bfloat16

LANES = 128
VMEM_LIMIT = 56 * 1024 * 1024

D_MODEL = 2048
HEAD_DIM = 128
MEM_HEADS = 4
MEM_WIDTH = MEM_HEADS * HEAD_DIM
MIX_WIDTH = D_MODEL - MEM_WIDTH
EPS = 1e-6
NEG_INF = -1e30

A_HEADS = MIX_WIDTH // HEAD_DIM
A_KV = 2
A_HPG = A_HEADS // A_KV
A_CMP_LEN = 32
A_CMP_STRIDE = 16
A_SEL_LEN = 64
A_TOP_N = 16
A_WINDOW = 512
A_FORCE = 1e6
A_EXCLUDE = -1e9

B_HEADS = MIX_WIDTH // HEAD_DIM

C_HEADS = 4
C_HEAD_DIM = MIX_WIDTH // C_HEADS
C_CONV = 4
C_CHUNK = 256

D_HEAD_DIM = 64
D_HEADS = MIX_WIDTH // D_HEAD_DIM
D_KV = D_HEADS // 8
D_HPG = D_HEADS // D_KV
D_WINDOW = 128


def _params(*sem):
    return pltpu.CompilerParams(dimension_semantics=sem, vmem_limit_bytes=VMEM_LIMIT)


def _dot(a, b):
    return jnp.dot(a, b, preferred_element_type=F32)


def _dot_nt(a, b):
    return lax.dot_general(a, b, (((1,), (1,)), ((), ())), preferred_element_type=F32)


def _dot_tn(a, b):
    return lax.dot_general(a, b, (((0,), (0,)), ((), ())), preferred_element_type=F32)


def _split2(x):
    hi = x.astype(BF16)
    lo = (x - hi.astype(F32)).astype(BF16)
    return hi, lo


def _split3(x):
    hi = x.astype(BF16)
    r = x - hi.astype(F32)
    mid = r.astype(BF16)
    lo = (r - mid.astype(F32)).astype(BF16)
    return hi, mid, lo


PROJ_CHUNK = 1024


def _rms_proj_kernel(x_ref, g_ref, w_ref, *o_refs, outs):
    x = x_ref[...]
    r = lax.rsqrt(jnp.mean(x * x, axis=-1, keepdims=True) + EPS)
    h = (x * r * g_ref[...]).astype(BF16)
    for o_ref, (start, width, split) in zip(o_refs, outs):
        step = split if split else min(PROJ_CHUNK, width)
        for n, c0 in enumerate(range(0, width, step)):
            cw = min(step, width - c0)
            y = _dot(h, w_ref[:, start + c0:start + c0 + cw]).astype(o_ref.dtype)
            if split:
                o_ref[n] = y
            else:
                o_ref[:, c0:c0 + cw] = y


def rms_proj(x, g, w, outs, *, tm=512):
    T, D = x.shape
    N = w.shape[1]
    tm = min(tm, T)
    shapes, specs = [], []
    for start, width, dtype, split in outs:
        if split:
            shapes.append(jax.ShapeDtypeStruct((width // split, T, split), dtype))
            specs.append(pl.BlockSpec((width // split, tm, split), lambda i: (0, i, 0)))
        else:
            shapes.append(jax.ShapeDtypeStruct((T, width), dtype))
            specs.append(pl.BlockSpec((tm, width), lambda i: (i, 0)))
    return pl.pallas_call(
        functools.partial(_rms_proj_kernel, outs=tuple((s, wd, sp) for s, wd, _, sp in outs)),
        out_shape=tuple(shapes),
        grid=(T // tm,),
        in_specs=[pl.BlockSpec((tm, D), lambda i: (i, 0)),
                  pl.BlockSpec((1, D), lambda i: (0, 0)),
                  pl.BlockSpec((D, N), lambda i: (0, 0), pipeline_mode=pl.Buffered(1))],
        out_specs=tuple(specs),
        compiler_params=_params("parallel"),
        name="rms_proj",
    )(x, g.reshape(1, D), w)


def _ffn_kernel(x_ref, y1_ref, y2_ref, wo1_ref, wo2_ref, g_ref, wg_ref, wu_ref, wd_ref, fin_ref, o_ref, h_ref,
                *, final_norm):
    @pl.when(pl.program_id(1) == 0)
    def _():
        x = x_ref[...] + _dot(y1_ref[...], wo1_ref[...]) + _dot(y2_ref[...], wo2_ref[...])
        r = lax.rsqrt(jnp.mean(x * x, axis=-1, keepdims=True) + EPS)
        h_ref[...] = (x * r * g_ref[...]).astype(BF16)
        o_ref[...] = x

    h = h_ref[...]
    a = _dot(h, wg_ref[...])
    u = _dot(h, wu_ref[...])
    act = (a * jax.nn.sigmoid(a) * u).astype(BF16)
    o_ref[...] += _dot(act, wd_ref[...])

    if final_norm:
        @pl.when(pl.program_id(1) == pl.num_programs(1) - 1)
        def _():
            y = o_ref[...]
            r = lax.rsqrt(jnp.mean(y * y, axis=-1, keepdims=True) + EPS)
            o_ref[...] = y * r * fin_ref[...]


def out_proj_ffn(x, y1, y2, wo, g, wg, wu, wd, fin, *, final_norm, tm=512, tf=512):
    T, D = x.shape
    K1, K2 = y1.shape[1], y2.shape[1]
    FF = wg.shape[1]
    tm = min(tm, T)
    assert FF % tf == 0 and K1 % K2 == 0
    once = pl.Buffered(1)
    return pl.pallas_call(
        functools.partial(_ffn_kernel, final_norm=final_norm),
        out_shape=jax.ShapeDtypeStruct((T, D), F32),
        grid=(T // tm, FF // tf),
        in_specs=[pl.BlockSpec((tm, D), lambda i, f: (i, 0)),
                  pl.BlockSpec((tm, K1), lambda i, f: (i, 0)),
                  pl.BlockSpec((tm, K2), lambda i, f: (i, 0)),
                  pl.BlockSpec((K1, D), lambda i, f: (0, 0), pipeline_mode=once),
                  pl.BlockSpec((K2, D), lambda i, f: (K1 // K2, 0), pipeline_mode=once),
                  pl.BlockSpec((1, D), lambda i, f: (0, 0)),
                  pl.BlockSpec((D, tf), lambda i, f: (0, f)),
                  pl.BlockSpec((D, tf), lambda i, f: (0, f)),
                  pl.BlockSpec((tf, D), lambda i, f: (f, 0)),
                  pl.BlockSpec((1, D), lambda i, f: (0, 0))],
        out_specs=pl.BlockSpec((tm, D), lambda i, f: (i, 0)),
        scratch_shapes=[pltpu.VMEM((tm, D), BF16)],
        compiler_params=_params("parallel", "arbitrary"),
        name="out_proj_ffn",
    )(x, y1, y2, wo, wo, g.reshape(1, D), wg, wu, wd, fin.reshape(1, D))


def _mem_attn_kernel(q_ref, k_ref, v_ref, o_ref):
    scale = HEAD_DIM ** -0.5
    sls = [slice(h * HEAD_DIM, (h + 1) * HEAD_DIM) for h in range(MEM_HEADS)]
    ss = [_dot_nt(q_ref[:, sl], k_ref[:, sl]) * scale for sl in sls]
    es = [jnp.exp(s - jnp.max(s, axis=-1, keepdims=True)) for s in ss]
    ps = [e / jnp.sum(e, axis=-1, keepdims=True) for e in es]
    for sl, p in zip(sls, ps):
        o_ref[:, sl] = _dot(p.astype(BF16), v_ref[:, sl]).astype(o_ref.dtype)


def mem_attention(u3, qblk, mem_kv, *, tq=512):
    B, S, _ = u3.shape
    M = mem_kv.shape[1]
    tq = min(tq, S)
    return pl.pallas_call(
        _mem_attn_kernel,
        out_shape=jax.ShapeDtypeStruct((B, S, MEM_WIDTH), BF16),
        grid=(B, S // tq),
        in_specs=[pl.BlockSpec((None, tq, MEM_WIDTH), lambda b, i: (b, i, qblk)),
                  pl.BlockSpec((None, M, MEM_WIDTH), lambda b, i: (b, 0, 0)),
                  pl.BlockSpec((None, M, MEM_WIDTH), lambda b, i: (b, 0, 1))],
        out_specs=pl.BlockSpec((None, tq, MEM_WIDTH), lambda b, i: (b, i, 0)),
        compiler_params=_params("parallel", "parallel"),
        name="mem_attention",
    )(u3, mem_kv, mem_kv)


LOG2E = 1.4426950408889634
STICK_DEAD_LOG2 = -160.0


def _stick_kernel(q_ref, k_ref, v_ref, o_ref, acc_ref, tail_ref, z0_ref, z1_ref, *, tq, tk):
    i = pl.program_id(2)
    z_refs = (z0_ref, z1_ref)
    c2 = HEAD_DIM ** -0.5 * LOG2E
    per_q = tq // tk
    nh = q_ref.shape[1] // HEAD_DIM
    lrow = lax.broadcasted_iota(jnp.int32, (2 * tk, tk), 0)
    lcol = lax.broadcasted_iota(jnp.int32, (2 * tk, tk), 1)
    later = ((lrow > lcol) & ((lrow < tk) | (lrow > lcol + tk))).astype(BF16)
    rows = lax.broadcasted_iota(jnp.int32, (tq, tk), 0)
    cols = lax.broadcasted_iota(jnp.int32, (tq, tk), 1)
    acc_ref[...] = jnp.zeros_like(acc_ref)
    tail_ref[...] = jnp.zeros_like(tail_ref)

    def scores(j, slot):
        off = pl.multiple_of(j * tk, tk)
        for hh in range(nh):
            sl = slice(hh * HEAD_DIM, (hh + 1) * HEAD_DIM)
            z_refs[slot][hh] = _dot_nt(q_ref[:, sl], k_ref[pl.ds(off, tk), sl])

    def step(j, slot, diag):
        off = pl.multiple_of(j * tk, tk)
        r0 = 0 if diag is None else diag * tk
        if diag is not None:
            causal = (cols + diag * tk < rows)[r0:]
        scores(jnp.maximum(j - 1, 0), 1 - slot)
        part = []
        for hh in range(nh):
            ns = z_refs[slot][hh, r0:, :] * (-c2)
            log_keep = jnp.minimum(ns, 0.0) - jnp.log2(1.0 + jnp.exp2(-jnp.abs(ns)))
            if diag is not None:
                log_keep = jnp.where(causal, log_keep, 0.0)
            hi, lo = _split2(log_keep)
            after = _dot(jnp.concatenate([hi, lo], axis=1), later) + tail_ref[hh, r0:, :]
            part.append((ns, log_keep, after))
        for hh in range(nh):
            sl = slice(hh * HEAD_DIM, (hh + 1) * HEAD_DIM)
            ns, log_keep, after = part[hh]
            a = jnp.exp2(log_keep + after - ns)
            if diag is not None:
                a = jnp.where(causal, a, 0.0)
            acc_ref[hh, r0:, :] += _dot(a.astype(BF16), v_ref[pl.ds(off, tk), sl])
            tail_ref[hh, r0:, :] = after[:, 0:1] + log_keep[:, 0:1]

    assert per_q % 2 == 0
    scores((i + 1) * per_q - 1, 1)
    for d in reversed(range(per_q)):
        step(i * per_q + d, d % 2, d)

    def pair(state):
        jj, _ = state
        step(i * per_q - 1 - 2 * jj, 1, None)
        step(i * per_q - 2 - 2 * jj, 0, None)
        return jj + 1, jnp.max(tail_ref[...]) > STICK_DEAD_LOG2

    lax.while_loop(lambda s: (s[0] < i * (per_q // 2)) & s[1], pair,
                   (jnp.int32(0), jnp.max(tail_ref[...]) > STICK_DEAD_LOG2))

    for hh in range(nh):
        o_ref[:, hh * HEAD_DIM:(hh + 1) * HEAD_DIM] = acc_ref[hh].astype(o_ref.dtype)


def stick_breaking(u3, *, tq=512, tk=256, nh=3):
    B, S, _ = u3.shape
    H = B_HEADS // nh
    tq = min(tq, S)
    tk = min(tk, tq)
    hw = nh * HEAD_DIM
    return pl.pallas_call(
        functools.partial(_stick_kernel, tq=tq, tk=tk),
        out_shape=jax.ShapeDtypeStruct((B, S, MIX_WIDTH), BF16),
        grid=(B, H, S // tq),
        in_specs=[pl.BlockSpec((None, tq, hw), lambda b, h, i: (b, i, h)),
                  pl.BlockSpec((None, S, hw), lambda b, h, i: (b, 0, H + h)),
                  pl.BlockSpec((None, S, hw), lambda b, h, i: (b, 0, 2 * H + h))],
        out_specs=pl.BlockSpec((None, tq, hw), lambda b, h, i: (b, i, h)),
        scratch_shapes=[pltpu.VMEM((nh, tq, HEAD_DIM), F32), pltpu.VMEM((nh, tq, 1), F32),
                        pltpu.VMEM((nh, tq, tk), F32), pltpu.VMEM((nh, tq, tk), F32)],
        compiler_params=_params("parallel", "parallel", "arbitrary"),
        name="stick_breaking",
    )(u3, u3, u3)


def _swa_kernel(sink_ref, q_ref, kvp_ref, kvc_ref, o_ref, *, tq):
    g = pl.program_id(1)
    i = pl.program_id(2)
    dh = D_HEAD_DIM
    scale = dh ** -0.5
    kv = jnp.concatenate([kvp_ref[...], kvc_ref[...]], axis=0)
    k = kv[:, :dh]
    v = kv[:, dh:]
    r = lax.broadcasted_iota(jnp.int32, (tq, tq), 0)
    c = lax.broadcasted_iota(jnp.int32, (tq, tq), 1)
    from_prev = c > r
    visible = (c <= r) | (i > 0)
    heads = range(D_HPG)
    sks = [sink_ref[g * D_HPG + hh] for hh in heads]
    raw = [_dot_nt(q_ref[:, hh * dh:(hh + 1) * dh], k) for hh in heads]
    ss = [jnp.where(visible, jnp.where(from_prev, s[:, :tq], s[:, tq:]) * scale, NEG_INF) for s in raw]
    ms = [jnp.maximum(jnp.max(ss[hh], axis=-1, keepdims=True), sks[hh]) for hh in heads]
    es = [jnp.exp(ss[hh] - ms[hh]) for hh in heads]
    ps = [es[hh] / (jnp.sum(es[hh], axis=-1, keepdims=True) + jnp.exp(sks[hh] - ms[hh])) for hh in heads]
    for hh in heads:
        p2 = jnp.concatenate([jnp.where(from_prev, ps[hh], 0.0), jnp.where(from_prev, 0.0, ps[hh])], axis=1)
        o_ref[:, hh * dh:(hh + 1) * dh] = _dot(p2.astype(BF16), v).astype(o_ref.dtype)


def sink_window(u3, sinks):
    B, S, _ = u3.shape
    tq = D_WINDOW
    gw = D_HPG * D_HEAD_DIM
    kv0 = (MIX_WIDTH + MEM_WIDTH) // (2 * D_HEAD_DIM)
    return pl.pallas_call(
        functools.partial(_swa_kernel, tq=tq),
        out_shape=jax.ShapeDtypeStruct((B, S, MIX_WIDTH), BF16),
        grid=(B, D_KV, S // tq),
        in_specs=[pl.BlockSpec(memory_space=pltpu.SMEM),
                  pl.BlockSpec((None, tq, gw), lambda b, g, i: (b, i, g)),
                  pl.BlockSpec((None, tq, 2 * D_HEAD_DIM),
                               lambda b, g, i: (b, jnp.maximum(i - 1, 0), kv0 + g)),
                  pl.BlockSpec((None, tq, 2 * D_HEAD_DIM), lambda b, g, i: (b, i, kv0 + g))],
        out_specs=pl.BlockSpec((None, tq, gw), lambda b, g, i: (b, i, g)),
        compiler_params=_params("parallel", "parallel", "arbitrary"),
        name="sink_window",
    )(sinks, u3, u3, u3)


CONV_HALO = 8
CONV_CHUNK = 512


def _proj_conv_kernel(x_ref, g_ref, w_ref, cw_ref, cb_ref, cs_ref, qk_ref, u_ref, um_ref,
                      halo_ref, buf_ref, *, tiles_per_seq, n_conv, n_f32):
    tm = x_ref.shape[0]
    hl = CONV_HALO
    x = x_ref[...]
    r = lax.rsqrt(jnp.mean(x * x, axis=-1, keepdims=True) + EPS)
    h = (x * r * g_ref[...]).astype(BF16)
    seq_start = pl.program_id(0) % tiles_per_seq == 0

    @pl.when(pl.program_id(0) == 0)
    def _():
        halo_ref[...] = jnp.zeros_like(halo_ref)

    for c0 in range(0, n_conv, CONV_CHUNK):
        sl = slice(c0, c0 + CONV_CHUNK)
        y = _dot(h, w_ref[:, sl])
        prev = halo_ref[:, sl]
        buf_ref[0:hl, :] = jnp.where(seq_start, jnp.zeros_like(prev), prev)
        buf_ref[hl:hl + tm, :] = y
        halo_ref[:, sl] = y[tm - hl:tm, :]
        acc = cb_ref[:, sl] + cw_ref[C_CONV - 1:C_CONV, sl] * y
        for j in range(C_CONV - 1):
            sh = C_CONV - 1 - j
            acc = acc + cw_ref[j:j + 1, sl] * buf_ref[hl - sh:hl - sh + tm, :]
        qk_ref[:, sl] = (acc * jax.nn.sigmoid(acc) * cs_ref[:, sl]).astype(qk_ref.dtype)
    for c0 in range(0, n_f32, PROJ_CHUNK):
        cw = min(PROJ_CHUNK, n_f32 - c0)
        u_ref[:, c0:c0 + cw] = _dot(h, w_ref[:, n_conv + c0:n_conv + c0 + cw])
    um_ref[...] = _dot(h, w_ref[:, n_conv + n_f32:]).astype(um_ref.dtype)


def proj_conv(x, g, w, conv_w, conv_b, col_scale, S, n_f32, *, tm=256):
    T, D = x.shape
    N = w.shape[1]
    n_conv = conv_w.shape[1]
    tm = min(tm, S)
    assert S % tm == 0 and n_conv % CONV_CHUNK == 0 and tm >= CONV_HALO
    n_mem = N - n_conv - n_f32
    row = lambda i: (i, 0)
    fixed = lambda i: (0, 0)
    return pl.pallas_call(
        functools.partial(_proj_conv_kernel, tiles_per_seq=S // tm, n_conv=n_conv, n_f32=n_f32),
        out_shape=(jax.ShapeDtypeStruct((T, n_conv), BF16), jax.ShapeDtypeStruct((T, n_f32), F32),
                   jax.ShapeDtypeStruct((T, n_mem), BF16)),
        grid=(T // tm,),
        in_specs=[pl.BlockSpec((tm, D), row),
                  pl.BlockSpec((1, D), fixed),
                  pl.BlockSpec((D, N), fixed, pipeline_mode=pl.Buffered(1)),
                  pl.BlockSpec((C_CONV, n_conv), fixed),
                  pl.BlockSpec((1, n_conv), fixed),
                  pl.BlockSpec((1, n_conv), fixed)],
        out_specs=(pl.BlockSpec((tm, n_conv), row), pl.BlockSpec((tm, n_f32), row),
                   pl.BlockSpec((tm, n_mem), row)),
        scratch_shapes=[pltpu.VMEM((CONV_HALO, n_conv), F32), pltpu.VMEM((tm + CONV_HALO, CONV_CHUNK), F32)],
        compiler_params=_params("arbitrary"),
        name="proj_conv",
    )(x, g.reshape(1, D), w, conv_w, conv_b.reshape(1, n_conv), col_scale.reshape(1, n_conv))


def _mlstm_kernel(q_ref, k_ref, v_ref, o_ref, gt_ref, gb_ref, hn_ref, out_ref,
                  state_ref, m_ref, *, L):
    H, dh = C_HEADS, C_HEAD_DIM
    ext = dh + LANES

    @pl.when(pl.program_id(1) == 0)
    def _():
        state_ref[...] = jnp.zeros_like(state_ref)
        m_ref[...] = jnp.zeros_like(m_ref)

    rows = lax.broadcasted_iota(jnp.int32, (L, L), 0)
    cols = lax.broadcasted_iota(jnp.int32, (L, L), 1)
    tril = rows >= cols
    eye = rows == cols
    gates = gt_ref[...] + gb_ref[...]
    log_f = jnp.minimum(gates, 0.0) - jnp.log1p(jnp.exp(-jnp.abs(gates)))
    f_hi, f_mid, f_lo = _split3(log_f)
    trilb = tril.astype(BF16)
    bsum = _dot(trilb, f_hi) + _dot(trilb, f_mid) + _dot(trilb, f_lo)
    ones_col = (lax.broadcasted_iota(jnp.int32, (L, LANES), 1) == 0).astype(F32)

    for h in range(H):
        sl = slice(h * dh, (h + 1) * dh)
        q = q_ref[:, sl]
        k = k_ref[:, sl]
        v_ext = jnp.concatenate([v_ref[:, sl], ones_col], axis=1)
        ic = gates[:, h:h + 1]
        bc = bsum[:, H + h:H + h + 1]
        m_prev = m_ref[h][0:1, 0:1]
        rc = ic - bc
        rrow = jnp.sum(jnp.where(eye, rc, 0.0), axis=0, keepdims=True)
        dm = jnp.where(tril, bc + rrow, NEG_INF)
        m_inter = bc + m_prev
        m_t = jnp.maximum(m_inter, jnp.max(dm, axis=-1, keepdims=True))
        w = jnp.exp(dm - m_t)
        a_inter = jnp.exp(m_inter - m_t)
        sqk = _dot_nt(q, k) * w
        state = state_ref[h]
        num = a_inter * _dot(q, state.astype(BF16)) + _dot(sqk.astype(BF16), v_ext.astype(BF16))
        den = jnp.maximum(jnp.abs(num[:, dh:dh + 1]), jnp.exp(-m_t))
        hid = num[:, :dh] / den
        hid = jax.nn.sigmoid(o_ref[:, sl]) * hid
        hid = hid * lax.rsqrt(jnp.mean(hid * hid, axis=-1, keepdims=True) + EPS)
        out_ref[:, sl] = (hid * hn_ref[:, sl]).astype(out_ref.dtype)

        b_last = bc[L - 1:L, :]
        gk = b_last - bc + ic
        m_new = jnp.maximum(b_last + m_prev, jnp.max(gk, axis=0, keepdims=True))
        wk = jnp.exp(gk - m_new)
        decay = jnp.exp(b_last + m_prev - m_new)
        state_ref[h] = decay * state + _dot_tn(k, (wk * v_ext).astype(BF16))
        m_ref[h] = jnp.broadcast_to(m_new, m_ref.shape[1:])


def mlstm(qk3, u3, gate_b, head_norm, *, L=C_CHUNK):
    B, S, _ = u3.shape
    W = MIX_WIDTH
    H, dh = C_HEADS, C_HEAD_DIM
    L = min(L, S)
    gblk = 2 * W // LANES
    gb = jnp.zeros((1, LANES), F32).at[0, :2 * H].set(gate_b)
    return pl.pallas_call(
        functools.partial(_mlstm_kernel, L=L),
        out_shape=jax.ShapeDtypeStruct((B, S, W), BF16),
        grid=(B, S // L),
        in_specs=[pl.BlockSpec((None, L, W), lambda b, c: (b, c, 0)),
                  pl.BlockSpec((None, L, W), lambda b, c: (b, c, 1)),
                  pl.BlockSpec((None, L, W), lambda b, c: (b, c, 0)),
                  pl.BlockSpec((None, L, W), lambda b, c: (b, c, 1)),
                  pl.BlockSpec((None, L, LANES), lambda b, c: (b, c, gblk)),
                  pl.BlockSpec((1, LANES), lambda b, c: (0, 0)),
                  pl.BlockSpec((1, W), lambda b, c: (0, 0))],
        out_specs=pl.BlockSpec((None, L, W), lambda b, c: (b, c, 0)),
        scratch_shapes=[pltpu.VMEM((H, dh, dh + LANES), F32),
                        pltpu.VMEM((H, 8, LANES), F32)],
        compiler_params=_params("parallel", "arbitrary"),
        name="mlstm",
    )(qk3, qk3, u3, u3, u3, gb, head_norm.reshape(1, W))


def _compress_kernel(x_ref, pe_ref, w1_ref, w2_ref, o_ref):
    half = w1_ref.shape[0] // 2
    x = x_ref[...]
    n = x.shape[0]
    first = _dot(x, w1_ref[0:half, :])
    second = _dot(x, w1_ref[half:, :])
    pe = jnp.broadcast_to(pe_ref[...], (8, pe_ref.shape[1])).astype(BF16)
    bias = _dot(pe, w1_ref[...])[0:1, :]
    hid = first + pltpu.roll(second, n - 1, 0) + bias
    o_ref[...] = _dot(jax.nn.gelu(hid).astype(BF16), w2_ref[...]).astype(o_ref.dtype)


def compress(xc, pe, w1, w2):
    _, G, B, n, width = xc.shape
    hid = w1.shape[2]
    return pl.pallas_call(
        _compress_kernel,
        out_shape=jax.ShapeDtypeStruct((2, G, B, n, HEAD_DIM), BF16),
        grid=(2, G, B),
        in_specs=[pl.BlockSpec((None, None, None, n, width), lambda j, g, b: (j, g, b, 0, 0)),
                  pl.BlockSpec((None, 1, 2 * width), lambda j, g, b: (j, 0, 0)),
                  pl.BlockSpec((None, 2 * width, hid), lambda j, g, b: (j, 0, 0)),
                  pl.BlockSpec((None, hid, HEAD_DIM), lambda j, g, b: (j, 0, 0))],
        out_specs=pl.BlockSpec((None, None, None, n, HEAD_DIM), lambda j, g, b: (j, g, b, 0, 0)),
        compiler_params=_params("parallel", "parallel", "parallel"),
        name="nsa_compress",
    )(xc, pe, w1, w2)


def _cmp_select_kernel(q_ref, kc_ref, vc_ref, o_ref, sel_ref, *, tq, n_cmp, n_slc):
    i = pl.program_id(2)
    scale = HEAD_DIM ** -0.5
    npad = kc_ref.shape[0]
    kc = kc_ref[...]
    vc = vc_ref[...]
    pos = i * tq + lax.broadcasted_iota(jnp.int32, (tq, npad), 0)
    nidx = lax.broadcasted_iota(jnp.int32, (tq, npad), 1)
    cmask = (nidx * A_CMP_STRIDE + (A_CMP_LEN - 1) <= pos) & (nidx < n_cmp)
    heads = range(A_HPG)
    ss = [jnp.where(cmask, _dot_nt(q_ref[:, h * HEAD_DIM:(h + 1) * HEAD_DIM], kc) * scale, NEG_INF)
          for h in heads]
    ms = [jnp.max(s, axis=-1, keepdims=True) for s in ss]
    es = [jnp.where(cmask, jnp.exp(ss[h] - ms[h]), 0.0) for h in heads]
    ps = [es[h] / jnp.maximum(jnp.sum(es[h], axis=-1, keepdims=True), 1e-30) for h in heads]
    psum = ps[0]
    for h in heads:
        o_ref[:, h * HEAD_DIM:(h + 1) * HEAD_DIM] = _dot(ps[h].astype(BF16), vc)
        if h:
            psum = psum + ps[h]

    on = lax.broadcasted_iota(jnp.int32, (npad, LANES), 0) * A_CMP_STRIDE
    oj = lax.broadcasted_iota(jnp.int32, (npad, LANES), 1)
    overlap = ((on < (oj + 1) * A_SEL_LEN) & (on + A_CMP_LEN > oj * A_SEL_LEN)
               & (on < n_cmp * A_CMP_STRIDE) & (oj < n_slc)).astype(BF16)
    p_hi, p_lo = _split2(psum)
    imp = _dot(p_hi, overlap) + _dot(p_lo, overlap)

    sub = 8
    blk = lax.broadcasted_iota(jnp.int32, (LANES, tq), 0)
    cur = (i * tq + lax.broadcasted_iota(jnp.int32, (LANES, tq), 1)) // A_SEL_LEN
    valid = blk <= cur
    forced = (blk == 0) | (blk == cur) | (blk == cur - 1)
    score = jnp.where(valid, jnp.where(forced, A_FORCE, imp.T), A_EXCLUDE)
    row8 = lax.broadcasted_iota(jnp.int32, (sub, tq), 0)
    n_top = min(A_TOP_N, n_slc)
    chosen = []
    for v in range(0, n_slc, sub):
        mine = score[v:v + sub, :]
        rank = jnp.zeros((sub, tq), jnp.int32)
        for j in range(n_slc):
            other = score[j:j + 1, :]
            if v > j:
                ahead = other >= mine
            elif v + sub - 1 <= j:
                ahead = other > mine
            else:
                ahead = (other > mine) | ((other == mine) & (row8 + v > j))
            rank = rank + ahead.astype(jnp.int32)
        chosen.append(((rank < n_top) & valid[v:v + sub, :]).astype(F32))
    if n_slc < LANES:
        chosen.append(jnp.zeros((LANES - n_slc, tq), F32))
    sel_ref[...] = jnp.concatenate(chosen, axis=0).T.astype(sel_ref.dtype)


def cmp_select(u3, kvc, *, tq=128):
    B, S, _ = u3.shape
    G = A_KV
    npad = kvc.shape[3]
    n_cmp = (S - A_CMP_LEN) // A_CMP_STRIDE + 1
    n_slc = S // A_SEL_LEN
    tq = min(tq, S)
    gw = A_HPG * HEAD_DIM
    return pl.pallas_call(
        functools.partial(_cmp_select_kernel, tq=tq, n_cmp=n_cmp, n_slc=n_slc),
        out_shape=(jax.ShapeDtypeStruct((B, S, MIX_WIDTH), F32),
                   jax.ShapeDtypeStruct((B, G, S, LANES), BF16)),
        grid=(B, G, S // tq),
        in_specs=[pl.BlockSpec((None, tq, gw), lambda b, g, i: (b, i, g)),
                  pl.BlockSpec((None, None, None, npad, HEAD_DIM), lambda b, g, i: (0, g, b, 0, 0)),
                  pl.BlockSpec((None, None, None, npad, HEAD_DIM), lambda b, g, i: (1, g, b, 0, 0))],
        out_specs=(pl.BlockSpec((None, tq, gw), lambda b, g, i: (b, i, g)),
                   pl.BlockSpec((None, None, tq, LANES), lambda b, g, i: (b, g, i, 0))),
        compiler_params=_params("parallel", "parallel", "parallel"),
        name="nsa_cmp_select",
    )(u3, kvc, kvc)


def _masked_scores(qs, k, mask, scale):
    hp = qs.shape[0] // mask.shape[0]
    s = _dot_nt(qs, k) * scale
    return jnp.where(mask[None], s.reshape((hp,) + mask.shape), NEG_INF).reshape(s.shape)


def _slc_win_kernel(q_ref, ks_ref, vs_ref, kw_ref, vw_ref, sel_ref, oc_ref, gt_ref, o_ref, *, tq, tk, wk):
    i = pl.program_id(2)
    S = ks_ref.shape[0]
    scale = HEAD_DIM ** -0.5 * LOG2E
    hp = A_HPG
    qs = jnp.concatenate([q_ref[:, h * HEAD_DIM:(h + 1) * HEAD_DIM] for h in range(hp)], axis=0)
    sel = sel_ref[...]
    qpos = i * tq + lax.broadcasted_iota(jnp.int32, (tq, tk), 0)
    c = lax.broadcasted_iota(jnp.int32, (tq, tk), 1)
    eb = lax.broadcasted_iota(jnp.int32, (LANES, tk), 0)
    ec = lax.broadcasted_iota(jnp.int32, (LANES, tk), 1) // A_SEL_LEN

    def slc_body(j, carry):
        m, l, acc = carry
        off = pl.multiple_of(j * tk, tk)
        expand = (eb == j * (tk // A_SEL_LEN) + ec).astype(BF16)
        mask = (_dot(sel, expand) > 0.5) & (off + c <= qpos)
        s = _masked_scores(qs, ks_ref[pl.ds(off, tk), :], mask, scale)
        m_new = jnp.maximum(m, jnp.max(s, axis=-1, keepdims=True))
        p = jnp.exp2(s - m_new)
        alpha = jnp.exp2(m - m_new)
        l = alpha * l + jnp.sum(p, axis=-1, keepdims=True)
        acc = alpha * acc + _dot(p.astype(BF16), vs_ref[pl.ds(off, tk), :])
        return m_new, l, acc

    init = (jnp.full((hp * tq, 1), NEG_INF, F32), jnp.zeros((hp * tq, 1), F32),
            jnp.zeros((hp * tq, HEAD_DIM), F32))
    _, l_s, acc_s = lax.fori_loop(0, (i * tq + tq + tk - 1) // tk, slc_body, init)
    o_slc = acc_s / l_s

    start = pl.multiple_of(jnp.clip(i * tq + tq - wk, 0, S - wk), tq)
    wq = i * tq + lax.broadcasted_iota(jnp.int32, (tq, wk), 0)
    rel = wq - (start + lax.broadcasted_iota(jnp.int32, (tq, wk), 1))
    s = _masked_scores(qs, kw_ref[pl.ds(start, wk), :], (rel >= 0) & (rel < A_WINDOW), scale)
    p = jnp.exp2(s - jnp.max(s, axis=-1, keepdims=True))
    o_win = _dot(p.astype(BF16), vw_ref[pl.ds(start, wk), :]) / jnp.sum(p, axis=-1, keepdims=True)

    gate = jax.nn.sigmoid(gt_ref[...])
    for h in range(hp):
        sl = slice(h * HEAD_DIM, (h + 1) * HEAD_DIM)
        rs = slice(h * tq, (h + 1) * tq)
        o = (gate[:, h:h + 1] * oc_ref[:, sl]
             + gate[:, hp + h:hp + h + 1] * o_slc[rs]
             + gate[:, 2 * hp + h:2 * hp + h + 1] * o_win[rs])
        o_ref[:, sl] = o.astype(o_ref.dtype)


def slc_win(u3, sel, o_cmp, gate3, *, tq=128, tk=512):
    B, S, _ = u3.shape
    G = A_KV
    tq = min(tq, S)
    tk = min(tk, S)
    wk = min(A_WINDOW + tq, S)
    assert S % tk == 0 and tk % tq == 0 and S // A_SEL_LEN <= LANES
    gw = A_HPG * HEAD_DIM
    kv0 = A_HEADS

    def kv_spec(which):
        return pl.BlockSpec((None, S, HEAD_DIM), lambda b, g, i: (b, 0, kv0 + which * G + g))

    return pl.pallas_call(
        functools.partial(_slc_win_kernel, tq=tq, tk=tk, wk=wk),
        out_shape=jax.ShapeDtypeStruct((B, S, MIX_WIDTH), BF16),
        grid=(B, G, S // tq),
        in_specs=[pl.BlockSpec((None, tq, gw), lambda b, g, i: (b, i, g)),
                  kv_spec(0), kv_spec(1), kv_spec(2), kv_spec(3),
                  pl.BlockSpec((None, None, tq, LANES), lambda b, g, i: (b, g, i, 0)),
                  pl.BlockSpec((None, tq, gw), lambda b, g, i: (b, i, g)),
                  pl.BlockSpec((None, tq, LANES), lambda b, g, i: (b, i, g))],
        out_specs=pl.BlockSpec((None, tq, gw), lambda b, g, i: (b, i, g)),
        compiler_params=_params("parallel", "parallel", "arbitrary"),
        name="nsa_slc_win",
    )(u3, u3, u3, u3, u3, sel, o_cmp, gate3)


def _pad_cols(w, n):
    return jnp.pad(w, ((0, 0), (0, n - w.shape[1])))


def nsa_layer_mix(x, B, S, norm, w_in, pe_k, w1_k, w2_k, pe_v, w1_v, w2_v, mem_kv):
    H, G, dh = A_HEADS, A_KV, HEAD_DIM
    qw = H * dh
    kvw = 6 * G * dh
    gw = 3 * H
    wg = w_in[:, qw + kvw:qw + kvw + gw].reshape(-1, G, A_HPG, 3).transpose(0, 1, 3, 2)
    wg = jnp.pad(wg.reshape(-1, G, 3 * A_HPG), ((0, 0), (0, 0), (0, LANES - 3 * A_HPG)))
    n_main = qw + 4 * G * dh + MEM_WIDTH
    n_cmp = 2 * G * dh
    w_all = jnp.concatenate([w_in[:, :qw], w_in[:, qw + n_cmp:qw + kvw], w_in[:, qw + kvw + gw:],
                             w_in[:, qw:qw + n_cmp], wg.reshape(-1, G * LANES)], axis=1).astype(BF16)

    u, xc, gate3 = rms_proj(x, norm, w_all, [(0, n_main, BF16, None), (n_main, n_cmp, BF16, dh),
                                             (n_main + n_cmp, G * LANES, F32, None)])
    u = u.reshape(B, S, -1)
    gate3 = gate3.reshape(B, S, -1)
    xc = xc.reshape(2, G, B, S // A_CMP_STRIDE, A_CMP_STRIDE * dh)
    pe = jnp.stack([pe_k.reshape(1, -1), pe_v.reshape(1, -1)])
    kvc = compress(xc, pe, jnp.stack([w1_k, w1_v]).astype(BF16), jnp.stack([w2_k, w2_v]).astype(BF16))
    o_cmp, sel = cmp_select(u, kvc)
    mixed = slc_win(u, sel, o_cmp, gate3)
    mem_out = mem_attention(u, (qw + 4 * G * dh) // MEM_WIDTH, mem_kv)
    return mixed, mem_out


def stick_layer_mix(x, B, S, norm, w_in, mem_kv):
    (u,) = rms_proj(x, norm, w_in.astype(BF16), [(0, w_in.shape[1], BF16, None)])
    u = u.reshape(B, S, -1)
    mixed = stick_breaking(u)
    mem_out = mem_attention(u, 3 * MIX_WIDTH // MEM_WIDTH, mem_kv)
    return mixed, mem_out


def mlstm_layer_mix(x, B, S, norm, w_in, conv_w, conv_b, gate_b, head_norm, mem_kv):
    W = MIX_WIDTH
    nf = 4 * W + 2 * C_HEADS
    n_f = 4 * W + LANES
    w_all = jnp.concatenate([_pad_cols(w_in[:, :nf], n_f), w_in[:, nf:]], axis=1).astype(BF16)
    col_scale = jnp.concatenate([jnp.ones((W,), F32), jnp.full((W,), C_HEAD_DIM ** -0.5, F32)])
    qk, u, um = proj_conv(x, norm, w_all, conv_w, conv_b, col_scale, S, n_f - 2 * W)
    qk = qk.reshape(B, S, -1)
    u = u.reshape(B, S, -1)
    um = um.reshape(B, S, -1)
    mixed = mlstm(qk, u, gate_b, head_norm)
    mem_out = mem_attention(um, 0, mem_kv)
    return mixed, mem_out


def swa_layer_mix(x, B, S, norm, w_in, sinks, mem_kv):
    W = MIX_WIDTH
    kw = D_KV * D_HEAD_DIM
    wk = w_in[:, W:W + kw].reshape(-1, D_KV, D_HEAD_DIM)
    wv = w_in[:, W + kw:W + 2 * kw].reshape(-1, D_KV, D_HEAD_DIM)
    wkv = jnp.concatenate([wk, wv], axis=2).reshape(-1, 2 * kw)
    w_main = jnp.concatenate([w_in[:, :W], w_in[:, W + 2 * kw:], wkv], axis=1)
    w_main = _pad_cols(w_main, W + MEM_WIDTH + 4 * LANES).astype(BF16)
    (u,) = rms_proj(x, norm, w_main, [(0, w_main.shape[1], BF16, None)])
    u = u.reshape(B, S, -1)
    mixed = sink_window(u, sinks)
    mem_out = mem_attention(u, W // MEM_WIDTH, mem_kv)
    return mixed, mem_out


def kernel(x, mem, mem_norm, mem_w_kv, l0_norm_mix, l0_w_in, l0_cmp_pe_k, l0_cmp_w1_k, l0_cmp_w2_k, l0_cmp_pe_v, l0_cmp_w1_v, l0_cmp_w2_v, l0_w_out, l0_norm_ffn, l0_w_gate, l0_w_up, l0_w_down, l1_norm_mix, l1_w_in, l1_w_out, l1_norm_ffn, l1_w_gate, l1_w_up, l1_w_down, l2_norm_mix, l2_w_in, l2_conv_w, l2_conv_b, l2_gate_b, l2_head_norm, l2_w_out, l2_norm_ffn, l2_w_gate, l2_w_up, l2_w_down, l3_norm_mix, l3_w_in, l3_sinks, l3_w_out, l3_norm_ffn, l3_w_gate, l3_w_up, l3_w_down, final_norm):
    B, S, D = x.shape
    M = mem.shape[1]
    (mem_kv,) = rms_proj(mem.reshape(B * M, D), mem_norm, mem_w_kv.astype(BF16),
                         [(0, mem_w_kv.shape[1], BF16, None)])
    mem_kv = mem_kv.reshape(B, M, -1)
    xs = x.reshape(B * S, D)

    def finish(xs, mixed, mem_out, w_out, norm_ffn, w_gate, w_up, w_down, last=False):
        return out_proj_ffn(xs, mixed.reshape(B * S, -1), mem_out.reshape(B * S, -1),
                            w_out.astype(BF16), norm_ffn,
                            w_gate.astype(BF16), w_up.astype(BF16), w_down.astype(BF16),
                            final_norm, final_norm=last)

    mixed, mem_out = nsa_layer_mix(xs, B, S, l0_norm_mix, l0_w_in, l0_cmp_pe_k, l0_cmp_w1_k, l0_cmp_w2_k,
                                   l0_cmp_pe_v, l0_cmp_w1_v, l0_cmp_w2_v, mem_kv)
    xs = finish(xs, mixed, mem_out, l0_w_out, l0_norm_ffn, l0_w_gate, l0_w_up, l0_w_down)
    mixed, mem_out = stick_layer_mix(xs, B, S, l1_norm_mix, l1_w_in, mem_kv)
    xs = finish(xs, mixed, mem_out, l1_w_out, l1_norm_ffn, l1_w_gate, l1_w_up, l1_w_down)
    mixed, mem_out = mlstm_layer_mix(xs, B, S, l2_norm_mix, l2_w_in, l2_conv_w, l2_conv_b, l2_gate_b,
                                     l2_head_norm, mem_kv)
    xs = finish(xs, mixed, mem_out, l2_w_out, l2_norm_ffn, l2_w_gate, l2_w_up, l2_w_down)
    mixed, mem_out = swa_layer_mix(xs, B, S, l3_norm_mix, l3_w_in, l3_sinks, mem_kv)
    xs = finish(xs, mixed, mem_out, l3_w_out, l3_norm_ffn, l3_w_gate, l3_w_up, l3_w_down, last=True)
    return xs.reshape(B, S, D)
```

```python
import functools

import jax
import jax.numpy as jnp
from jax import lax
from jax.experimental import pallas as pl
from jax.experimental.pallas import tpu as pltpu

F32 = jnp.float32
BF16 = jnp.bfloat16

LANES = 128
VMEM_LIMIT = 56 * 1024 * 1024

D_MODEL = 2048
HEAD_DIM = 128
MEM_HEADS = 4
MEM_WIDTH = MEM_HEADS * HEAD_DIM
MIX_WIDTH = D_MODEL - MEM_WIDTH
EPS = 1e-6
NEG_INF = -1e30

A_HEADS = MIX_WIDTH // HEAD_DIM
A_KV = 2
A_HPG = A_HEADS // A_KV
A_CMP_LEN = 32
A_CMP_STRIDE = 16
A_SEL_LEN = 64
A_TOP_N = 16
A_WINDOW = 512
A_FORCE = 1e6
A_EXCLUDE = -1e9

B_HEADS = MIX_WIDTH // HEAD_DIM

C_HEADS = 4
C_HEAD_DIM = MIX_WIDTH // C_HEADS
C_CONV = 4
C_CHUNK = 256

D_HEAD_DIM = 64
D_HEADS = MIX_WIDTH // D_HEAD_DIM
D_KV = D_HEADS // 8
D_HPG = D_HEADS // D_KV
D_WINDOW = 128


def _params(*sem):
    return pltpu.CompilerParams(dimension_semantics=sem, vmem_limit_bytes=VMEM_LIMIT)


def _dot(a, b):
    return jnp.dot(a, b, preferred_element_type=F32)


def _dot_nt(a, b):
    return lax.dot_general(a, b, (((1,), (1,)), ((), ())), preferred_element_type=F32)


def _dot_tn(a, b):
    return lax.dot_general(a, b, (((0,), (0,)), ((), ())), preferred_element_type=F32)


def _split2(x):
    hi = x.astype(BF16)
    lo = (x - hi.astype(F32)).astype(BF16)
    return hi, lo


def _split3(x):
    hi = x.astype(BF16)
    r = x - hi.astype(F32)
    mid = r.astype(BF16)
    lo = (r - mid.astype(F32)).astype(BF16)
    return hi, mid, lo


PROJ_CHUNK = 1024


def _rms_proj_kernel(x_ref, g_ref, w_ref, *o_refs, outs):
    x = x_ref[...]
    r = lax.rsqrt(jnp.mean(x * x, axis=-1, keepdims=True) + EPS)
    h = (x * r * g_ref[...]).astype(BF16)
    for o_ref, (start, width, split) in zip(o_refs, outs):
        step = split if split else min(PROJ_CHUNK, width)
        for n, c0 in enumerate(range(0, width, step)):
            cw = min(step, width - c0)
            y = _dot(h, w_ref[:, start + c0:start + c0 + cw]).astype(o_ref.dtype)
            if split:
                o_ref[n] = y
            else:
                o_ref[:, c0:c0 + cw] = y


def rms_proj(x, g, w, outs, *, tm=512):
    T, D = x.shape
    N = w.shape[1]
    tm = min(tm, T)
    shapes, specs = [], []
    for start, width, dtype, split in outs:
        if split:
            shapes.append(jax.ShapeDtypeStruct((width // split, T, split), dtype))
            specs.append(pl.BlockSpec((width // split, tm, split), lambda i: (0, i, 0)))
        else:
            shapes.append(jax.ShapeDtypeStruct((T, width), dtype))
            specs.append(pl.BlockSpec((tm, width), lambda i: (i, 0)))
    return pl.pallas_call(
        functools.partial(_rms_proj_kernel, outs=tuple((s, wd, sp) for s, wd, _, sp in outs)),
        out_shape=tuple(shapes),
        grid=(T // tm,),
        in_specs=[pl.BlockSpec((tm, D), lambda i: (i, 0)),
                  pl.BlockSpec((1, D), lambda i: (0, 0)),
                  pl.BlockSpec((D, N), lambda i: (0, 0), pipeline_mode=pl.Buffered(1))],
        out_specs=tuple(specs),
        compiler_params=_params("parallel"),
        name="rms_proj",
    )(x, g.reshape(1, D), w)


def _ffn_kernel(x_ref, y1_ref, y2_ref, wo1_ref, wo2_ref, g_ref, wg_ref, wu_ref, wd_ref, fin_ref, o_ref, h_ref,
                *, final_norm):
    @pl.when(pl.program_id(1) == 0)
    def _():
        x = x_ref[...] + _dot(y1_ref[...], wo1_ref[...]) + _dot(y2_ref[...], wo2_ref[...])
        r = lax.rsqrt(jnp.mean(x * x, axis=-1, keepdims=True) + EPS)
        h_ref[...] = (x * r * g_ref[...]).astype(BF16)
        o_ref[...] = x

    h = h_ref[...]
    a = _dot(h, wg_ref[...])
    u = _dot(h, wu_ref[...])
    act = (a * jax.nn.sigmoid(a) * u).astype(BF16)
    o_ref[...] += _dot(act, wd_ref[...])

    if final_norm:
        @pl.when(pl.program_id(1) == pl.num_programs(1) - 1)
        def _():
            y = o_ref[...]
            r = lax.rsqrt(jnp.mean(y * y, axis=-1, keepdims=True) + EPS)
            o_ref[...] = y * r * fin_ref[...]


def out_proj_ffn(x, y1, y2, wo, g, wg, wu, wd, fin, *, final_norm, tm=512, tf=512):
    T, D = x.shape
    K1, K2 = y1.shape[1], y2.shape[1]
    FF = wg.shape[1]
    tm = min(tm, T)
    assert FF % tf == 0 and K1 % K2 == 0
    once = pl.Buffered(1)
    return pl.pallas_call(
        functools.partial(_ffn_kernel, final_norm=final_norm),
        out_shape=jax.ShapeDtypeStruct((T, D), F32),
        grid=(T // tm, FF // tf),
        in_specs=[pl.BlockSpec((tm, D), lambda i, f: (i, 0)),
                  pl.BlockSpec((tm, K1), lambda i, f: (i, 0)),
                  pl.BlockSpec((tm, K2), lambda i, f: (i, 0)),
                  pl.BlockSpec((K1, D), lambda i, f: (0, 0), pipeline_mode=once),
                  pl.BlockSpec((K2, D), lambda i, f: (K1 // K2, 0), pipeline_mode=once),
                  pl.BlockSpec((1, D), lambda i, f: (0, 0)),
                  pl.BlockSpec((D, tf), lambda i, f: (0, f)),
                  pl.BlockSpec((D, tf), lambda i, f: (0, f)),
                  pl.BlockSpec((tf, D), lambda i, f: (f, 0)),
                  pl.BlockSpec((1, D), lambda i, f: (0, 0))],
        out_specs=pl.BlockSpec((tm, D), lambda i, f: (i, 0)),
        scratch_shapes=[pltpu.VMEM((tm, D), BF16)],
        compiler_params=_params("parallel", "arbitrary"),
        name="out_proj_ffn",
    )(x, y1, y2, wo, wo, g.reshape(1, D), wg, wu, wd, fin.reshape(1, D))


def _mem_attn_kernel(q_ref, k_ref, v_ref, o_ref):
    scale = HEAD_DIM ** -0.5
    sls = [slice(h * HEAD_DIM, (h + 1) * HEAD_DIM) for h in range(MEM_HEADS)]
    ss = [_dot_nt(q_ref[:, sl], k_ref[:, sl]) * scale for sl in sls]
    es = [jnp.exp(s - jnp.max(s, axis=-1, keepdims=True)) for s in ss]
    ps = [e / jnp.sum(e, axis=-1, keepdims=True) for e in es]
    for sl, p in zip(sls, ps):
        o_ref[:, sl] = _dot(p.astype(BF16), v_ref[:, sl]).astype(o_ref.dtype)


def mem_attention(u3, qblk, mem_kv, *, tq=512):
    B, S, _ = u3.shape
    M = mem_kv.shape[1]
    tq = min(tq, S)
    return pl.pallas_call(
        _mem_attn_kernel,
        out_shape=jax.ShapeDtypeStruct((B, S, MEM_WIDTH), BF16),
        grid=(B, S // tq),
        in_specs=[pl.BlockSpec((None, tq, MEM_WIDTH), lambda b, i: (b, i, qblk)),
                  pl.BlockSpec((None, M, MEM_WIDTH), lambda b, i: (b, 0, 0)),
                  pl.BlockSpec((None, M, MEM_WIDTH), lambda b, i: (b, 0, 1))],
        out_specs=pl.BlockSpec((None, tq, MEM_WIDTH), lambda b, i: (b, i, 0)),
        compiler_params=_params("parallel", "parallel"),
        name="mem_attention",
    )(u3, mem_kv, mem_kv)


LOG2E = 1.4426950408889634
STICK_DEAD_LOG2 = -160.0


def _stick_kernel(q_ref, k_ref, v_ref, o_ref, acc_ref, tail_ref, z0_ref, z1_ref, *, tq, tk):
    i = pl.program_id(2)
    z_refs = (z0_ref, z1_ref)
    c2 = HEAD_DIM ** -0.5 * LOG2E
    per_q = tq // tk
    nh = q_ref.shape[1] // HEAD_DIM
    lrow = lax.broadcasted_iota(jnp.int32, (2 * tk, tk), 0)
    lcol = lax.broadcasted_iota(jnp.int32, (2 * tk, tk), 1)
    later = ((lrow > lcol) & ((lrow < tk) | (lrow > lcol + tk))).astype(BF16)
    rows = lax.broadcasted_iota(jnp.int32, (tq, tk), 0)
    cols = lax.broadcasted_iota(jnp.int32, (tq, tk), 1)
    acc_ref[...] = jnp.zeros_like(acc_ref)
    tail_ref[...] = jnp.zeros_like(tail_ref)

    def scores(j, slot):
        off = pl.multiple_of(j * tk, tk)
        for hh in range(nh):
            sl = slice(hh * HEAD_DIM, (hh + 1) * HEAD_DIM)
            z_refs[slot][hh] = _dot_nt(q_ref[:, sl], k_ref[pl.ds(off, tk), sl])

    def step(j, slot, diag):
        off = pl.multiple_of(j * tk, tk)
        r0 = 0 if diag is None else diag * tk
        if diag is not None:
            causal = (cols + diag * tk < rows)[r0:]
        scores(jnp.maximum(j - 1, 0), 1 - slot)
        part = []
        for hh in range(nh):
            ns = z_refs[slot][hh, r0:, :] * (-c2)
            log_keep = jnp.minimum(ns, 0.0) - jnp.log2(1.0 + jnp.exp2(-jnp.abs(ns)))
            if diag is not None:
                log_keep = jnp.where(causal, log_keep, 0.0)
            hi, lo = _split2(log_keep)
            after = _dot(jnp.concatenate([hi, lo], axis=1), later) + tail_ref[hh, r0:, :]
            part.append((ns, log_keep, after))
        for hh in range(nh):
            sl = slice(hh * HEAD_DIM, (hh + 1) * HEAD_DIM)
            ns, log_keep, after = part[hh]
            a = jnp.exp2(log_keep + after - ns)
            if diag is not None:
                a = jnp.where(causal, a, 0.0)
            acc_ref[hh, r0:, :] += _dot(a.astype(BF16), v_ref[pl.ds(off, tk), sl])
            tail_ref[hh, r0:, :] = after[:, 0:1] + log_keep[:, 0:1]

    assert per_q % 2 == 0
    scores((i + 1) * per_q - 1, 1)
    for d in reversed(range(per_q)):
        step(i * per_q + d, d % 2, d)

    def alive():
        return jnp.max(tail_ref[...]) > STICK_DEAD_LOG2

    def pair(state):
        jj, _ = state
        step(i * per_q - 1 - 2 * jj, 1, None)

        @pl.when(alive())
        def _():
            step(i * per_q - 2 - 2 * jj, 0, None)

        return jj + 1, alive()

    lax.while_loop(lambda s: (s[0] < i * (per_q // 2)) & s[1], pair, (jnp.int32(0), alive()))

    for hh in range(nh):
        o_ref[:, hh * HEAD_DIM:(hh + 1) * HEAD_DIM] = acc_ref[hh].astype(o_ref.dtype)


def stick_breaking(u3, *, tq=512, tk=256, nh=3):
    B, S, _ = u3.shape
    H = B_HEADS // nh
    tq = min(tq, S)
    tk = min(tk, tq)
    hw = nh * HEAD_DIM
    return pl.pallas_call(
        functools.partial(_stick_kernel, tq=tq, tk=tk),
        out_shape=jax.ShapeDtypeStruct((B, S, MIX_WIDTH), BF16),
        grid=(B, H, S // tq),
        in_specs=[pl.BlockSpec((None, tq, hw), lambda b, h, i: (b, i, h)),
                  pl.BlockSpec((None, S, hw), lambda b, h, i: (b, 0, H + h)),
                  pl.BlockSpec((None, S, hw), lambda b, h, i: (b, 0, 2 * H + h))],
        out_specs=pl.BlockSpec((None, tq, hw), lambda b, h, i: (b, i, h)),
        scratch_shapes=[pltpu.VMEM((nh, tq, HEAD_DIM), F32), pltpu.VMEM((nh, tq, 1), F32),
                        pltpu.VMEM((nh, tq, tk), F32), pltpu.VMEM((nh, tq, tk), F32)],
        compiler_params=_params("parallel", "parallel", "arbitrary"),
        name="stick_breaking",
    )(u3, u3, u3)


def _swa_kernel(sink_ref, q_ref, kvp_ref, kvc_ref, o_ref, *, tq):
    g = pl.program_id(1)
    i = pl.program_id(2)
    dh = D_HEAD_DIM
    scale = dh ** -0.5
    kv = jnp.concatenate([kvp_ref[...], kvc_ref[...]], axis=0)
    k = kv[:, :dh]
    v = kv[:, dh:]
    r = lax.broadcasted_iota(jnp.int32, (tq, tq), 0)
    c = lax.broadcasted_iota(jnp.int32, (tq, tq), 1)
    from_prev = c > r
    visible = (c <= r) | (i > 0)
    heads = range(D_HPG)
    sks = [sink_ref[g * D_HPG + hh] for hh in heads]
    raw = [_dot_nt(q_ref[:, hh * dh:(hh + 1) * dh], k) for hh in heads]
    ss = [jnp.where(visible, jnp.where(from_prev, s[:, :tq], s[:, tq:]) * scale, NEG_INF) for s in raw]
    ms = [jnp.maximum(jnp.max(ss[hh], axis=-1, keepdims=True), sks[hh]) for hh in heads]
    es = [jnp.exp(ss[hh] - ms[hh]) for hh in heads]
    ps = [es[hh] / (jnp.sum(es[hh], axis=-1, keepdims=True) + jnp.exp(sks[hh] - ms[hh])) for hh in heads]
    for hh in heads:
        p2 = jnp.concatenate([jnp.where(from_prev, ps[hh], 0.0), jnp.where(from_prev, 0.0, ps[hh])], axis=1)
        o_ref[:, hh * dh:(hh + 1) * dh] = _dot(p2.astype(BF16), v).astype(o_ref.dtype)


def sink_window(u3, sinks):
    B, S, _ = u3.shape
    tq = D_WINDOW
    gw = D_HPG * D_HEAD_DIM
    kv0 = (MIX_WIDTH + MEM_WIDTH) // (2 * D_HEAD_DIM)
    return pl.pallas_call(
        functools.partial(_swa_kernel, tq=tq),
        out_shape=jax.ShapeDtypeStruct((B, S, MIX_WIDTH), BF16),
        grid=(B, D_KV, S // tq),
        in_specs=[pl.BlockSpec(memory_space=pltpu.SMEM),
                  pl.BlockSpec((None, tq, gw), lambda b, g, i: (b, i, g)),
                  pl.BlockSpec((None, tq, 2 * D_HEAD_DIM),
                               lambda b, g, i: (b, jnp.maximum(i - 1, 0), kv0 + g)),
                  pl.BlockSpec((None, tq, 2 * D_HEAD_DIM), lambda b, g, i: (b, i, kv0 + g))],
        out_specs=pl.BlockSpec((None, tq, gw), lambda b, g, i: (b, i, g)),
        compiler_params=_params("parallel", "parallel", "arbitrary"),
        name="sink_window",
    )(sinks, u3, u3, u3)


CONV_HALO = 8
CONV_CHUNK = 512


def _proj_conv_kernel(x_ref, g_ref, w_ref, cw_ref, cb_ref, cs_ref, qk_ref, u_ref, um_ref,
                      halo_ref, buf_ref, *, tiles_per_seq, n_conv, n_f32):
    tm = x_ref.shape[0]
    hl = CONV_HALO
    x = x_ref[...]
    r = lax.rsqrt(jnp.mean(x * x, axis=-1, keepdims=True) + EPS)
    h = (x * r * g_ref[...]).astype(BF16)
    seq_start = pl.program_id(0) % tiles_per_seq == 0

    @pl.when(pl.program_id(0) == 0)
    def _():
        halo_ref[...] = jnp.zeros_like(halo_ref)

    for c0 in range(0, n_conv, CONV_CHUNK):
        sl = slice(c0, c0 + CONV_CHUNK)
        y = _dot(h, w_ref[:, sl])
        prev = halo_ref[:, sl]
        buf_ref[0:hl, :] = jnp.where(seq_start, jnp.zeros_like(prev), prev)
        buf_ref[hl:hl + tm, :] = y
        halo_ref[:, sl] = y[tm - hl:tm, :]
        acc = cb_ref[:, sl] + cw_ref[C_CONV - 1:C_CONV, sl] * y
        for j in range(C_CONV - 1):
            sh = C_CONV - 1 - j
            acc = acc + cw_ref[j:j + 1, sl] * buf_ref[hl - sh:hl - sh + tm, :]
        qk_ref[:, sl] = (acc * jax.nn.sigmoid(acc) * cs_ref[:, sl]).astype(qk_ref.dtype)
    for c0 in range(0, n_f32, PROJ_CHUNK):
        cw = min(PROJ_CHUNK, n_f32 - c0)
        u_ref[:, c0:c0 + cw] = _dot(h, w_ref[:, n_conv + c0:n_conv + c0 + cw])
    um_ref[...] = _dot(h, w_ref[:, n_conv + n_f32:]).astype(um_ref.dtype)


def proj_conv(x, g, w, conv_w, conv_b, col_scale, S, n_f32, *, tm=256):
    T, D = x.shape
    N = w.shape[1]
    n_conv = conv_w.shape[1]
    tm = min(tm, S)
    assert S % tm == 0 and n_conv % CONV_CHUNK == 0 and tm >= CONV_HALO
    n_mem = N - n_conv - n_f32
    row = lambda i: (i, 0)
    fixed = lambda i: (0, 0)
    return pl.pallas_call(
        functools.partial(_proj_conv_kernel, tiles_per_seq=S // tm, n_conv=n_conv, n_f32=n_f32),
        out_shape=(jax.ShapeDtypeStruct((T, n_conv), BF16), jax.ShapeDtypeStruct((T, n_f32), F32),
                   jax.ShapeDtypeStruct((T, n_mem), BF16)),
        grid=(T // tm,),
        in_specs=[pl.BlockSpec((tm, D), row),
                  pl.BlockSpec((1, D), fixed),
                  pl.BlockSpec((D, N), fixed, pipeline_mode=pl.Buffered(1)),
                  pl.BlockSpec((C_CONV, n_conv), fixed),
                  pl.BlockSpec((1, n_conv), fixed),
                  pl.BlockSpec((1, n_conv), fixed)],
        out_specs=(pl.BlockSpec((tm, n_conv), row), pl.BlockSpec((tm, n_f32), row),
                   pl.BlockSpec((tm, n_mem), row)),
        scratch_shapes=[pltpu.VMEM((CONV_HALO, n_conv), F32), pltpu.VMEM((tm + CONV_HALO, CONV_CHUNK), F32)],
        compiler_params=_params("arbitrary"),
        name="proj_conv",
    )(x, g.reshape(1, D), w, conv_w, conv_b.reshape(1, n_conv), col_scale.reshape(1, n_conv))


def _mlstm_kernel(q_ref, k_ref, v_ref, o_ref, gt_ref, gb_ref, hn_ref, out_ref,
                  state_ref, m_ref, *, L):
    H, dh = C_HEADS, C_HEAD_DIM
    ext = dh + LANES

    @pl.when(pl.program_id(1) == 0)
    def _():
        state_ref[...] = jnp.zeros_like(state_ref)
        m_ref[...] = jnp.zeros_like(m_ref)

    rows = lax.broadcasted_iota(jnp.int32, (L, L), 0)
    cols = lax.broadcasted_iota(jnp.int32, (L, L), 1)
    tril = rows >= cols
    eye = rows == cols
    gates = gt_ref[...] + gb_ref[...]
    log_f = jnp.minimum(gates, 0.0) - jnp.log1p(jnp.exp(-jnp.abs(gates)))
    f_hi, f_mid, f_lo = _split3(log_f)
    trilb = tril.astype(BF16)
    bsum = _dot(trilb, f_hi) + _dot(trilb, f_mid) + _dot(trilb, f_lo)
    ones_col = (lax.broadcasted_iota(jnp.int32, (L, LANES), 1) == 0).astype(F32)

    for h in range(H):
        sl = slice(h * dh, (h + 1) * dh)
        q = q_ref[:, sl]
        k = k_ref[:, sl]
        v_ext = jnp.concatenate([v_ref[:, sl], ones_col], axis=1)
        ic = gates[:, h:h + 1]
        bc = bsum[:, H + h:H + h + 1]
        m_prev = m_ref[h][0:1, 0:1]
        rc = ic - bc
        rrow = jnp.sum(jnp.where(eye, rc, 0.0), axis=0, keepdims=True)
        dm = jnp.where(tril, bc + rrow, NEG_INF)
        m_inter = bc + m_prev
        m_t = jnp.maximum(m_inter, jnp.max(dm, axis=-1, keepdims=True))
        w = jnp.exp(dm - m_t)
        a_inter = jnp.exp(m_inter - m_t)
        sqk = _dot_nt(q, k) * w
        state = state_ref[h]
        num = a_inter * _dot(q, state.astype(BF16)) + _dot(sqk.astype(BF16), v_ext.astype(BF16))
        den = jnp.maximum(jnp.abs(num[:, dh:dh + 1]), jnp.exp(-m_t))
        hid = num[:, :dh] / den
        hid = jax.nn.sigmoid(o_ref[:, sl]) * hid
        hid = hid * lax.rsqrt(jnp.mean(hid * hid, axis=-1, keepdims=True) + EPS)
        out_ref[:, sl] = (hid * hn_ref[:, sl]).astype(out_ref.dtype)

        b_last = bc[L - 1:L, :]
        gk = b_last - bc + ic
        m_new = jnp.maximum(b_last + m_prev, jnp.max(gk, axis=0, keepdims=True))
        wk = jnp.exp(gk - m_new)
        decay = jnp.exp(b_last + m_prev - m_new)
        state_ref[h] = decay * state + _dot_tn(k, (wk * v_ext).astype(BF16))
        m_ref[h] = jnp.broadcast_to(m_new, m_ref.shape[1:])


def mlstm(qk3, u3, gate_b, head_norm, *, L=C_CHUNK):
    B, S, _ = u3.shape
    W = MIX_WIDTH
    H, dh = C_HEADS, C_HEAD_DIM
    L = min(L, S)
    gblk = 2 * W // LANES
    gb = jnp.zeros((1, LANES), F32).at[0, :2 * H].set(gate_b)
    return pl.pallas_call(
        functools.partial(_mlstm_kernel, L=L),
        out_shape=jax.ShapeDtypeStruct((B, S, W), BF16),
        grid=(B, S // L),
        in_specs=[pl.BlockSpec((None, L, W), lambda b, c: (b, c, 0)),
                  pl.BlockSpec((None, L, W), lambda b, c: (b, c, 1)),
                  pl.BlockSpec((None, L, W), lambda b, c: (b, c, 0)),
                  pl.BlockSpec((None, L, W), lambda b, c: (b, c, 1)),
                  pl.BlockSpec((None, L, LANES), lambda b, c: (b, c, gblk)),
                  pl.BlockSpec((1, LANES), lambda b, c: (0, 0)),
                  pl.BlockSpec((1, W), lambda b, c: (0, 0))],
        out_specs=pl.BlockSpec((None, L, W), lambda b, c: (b, c, 0)),
        scratch_shapes=[pltpu.VMEM((H, dh, dh + LANES), F32),
                        pltpu.VMEM((H, 8, LANES), F32)],
        compiler_params=_params("parallel", "arbitrary"),
        name="mlstm",
    )(qk3, qk3, u3, u3, u3, gb, head_norm.reshape(1, W))


def _compress_kernel(x_ref, pe_ref, w1_ref, w2_ref, o_ref):
    half = w1_ref.shape[0] // 2
    x = x_ref[...]
    n = x.shape[0]
    first = _dot(x, w1_ref[0:half, :])
    second = _dot(x, w1_ref[half:, :])
    pe = jnp.broadcast_to(pe_ref[...], (8, pe_ref.shape[1])).astype(BF16)
    bias = _dot(pe, w1_ref[...])[0:1, :]
    hid = first + pltpu.roll(second, n - 1, 0) + bias
    o_ref[...] = _dot(jax.nn.gelu(hid).astype(BF16), w2_ref[...]).astype(o_ref.dtype)


def compress(xc, pe, w1, w2):
    _, G, B, n, width = xc.shape
    hid = w1.shape[2]
    return pl.pallas_call(
        _compress_kernel,
        out_shape=jax.ShapeDtypeStruct((2, G, B, n, HEAD_DIM), BF16),
        grid=(2, G, B),
        in_specs=[pl.BlockSpec((None, None, None, n, width), lambda j, g, b: (j, g, b, 0, 0)),
                  pl.BlockSpec((None, 1, 2 * width), lambda j, g, b: (j, 0, 0)),
                  pl.BlockSpec((None, 2 * width, hid), lambda j, g, b: (j, 0, 0)),
                  pl.BlockSpec((None, hid, HEAD_DIM), lambda j, g, b: (j, 0, 0))],
        out_specs=pl.BlockSpec((None, None, None, n, HEAD_DIM), lambda j, g, b: (j, g, b, 0, 0)),
        compiler_params=_params("parallel", "parallel", "parallel"),
        name="nsa_compress",
    )(xc, pe, w1, w2)


def _cmp_select_kernel(q_ref, kc_ref, vc_ref, o_ref, sel_ref, *, tq, n_cmp, n_slc):
    i = pl.program_id(2)
    scale = HEAD_DIM ** -0.5
    npad = kc_ref.shape[0]
    kc = kc_ref[...]
    vc = vc_ref[...]
    pos = i * tq + lax.broadcasted_iota(jnp.int32, (tq, npad), 0)
    nidx = lax.broadcasted_iota(jnp.int32, (tq, npad), 1)
    cmask = (nidx * A_CMP_STRIDE + (A_CMP_LEN - 1) <= pos) & (nidx < n_cmp)
    heads = range(A_HPG)
    ss = [jnp.where(cmask, _dot_nt(q_ref[:, h * HEAD_DIM:(h + 1) * HEAD_DIM], kc) * scale, NEG_INF)
          for h in heads]
    ms = [jnp.max(s, axis=-1, keepdims=True) for s in ss]
    es = [jnp.where(cmask, jnp.exp(ss[h] - ms[h]), 0.0) for h in heads]
    ps = [es[h] / jnp.maximum(jnp.sum(es[h], axis=-1, keepdims=True), 1e-30) for h in heads]
    psum = ps[0]
    for h in heads:
        o_ref[:, h * HEAD_DIM:(h + 1) * HEAD_DIM] = _dot(ps[h].astype(BF16), vc)
        if h:
            psum = psum + ps[h]

    on = lax.broadcasted_iota(jnp.int32, (npad, LANES), 0) * A_CMP_STRIDE
    oj = lax.broadcasted_iota(jnp.int32, (npad, LANES), 1)
    overlap = ((on < (oj + 1) * A_SEL_LEN) & (on + A_CMP_LEN > oj * A_SEL_LEN)
               & (on < n_cmp * A_CMP_STRIDE) & (oj < n_slc)).astype(BF16)
    p_hi, p_lo = _split2(psum)
    imp = _dot(p_hi, overlap) + _dot(p_lo, overlap)

    sub = 8
    blk = lax.broadcasted_iota(jnp.int32, (LANES, tq), 0)
    cur = (i * tq + lax.broadcasted_iota(jnp.int32, (LANES, tq), 1)) // A_SEL_LEN
    valid = blk <= cur
    forced = (blk == 0) | (blk == cur) | (blk == cur - 1)
    score = jnp.where(valid, jnp.where(forced, A_FORCE, imp.T), A_EXCLUDE)
    row8 = lax.broadcasted_iota(jnp.int32, (sub, tq), 0)
    n_top = min(A_TOP_N, n_slc)
    chosen = []
    for v in range(0, n_slc, sub):
        mine = score[v:v + sub, :]
        rank = jnp.zeros((sub, tq), jnp.int32)
        for j in range(n_slc):
            other = score[j:j + 1, :]
            if v > j:
                ahead = other >= mine
            elif v + sub - 1 <= j:
                ahead = other > mine
            else:
                ahead = (other > mine) | ((other == mine) & (row8 + v > j))
            rank = rank + ahead.astype(jnp.int32)
        chosen.append(((rank < n_top) & valid[v:v + sub, :]).astype(F32))
    if n_slc < LANES:
        chosen.append(jnp.zeros((LANES - n_slc, tq), F32))
    sel_ref[...] = jnp.concatenate(chosen, axis=0).T.astype(sel_ref.dtype)


def cmp_select(u3, kvc, *, tq=128):
    B, S, _ = u3.shape
    G = A_KV
    npad = kvc.shape[3]
    n_cmp = (S - A_CMP_LEN) // A_CMP_STRIDE + 1
    n_slc = S // A_SEL_LEN
    tq = min(tq, S)
    gw = A_HPG * HEAD_DIM
    return pl.pallas_call(
        functools.partial(_cmp_select_kernel, tq=tq, n_cmp=n_cmp, n_slc=n_slc),
        out_shape=(jax.ShapeDtypeStruct((B, S, MIX_WIDTH), F32),
                   jax.ShapeDtypeStruct((B, G, S, LANES), BF16)),
        grid=(B, G, S // tq),
        in_specs=[pl.BlockSpec((None, tq, gw), lambda b, g, i: (b, i, g)),
                  pl.BlockSpec((None, None, None, npad, HEAD_DIM), lambda b, g, i: (0, g, b, 0, 0)),
                  pl.BlockSpec((None, None, None, npad, HEAD_DIM), lambda b, g, i: (1, g, b, 0, 0))],
        out_specs=(pl.BlockSpec((None, tq, gw), lambda b, g, i: (b, i, g)),
                   pl.BlockSpec((None, None, tq, LANES), lambda b, g, i: (b, g, i, 0))),
        compiler_params=_params("parallel", "parallel", "parallel"),
        name="nsa_cmp_select",
    )(u3, kvc, kvc)


def _masked_scores(qs, k, mask, scale):
    hp = qs.shape[0] // mask.shape[0]
    s = _dot_nt(qs, k) * scale
    return jnp.where(mask[None], s.reshape((hp,) + mask.shape), NEG_INF).reshape(s.shape)


def _slc_win_kernel(q_ref, ks_ref, vs_ref, kw_ref, vw_ref, sel_ref, oc_ref, gt_ref, o_ref, *, tq, tk, wk):
    i = pl.program_id(2)
    S = ks_ref.shape[0]
    scale = HEAD_DIM ** -0.5 * LOG2E
    hp = A_HPG
    qs = jnp.concatenate([q_ref[:, h * HEAD_DIM:(h + 1) * HEAD_DIM] for h in range(hp)], axis=0)
    sel = sel_ref[...]
    qpos = i * tq + lax.broadcasted_iota(jnp.int32, (tq, tk), 0)
    c = lax.broadcasted_iota(jnp.int32, (tq, tk), 1)
    eb = lax.broadcasted_iota(jnp.int32, (LANES, tk), 0)
    ec = lax.broadcasted_iota(jnp.int32, (LANES, tk), 1) // A_SEL_LEN

    def slc_body(j, carry):
        m, l, acc = carry
        off = pl.multiple_of(j * tk, tk)
        expand = (eb == j * (tk // A_SEL_LEN) + ec).astype(BF16)
        mask = (_dot(sel, expand) > 0.5) & (off + c <= qpos)
        s = _masked_scores(qs, ks_ref[pl.ds(off, tk), :], mask, scale)
        m_new = jnp.maximum(m, jnp.max(s, axis=-1, keepdims=True))
        p = jnp.exp2(s - m_new)
        alpha = jnp.exp2(m - m_new)
        l = alpha * l + jnp.sum(p, axis=-1, keepdims=True)
        acc = alpha * acc + _dot(p.astype(BF16), vs_ref[pl.ds(off, tk), :])
        return m_new, l, acc

    init = (jnp.full((hp * tq, 1), NEG_INF, F32), jnp.zeros((hp * tq, 1), F32),
            jnp.zeros((hp * tq, HEAD_DIM), F32))
    _, l_s, acc_s = lax.fori_loop(0, (i * tq + tq + tk - 1) // tk, slc_body, init)
    o_slc = acc_s / l_s

    start = pl.multiple_of(jnp.clip(i * tq + tq - wk, 0, S - wk), tq)
    wq = i * tq + lax.broadcasted_iota(jnp.int32, (tq, wk), 0)
    rel = wq - (start + lax.broadcasted_iota(jnp.int32, (tq, wk), 1))
    s = _masked_scores(qs, kw_ref[pl.ds(start, wk), :], (rel >= 0) & (rel < A_WINDOW), scale)
    p = jnp.exp2(s - jnp.max(s, axis=-1, keepdims=True))
    o_win = _dot(p.astype(BF16), vw_ref[pl.ds(start, wk), :]) / jnp.sum(p, axis=-1, keepdims=True)

    gate = jax.nn.sigmoid(gt_ref[...])
    for h in range(hp):
        sl = slice(h * HEAD_DIM, (h + 1) * HEAD_DIM)
        rs = slice(h * tq, (h + 1) * tq)
        o = (gate[:, h:h + 1] * oc_ref[:, sl]
             + gate[:, hp + h:hp + h + 1] * o_slc[rs]
             + gate[:, 2 * hp + h:2 * hp + h + 1] * o_win[rs])
        o_ref[:, sl] = o.astype(o_ref.dtype)


def slc_win(u3, sel, o_cmp, gate3, *, tq=128, tk=512):
    B, S, _ = u3.shape
    G = A_KV
    tq = min(tq, S)
    tk = min(tk, S)
    wk = min(A_WINDOW + tq, S)
    assert S % tk == 0 and tk % tq == 0 and S // A_SEL_LEN <= LANES
    gw = A_HPG * HEAD_DIM
    kv0 = A_HEADS

    def kv_spec(which):
        return pl.BlockSpec((None, S, HEAD_DIM), lambda b, g, i: (b, 0, kv0 + which * G + g))

    return pl.pallas_call(
        functools.partial(_slc_win_kernel, tq=tq, tk=tk, wk=wk),
        out_shape=jax.ShapeDtypeStruct((B, S, MIX_WIDTH), BF16),
        grid=(B, G, S // tq),
        in_specs=[pl.BlockSpec((None, tq, gw), lambda b, g, i: (b, i, g)),
                  kv_spec(0), kv_spec(1), kv_spec(2), kv_spec(3),
                  pl.BlockSpec((None, None, tq, LANES), lambda b, g, i: (b, g, i, 0)),
                  pl.BlockSpec((None, tq, gw), lambda b, g, i: (b, i, g)),
                  pl.BlockSpec((None, tq, LANES), lambda b, g, i: (b, i, g))],
        out_specs=pl.BlockSpec((None, tq, gw), lambda b, g, i: (b, i, g)),
        compiler_params=_params("parallel", "parallel", "arbitrary"),
        name="nsa_slc_win",
    )(u3, u3, u3, u3, u3, sel, o_cmp, gate3)


def _pad_cols(w, n):
    return jnp.pad(w, ((0, 0), (0, n - w.shape[1])))


def nsa_layer_mix(x, B, S, norm, w_in, pe_k, w1_k, w2_k, pe_v, w1_v, w2_v, mem_kv):
    H, G, dh = A_HEADS, A_KV, HEAD_DIM
    qw = H * dh
    kvw = 6 * G * dh
    gw = 3 * H
    wg = w_in[:, qw + kvw:qw + kvw + gw].reshape(-1, G, A_HPG, 3).transpose(0, 1, 3, 2)
    wg = jnp.pad(wg.reshape(-1, G, 3 * A_HPG), ((0, 0), (0, 0), (0, LANES - 3 * A_HPG)))
    n_main = qw + 4 * G * dh + MEM_WIDTH
    n_cmp = 2 * G * dh
    w_all = jnp.concatenate([w_in[:, :qw], w_in[:, qw + n_cmp:qw + kvw], w_in[:, qw + kvw + gw:],
                             w_in[:, qw:qw + n_cmp], wg.reshape(-1, G * LANES)], axis=1).astype(BF16)

    u, xc, gate3 = rms_proj(x, norm, w_all, [(0, n_main, BF16, None), (n_main, n_cmp, BF16, dh),
                                             (n_main + n_cmp, G * LANES, F32, None)])
    u = u.reshape(B, S, -1)
    gate3 = gate3.reshape(B, S, -1)
    xc = xc.reshape(2, G, B, S // A_CMP_STRIDE, A_CMP_STRIDE * dh)
    pe = jnp.stack([pe_k.reshape(1, -1), pe_v.reshape(1, -1)])
    kvc = compress(xc, pe, jnp.stack([w1_k, w1_v]).astype(BF16), jnp.stack([w2_k, w2_v]).astype(BF16))
    o_cmp, sel = cmp_select(u, kvc)
    mixed = slc_win(u, sel, o_cmp, gate3)
    mem_out = mem_attention(u, (qw + 4 * G * dh) // MEM_WIDTH, mem_kv)
    return mixed, mem_out


def stick_layer_mix(x, B, S, norm, w_in, mem_kv):
    (u,) = rms_proj(x, norm, w_in.astype(BF16), [(0, w_in.shape[1], BF16, None)])
    u = u.reshape(B, S, -1)
    mixed = stick_breaking(u)
    mem_out = mem_attention(u, 3 * MIX_WIDTH // MEM_WIDTH, mem_kv)
    return mixed, mem_out


def mlstm_layer_mix(x, B, S, norm, w_in, conv_w, conv_b, gate_b, head_norm, mem_kv):
    W = MIX_WIDTH
    nf = 4 * W + 2 * C_HEADS
    n_f = 4 * W + LANES
    w_all = jnp.concatenate([_pad_cols(w_in[:, :nf], n_f), w_in[:, nf:]], axis=1).astype(BF16)
    col_scale = jnp.concatenate([jnp.ones((W,), F32), jnp.full((W,), C_HEAD_DIM ** -0.5, F32)])
    qk, u, um = proj_conv(x, norm, w_all, conv_w, conv_b, col_scale, S, n_f - 2 * W)
    qk = qk.reshape(B, S, -1)
    u = u.reshape(B, S, -1)
    um = um.reshape(B, S, -1)
    mixed = mlstm(qk, u, gate_b, head_norm)
    mem_out = mem_attention(um, 0, mem_kv)
    return mixed, mem_out


def swa_layer_mix(x, B, S, norm, w_in, sinks, mem_kv):
    W = MIX_WIDTH
    kw = D_KV * D_HEAD_DIM
    wk = w_in[:, W:W + kw].reshape(-1, D_KV, D_HEAD_DIM)
    wv = w_in[:, W + kw:W + 2 * kw].reshape(-1, D_KV, D_HEAD_DIM)
    wkv = jnp.concatenate([wk, wv], axis=2).reshape(-1, 2 * kw)
    w_main = jnp.concatenate([w_in[:, :W], w_in[:, W + 2 * kw:], wkv], axis=1)
    w_main = _pad_cols(w_main, W + MEM_WIDTH + 4 * LANES).astype(BF16)
    (u,) = rms_proj(x, norm, w_main, [(0, w_main.shape[1], BF16, None)])
    u = u.reshape(B, S, -1)
    mixed = sink_window(u, sinks)
    mem_out = mem_attention(u, W // MEM_WIDTH, mem_kv)
    return mixed, mem_out


def kernel(x, mem, mem_norm, mem_w_kv, l0_norm_mix, l0_w_in, l0_cmp_pe_k, l0_cmp_w1_k, l0_cmp_w2_k, l0_cmp_pe_v, l0_cmp_w1_v, l0_cmp_w2_v, l0_w_out, l0_norm_ffn, l0_w_gate, l0_w_up, l0_w_down, l1_norm_mix, l1_w_in, l1_w_out, l1_norm_ffn, l1_w_gate, l1_w_up, l1_w_down, l2_norm_mix, l2_w_in, l2_conv_w, l2_conv_b, l2_gate_b, l2_head_norm, l2_w_out, l2_norm_ffn, l2_w_gate, l2_w_up, l2_w_down, l3_norm_mix, l3_w_in, l3_sinks, l3_w_out, l3_norm_ffn, l3_w_gate, l3_w_up, l3_w_down, final_norm):
    B, S, D = x.shape
    M = mem.shape[1]
    (mem_kv,) = rms_proj(mem.reshape(B * M, D), mem_norm, mem_w_kv.astype(BF16),
                         [(0, mem_w_kv.shape[1], BF16, None)])
    mem_kv = mem_kv.reshape(B, M, -1)
    xs = x.reshape(B * S, D)

    def finish(xs, mixed, mem_out, w_out, norm_ffn, w_gate, w_up, w_down, last=False):
        return out_proj_ffn(xs, mixed.reshape(B * S, -1), mem_out.reshape(B * S, -1),
                            w_out.astype(BF16), norm_ffn,
                            w_gate.astype(BF16), w_up.astype(BF16), w_down.astype(BF16),
                            final_norm, final_norm=last)

    mixed, mem_out = nsa_layer_mix(xs, B, S, l0_norm_mix, l0_w_in, l0_cmp_pe_k, l0_cmp_w1_k, l0_cmp_w2_k,
                                   l0_cmp_pe_v, l0_cmp_w1_v, l0_cmp_w2_v, mem_kv)
    xs = finish(xs, mixed, mem_out, l0_w_out, l0_norm_ffn, l0_w_gate, l0_w_up, l0_w_down)
    mixed, mem_out = stick_layer_mix(xs, B, S, l1_norm_mix, l1_w_in, mem_kv)
    xs = finish(xs, mixed, mem_out, l1_w_out, l1_norm_ffn, l1_w_gate, l1_w_up, l1_w_down)
    mixed, mem_out = mlstm_layer_mix(xs, B, S, l2_norm_mix, l2_w_in, l2_conv_w, l2_conv_b, l2_gate_b,
                                     l2_head_norm, mem_kv)
    xs = finish(xs, mixed, mem_out, l2_w_out, l2_norm_ffn, l2_w_gate, l2_w_up, l2_w_down)
    mixed, mem_out = swa_layer_mix(xs, B, S, l3_norm_mix, l3_w_in, l3_sinks, mem_kv)
    xs = finish(xs, mixed, mem_out, l3_w_out, l3_norm_ffn, l3_w_gate, l3_w_up, l3_w_down, last=True)
    return xs.reshape(B, S, D)
```

```python
import functools

import jax
import jax.numpy as jnp
from jax import lax
from jax.experimental import pallas as pl
from jax.experimental.pallas import tpu as pltpu

F32 = jnp.float32
BF16 = jnp.bfloat16

LANES = 128
VMEM_LIMIT = 56 * 1024 * 1024

D_MODEL = 2048
HEAD_DIM = 128
MEM_HEADS = 4
MEM_WIDTH = MEM_HEADS * HEAD_DIM
MIX_WIDTH = D_MODEL - MEM_WIDTH
EPS = 1e-6
NEG_INF = -1e30

A_HEADS = MIX_WIDTH // HEAD_DIM
A_KV = 2
A_HPG = A_HEADS // A_KV
A_CMP_LEN = 32
A_CMP_STRIDE = 16
A_SEL_LEN = 64
A_TOP_N = 16
A_WINDOW = 512
A_FORCE = 1e6
A_EXCLUDE = -1e9

B_HEADS = MIX_WIDTH // HEAD_DIM

C_HEADS = 4
C_HEAD_DIM = MIX_WIDTH // C_HEADS
C_CONV = 4
C_CHUNK = 256

D_HEAD_DIM = 64
D_HEADS = MIX_WIDTH // D_HEAD_DIM
D_KV = D_HEADS // 8
D_HPG = D_HEADS // D_KV
D_WINDOW = 128


def _params(*sem):
    return pltpu.CompilerParams(dimension_semantics=sem, vmem_limit_bytes=VMEM_LIMIT)


def _dot(a, b):
    return jnp.dot(a, b, preferred_element_type=F32)


def _dot_nt(a, b):
    return lax.dot_general(a, b, (((1,), (1,)), ((), ())), preferred_element_type=F32)


def _dot_tn(a, b):
    return lax.dot_general(a, b, (((0,), (0,)), ((), ())), preferred_element_type=F32)


def _split2(x):
    hi = x.astype(BF16)
    lo = (x - hi.astype(F32)).astype(BF16)
    return hi, lo


def _split3(x):
    hi = x.astype(BF16)
    r = x - hi.astype(F32)
    mid = r.astype(BF16)
    lo = (r - mid.astype(F32)).astype(BF16)
    return hi, mid, lo


PROJ_CHUNK = 1024


def _rms_proj_kernel(x_ref, g_ref, w_ref, *o_refs, outs):
    x = x_ref[...]
    r = lax.rsqrt(jnp.mean(x * x, axis=-1, keepdims=True) + EPS)
    h = (x * r * g_ref[...]).astype(BF16)
    for o_ref, (start, width, split) in zip(o_refs, outs):
        step = split if split else min(PROJ_CHUNK, width)
        for n, c0 in enumerate(range(0, width, step)):
            cw = min(step, width - c0)
            y = _dot(h, w_ref[:, start + c0:start + c0 + cw]).astype(o_ref.dtype)
            if split:
                o_ref[n] = y
            else:
                o_ref[:, c0:c0 + cw] = y


def rms_proj(x, g, w, outs, *, tm=512):
    T, D = x.shape
    N = w.shape[1]
    tm = min(tm, T)
    shapes, specs = [], []
    for start, width, dtype, split in outs:
        if split:
            shapes.append(jax.ShapeDtypeStruct((width // split, T, split), dtype))
            specs.append(pl.BlockSpec((width // split, tm, split), lambda i: (0, i, 0)))
        else:
            shapes.append(jax.ShapeDtypeStruct((T, width), dtype))
            specs.append(pl.BlockSpec((tm, width), lambda i: (i, 0)))
    return pl.pallas_call(
        functools.partial(_rms_proj_kernel, outs=tuple((s, wd, sp) for s, wd, _, sp in outs)),
        out_shape=tuple(shapes),
        grid=(T // tm,),
        in_specs=[pl.BlockSpec((tm, D), lambda i: (i, 0)),
                  pl.BlockSpec((1, D), lambda i: (0, 0)),
                  pl.BlockSpec((D, N), lambda i: (0, 0), pipeline_mode=pl.Buffered(1))],
        out_specs=tuple(specs),
        compiler_params=_params("parallel"),
        name="rms_proj",
    )(x, g.reshape(1, D), w)


def _ffn_kernel(x_ref, y1_ref, y2_ref, wo1_ref, wo2_ref, g_ref, wg_ref, wu_ref, wd_ref, fin_ref, *rest,
                final_norm, cast_next):
    if cast_next:
        ng_ref, nu_ref, nd_ref, o_ref, cg_ref, cu_ref, cd_ref, h_ref = rest
        cg_ref[...] = ng_ref[...].astype(BF16)
        cu_ref[...] = nu_ref[...].astype(BF16)
    else:
        o_ref, h_ref = rest

    @pl.when(pl.program_id(1) == 0)
    def _():
        x = x_ref[...] + _dot(y1_ref[...], wo1_ref[...]) + _dot(y2_ref[...], wo2_ref[...])
        r = lax.rsqrt(jnp.mean(x * x, axis=-1, keepdims=True) + EPS)
        h_ref[...] = (x * r * g_ref[...]).astype(BF16)
        o_ref[...] = x
        if cast_next:
            cd_ref[...] = nd_ref[...].astype(BF16)

    h = h_ref[...]
    a = _dot(h, wg_ref[...])
    u = _dot(h, wu_ref[...])
    act = (a * jax.nn.sigmoid(a) * u).astype(BF16)
    o_ref[...] += _dot(act, wd_ref[...])

    if final_norm:
        @pl.when(pl.program_id(1) == pl.num_programs(1) - 1)
        def _():
            y = o_ref[...]
            r = lax.rsqrt(jnp.mean(y * y, axis=-1, keepdims=True) + EPS)
            o_ref[...] = y * r * fin_ref[...]


def out_proj_ffn(x, y1, y2, wo, g, wg, wu, wd, fin, *, final_norm, next_f32=None, tm=512, tf=512):
    T, D = x.shape
    K1, K2 = y1.shape[1], y2.shape[1]
    FF = wg.shape[1]
    tm = min(tm, T)
    assert FF % tf == 0 and K1 % K2 == 0
    once = pl.Buffered(1)
    n_i = T // tm
    extra_in, extra_out, extra_shape, extra_args = [], [], [], []
    if next_f32 is not None:
        assert D % (16 * n_i) == 0 and FF % (16 * n_i) == 0
        up_spec = pl.BlockSpec((D // n_i, tf), lambda i, f: (i, f))
        down_spec = pl.BlockSpec((FF // n_i, D), lambda i, f: (i, 0))
        extra_in = [up_spec, up_spec, down_spec]
        extra_out = [up_spec, up_spec, down_spec]
        extra_shape = [jax.ShapeDtypeStruct(w.shape, BF16) for w in next_f32]
        extra_args = list(next_f32)
    out = pl.pallas_call(
        functools.partial(_ffn_kernel, final_norm=final_norm, cast_next=next_f32 is not None),
        out_shape=[jax.ShapeDtypeStruct((T, D), F32)] + extra_shape,
        grid=(n_i, FF // tf),
        in_specs=[pl.BlockSpec((tm, D), lambda i, f: (i, 0)),
                  pl.BlockSpec((tm, K1), lambda i, f: (i, 0)),
                  pl.BlockSpec((tm, K2), lambda i, f: (i, 0)),
                  pl.BlockSpec((K1, D), lambda i, f: (0, 0), pipeline_mode=once),
                  pl.BlockSpec((K2, D), lambda i, f: (K1 // K2, 0), pipeline_mode=once),
                  pl.BlockSpec((1, D), lambda i, f: (0, 0)),
                  pl.BlockSpec((D, tf), lambda i, f: (0, f)),
                  pl.BlockSpec((D, tf), lambda i, f: (0, f)),
                  pl.BlockSpec((tf, D), lambda i, f: (f, 0)),
                  pl.BlockSpec((1, D), lambda i, f: (0, 0))] + extra_in,
        out_specs=[pl.BlockSpec((tm, D), lambda i, f: (i, 0))] + extra_out,
        scratch_shapes=[pltpu.VMEM((tm, D), BF16)],
        compiler_params=_params("parallel", "arbitrary"),
        name="out_proj_ffn",
    )(x, y1, y2, wo, wo, g.reshape(1, D), wg, wu, wd, fin.reshape(1, D), *extra_args)
    return out[0] if next_f32 is None else tuple(out)


def _mem_attn_kernel(q_ref, k_ref, v_ref, o_ref):
    scale = HEAD_DIM ** -0.5
    sls = [slice(h * HEAD_DIM, (h + 1) * HEAD_DIM) for h in range(MEM_HEADS)]
    ss = [_dot_nt(q_ref[:, sl], k_ref[:, sl]) * scale for sl in sls]
    es = [jnp.exp(s - jnp.max(s, axis=-1, keepdims=True)) for s in ss]
    ps = [e / jnp.sum(e, axis=-1, keepdims=True) for e in es]
    for sl, p in zip(sls, ps):
        o_ref[:, sl] = _dot(p.astype(BF16), v_ref[:, sl]).astype(o_ref.dtype)


def mem_attention(u3, qblk, mem_kv, *, tq=512):
    B, S, _ = u3.shape
    M = mem_kv.shape[1]
    tq = min(tq, S)
    return pl.pallas_call(
        _mem_attn_kernel,
        out_shape=jax.ShapeDtypeStruct((B, S, MEM_WIDTH), BF16),
        grid=(B, S // tq),
        in_specs=[pl.BlockSpec((None, tq, MEM_WIDTH), lambda b, i: (b, i, qblk)),
                  pl.BlockSpec((None, M, MEM_WIDTH), lambda b, i: (b, 0, 0)),
                  pl.BlockSpec((None, M, MEM_WIDTH), lambda b, i: (b, 0, 1))],
        out_specs=pl.BlockSpec((None, tq, MEM_WIDTH), lambda b, i: (b, i, 0)),
        compiler_params=_params("parallel", "parallel"),
        name="mem_attention",
    )(u3, mem_kv, mem_kv)


LOG2E = 1.4426950408889634
STICK_DEAD_LOG2 = -160.0


def _stick_kernel(q_ref, k_ref, v_ref, o_ref, acc_ref, tail_ref, z0_ref, z1_ref, *, tq, tk):
    i = pl.program_id(2)
    z_refs = (z0_ref, z1_ref)
    c2 = HEAD_DIM ** -0.5 * LOG2E
    per_q = tq // tk
    nh = q_ref.shape[1] // HEAD_DIM
    lrow = lax.broadcasted_iota(jnp.int32, (2 * tk, tk), 0)
    lcol = lax.broadcasted_iota(jnp.int32, (2 * tk, tk), 1)
    later = ((lrow > lcol) & ((lrow < tk) | (lrow > lcol + tk))).astype(BF16)
    rows = lax.broadcasted_iota(jnp.int32, (tq, tk), 0)
    cols = lax.broadcasted_iota(jnp.int32, (tq, tk), 1)
    acc_ref[...] = jnp.zeros_like(acc_ref)
    tail_ref[...] = jnp.zeros_like(tail_ref)

    def scores(j, slot):
        off = pl.multiple_of(j * tk, tk)
        for hh in range(nh):
            sl = slice(hh * HEAD_DIM, (hh + 1) * HEAD_DIM)
            z_refs[slot][hh] = _dot_nt(q_ref[:, sl], k_ref[pl.ds(off, tk), sl])

    def step(j, slot, diag):
        off = pl.multiple_of(j * tk, tk)
        r0 = 0 if diag is None else diag * tk
        if diag is not None:
            causal = (cols + diag * tk < rows)[r0:]
        scores(jnp.maximum(j - 1, 0), 1 - slot)
        part = []
        for hh in range(nh):
            ns = z_refs[slot][hh, r0:, :] * (-c2)
            log_keep = jnp.minimum(ns, 0.0) - jnp.log2(1.0 + jnp.exp2(-jnp.abs(ns)))
            if diag is not None:
                log_keep = jnp.where(causal, log_keep, 0.0)
            hi, lo = _split2(log_keep)
            after = _dot(jnp.concatenate([hi, lo], axis=1), later) + tail_ref[hh, r0:, :]
            part.append((ns, log_keep, after))
        for hh in range(nh):
            sl = slice(hh * HEAD_DIM, (hh + 1) * HEAD_DIM)
            ns, log_keep, after = part[hh]
            a = jnp.exp2(log_keep + after - ns)
            if diag is not None:
                a = jnp.where(causal, a, 0.0)
            acc_ref[hh, r0:, :] += _dot(a.astype(BF16), v_ref[pl.ds(off, tk), sl])
            tail_ref[hh, r0:, :] = after[:, 0:1] + log_keep[:, 0:1]

    assert per_q % 2 == 0
    scores((i + 1) * per_q - 1, 1)
    for d in reversed(range(per_q)):
        step(i * per_q + d, d % 2, d)

    def alive():
        return jnp.max(tail_ref[...]) > STICK_DEAD_LOG2

    def pair(state):
        jj, _ = state
        step(i * per_q - 1 - 2 * jj, 1, None)

        @pl.when(alive())
        def _():
            step(i * per_q - 2 - 2 * jj, 0, None)

        return jj + 1, alive()

    lax.while_loop(lambda s: (s[0] < i * (per_q // 2)) & s[1], pair, (jnp.int32(0), alive()))

    for hh in range(nh):
        o_ref[:, hh * HEAD_DIM:(hh + 1) * HEAD_DIM] = acc_ref[hh].astype(o_ref.dtype)


def stick_breaking(u3, *, tq=512, tk=256, nh=3):
    B, S, _ = u3.shape
    H = B_HEADS // nh
    tq = min(tq, S)
    tk = min(tk, tq)
    hw = nh * HEAD_DIM
    return pl.pallas_call(
        functools.partial(_stick_kernel, tq=tq, tk=tk),
        out_shape=jax.ShapeDtypeStruct((B, S, MIX_WIDTH), BF16),
        grid=(B, H, S // tq),
        in_specs=[pl.BlockSpec((None, tq, hw), lambda b, h, i: (b, i, h)),
                  pl.BlockSpec((None, S, hw), lambda b, h, i: (b, 0, H + h)),
                  pl.BlockSpec((None, S, hw), lambda b, h, i: (b, 0, 2 * H + h))],
        out_specs=pl.BlockSpec((None, tq, hw), lambda b, h, i: (b, i, h)),
        scratch_shapes=[pltpu.VMEM((nh, tq, HEAD_DIM), F32), pltpu.VMEM((nh, tq, 1), F32),
                        pltpu.VMEM((nh, tq, tk), F32), pltpu.VMEM((nh, tq, tk), F32)],
        compiler_params=_params("parallel", "parallel", "arbitrary"),
        name="stick_breaking",
    )(u3, u3, u3)


def _swa_kernel(sink_ref, q_ref, kvp_ref, kvc_ref, o_ref, *, tq):
    g = pl.program_id(1)
    i = pl.program_id(2)
    dh = D_HEAD_DIM
    scale = dh ** -0.5
    kv = jnp.concatenate([kvp_ref[...], kvc_ref[...]], axis=0)
    k = kv[:, :dh]
    v = kv[:, dh:]
    r = lax.broadcasted_iota(jnp.int32, (tq, tq), 0)
    c = lax.broadcasted_iota(jnp.int32, (tq, tq), 1)
    from_prev = c > r
    visible = (c <= r) | (i > 0)
    heads = range(D_HPG)
    sks = [sink_ref[g * D_HPG + hh] for hh in heads]
    raw = [_dot_nt(q_ref[:, hh * dh:(hh + 1) * dh], k) for hh in heads]
    ss = [jnp.where(visible, jnp.where(from_prev, s[:, :tq], s[:, tq:]) * scale, NEG_INF) for s in raw]
    ms = [jnp.maximum(jnp.max(ss[hh], axis=-1, keepdims=True), sks[hh]) for hh in heads]
    es = [jnp.exp(ss[hh] - ms[hh]) for hh in heads]
    ps = [es[hh] / (jnp.sum(es[hh], axis=-1, keepdims=True) + jnp.exp(sks[hh] - ms[hh])) for hh in heads]
    for hh in heads:
        p2 = jnp.concatenate([jnp.where(from_prev, ps[hh], 0.0), jnp.where(from_prev, 0.0, ps[hh])], axis=1)
        o_ref[:, hh * dh:(hh + 1) * dh] = _dot(p2.astype(BF16), v).astype(o_ref.dtype)


def sink_window(u3, sinks):
    B, S, _ = u3.shape
    tq = D_WINDOW
    gw = D_HPG * D_HEAD_DIM
    kv0 = (MIX_WIDTH + MEM_WIDTH) // (2 * D_HEAD_DIM)
    return pl.pallas_call(
        functools.partial(_swa_kernel, tq=tq),
        out_shape=jax.ShapeDtypeStruct((B, S, MIX_WIDTH), BF16),
        grid=(B, D_KV, S // tq),
        in_specs=[pl.BlockSpec(memory_space=pltpu.SMEM),
                  pl.BlockSpec((None, tq, gw), lambda b, g, i: (b, i, g)),
                  pl.BlockSpec((None, tq, 2 * D_HEAD_DIM),
                               lambda b, g, i: (b, jnp.maximum(i - 1, 0), kv0 + g)),
                  pl.BlockSpec((None, tq, 2 * D_HEAD_DIM), lambda b, g, i: (b, i, kv0 + g))],
        out_specs=pl.BlockSpec((None, tq, gw), lambda b, g, i: (b, i, g)),
        compiler_params=_params("parallel", "parallel", "arbitrary"),
        name="sink_window",
    )(sinks, u3, u3, u3)


CONV_HALO = 8
CONV_CHUNK = 512


def _proj_conv_kernel(x_ref, g_ref, w_ref, cw_ref, cb_ref, cs_ref, qk_ref, u_ref, um_ref,
                      halo_ref, buf_ref, *, tiles_per_seq, n_conv, n_f32):
    tm = x_ref.shape[0]
    hl = CONV_HALO
    x = x_ref[...]
    r = lax.rsqrt(jnp.mean(x * x, axis=-1, keepdims=True) + EPS)
    h = (x * r * g_ref[...]).astype(BF16)
    seq_start = pl.program_id(0) % tiles_per_seq == 0

    @pl.when(pl.program_id(0) == 0)
    def _():
        halo_ref[...] = jnp.zeros_like(halo_ref)

    for c0 in range(0, n_conv, CONV_CHUNK):
        sl = slice(c0, c0 + CONV_CHUNK)
        y = _dot(h, w_ref[:, sl])
        prev = halo_ref[:, sl]
        buf_ref[0:hl, :] = jnp.where(seq_start, jnp.zeros_like(prev), prev)
        buf_ref[hl:hl + tm, :] = y
        halo_ref[:, sl] = y[tm - hl:tm, :]
        acc = cb_ref[:, sl] + cw_ref[C_CONV - 1:C_CONV, sl] * y
        for j in range(C_CONV - 1):
            sh = C_CONV - 1 - j
            acc = acc + cw_ref[j:j + 1, sl] * buf_ref[hl - sh:hl - sh + tm, :]
        qk_ref[:, sl] = (acc * jax.nn.sigmoid(acc) * cs_ref[:, sl]).astype(qk_ref.dtype)
    for c0 in range(0, n_f32, PROJ_CHUNK):
        cw = min(PROJ_CHUNK, n_f32 - c0)
        u_ref[:, c0:c0 + cw] = _dot(h, w_ref[:, n_conv + c0:n_conv + c0 + cw])
    um_ref[...] = _dot(h, w_ref[:, n_conv + n_f32:]).astype(um_ref.dtype)


def proj_conv(x, g, w, conv_w, conv_b, col_scale, S, n_f32, *, tm=256):
    T, D = x.shape
    N = w.shape[1]
    n_conv = conv_w.shape[1]
    tm = min(tm, S)
    assert S % tm == 0 and n_conv % CONV_CHUNK == 0 and tm >= CONV_HALO
    n_mem = N - n_conv - n_f32
    row = lambda i: (i, 0)
    fixed = lambda i: (0, 0)
    return pl.pallas_call(
        functools.partial(_proj_conv_kernel, tiles_per_seq=S // tm, n_conv=n_conv, n_f32=n_f32),
        out_shape=(jax.ShapeDtypeStruct((T, n_conv), BF16), jax.ShapeDtypeStruct((T, n_f32), F32),
                   jax.ShapeDtypeStruct((T, n_mem), BF16)),
        grid=(T // tm,),
        in_specs=[pl.BlockSpec((tm, D), row),
                  pl.BlockSpec((1, D), fixed),
                  pl.BlockSpec((D, N), fixed, pipeline_mode=pl.Buffered(1)),
                  pl.BlockSpec((C_CONV, n_conv), fixed),
                  pl.BlockSpec((1, n_conv), fixed),
                  pl.BlockSpec((1, n_conv), fixed)],
        out_specs=(pl.BlockSpec((tm, n_conv), row), pl.BlockSpec((tm, n_f32), row),
                   pl.BlockSpec((tm, n_mem), row)),
        scratch_shapes=[pltpu.VMEM((CONV_HALO, n_conv), F32), pltpu.VMEM((tm + CONV_HALO, CONV_CHUNK), F32)],
        compiler_params=_params("arbitrary"),
        name="proj_conv",
    )(x, g.reshape(1, D), w, conv_w, conv_b.reshape(1, n_conv), col_scale.reshape(1, n_conv))


def _mlstm_kernel(q_ref, k_ref, v_ref, o_ref, gt_ref, gb_ref, hn_ref, out_ref,
                  state_ref, m_ref, *, L):
    H, dh = C_HEADS, C_HEAD_DIM
    ext = dh + LANES

    @pl.when(pl.program_id(1) == 0)
    def _():
        state_ref[...] = jnp.zeros_like(state_ref)
        m_ref[...] = jnp.zeros_like(m_ref)

    rows = lax.broadcasted_iota(jnp.int32, (L, L), 0)
    cols = lax.broadcasted_iota(jnp.int32, (L, L), 1)
    tril = rows >= cols
    eye = rows == cols
    gates = gt_ref[...] + gb_ref[...]
    log_f = jnp.minimum(gates, 0.0) - jnp.log1p(jnp.exp(-jnp.abs(gates)))
    f_hi, f_mid, f_lo = _split3(log_f)
    trilb = tril.astype(BF16)
    bsum = _dot(trilb, f_hi) + _dot(trilb, f_mid) + _dot(trilb, f_lo)
    ones_col = (lax.broadcasted_iota(jnp.int32, (L, LANES), 1) == 0).astype(F32)

    for h in range(H):
        sl = slice(h * dh, (h + 1) * dh)
        q = q_ref[:, sl]
        k = k_ref[:, sl]
        v_ext = jnp.concatenate([v_ref[:, sl], ones_col], axis=1)
        ic = gates[:, h:h + 1]
        bc = bsum[:, H + h:H + h + 1]
        m_prev = m_ref[h][0:1, 0:1]
        rc = ic - bc
        rrow = jnp.sum(jnp.where(eye, rc, 0.0), axis=0, keepdims=True)
        dm = jnp.where(tril, bc + rrow, NEG_INF)
        m_inter = bc + m_prev
        m_t = jnp.maximum(m_inter, jnp.max(dm, axis=-1, keepdims=True))
        w = jnp.exp(dm - m_t)
        a_inter = jnp.exp(m_inter - m_t)
        sqk = _dot_nt(q, k) * w
        state = state_ref[h]
        num = a_inter * _dot(q, state.astype(BF16)) + _dot(sqk.astype(BF16), v_ext.astype(BF16))
        den = jnp.maximum(jnp.abs(num[:, dh:dh + 1]), jnp.exp(-m_t))
        hid = num[:, :dh] / den
        hid = jax.nn.sigmoid(o_ref[:, sl]) * hid
        hid = hid * lax.rsqrt(jnp.mean(hid * hid, axis=-1, keepdims=True) + EPS)
        out_ref[:, sl] = (hid * hn_ref[:, sl]).astype(out_ref.dtype)

        b_last = bc[L - 1:L, :]
        gk = b_last - bc + ic
        m_new = jnp.maximum(b_last + m_prev, jnp.max(gk, axis=0, keepdims=True))
        wk = jnp.exp(gk - m_new)
        decay = jnp.exp(b_last + m_prev - m_new)
        state_ref[h] = decay * state + _dot_tn(k, (wk * v_ext).astype(BF16))
        m_ref[h] = jnp.broadcast_to(m_new, m_ref.shape[1:])


def mlstm(qk3, u3, gate_b, head_norm, *, L=C_CHUNK):
    B, S, _ = u3.shape
    W = MIX_WIDTH
    H, dh = C_HEADS, C_HEAD_DIM
    L = min(L, S)
    gblk = 2 * W // LANES
    gb = jnp.zeros((1, LANES), F32).at[0, :2 * H].set(gate_b)
    return pl.pallas_call(
        functools.partial(_mlstm_kernel, L=L),
        out_shape=jax.ShapeDtypeStruct((B, S, W), BF16),
        grid=(B, S // L),
        in_specs=[pl.BlockSpec((None, L, W), lambda b, c: (b, c, 0)),
                  pl.BlockSpec((None, L, W), lambda b, c: (b, c, 1)),
                  pl.BlockSpec((None, L, W), lambda b, c: (b, c, 0)),
                  pl.BlockSpec((None, L, W), lambda b, c: (b, c, 1)),
                  pl.BlockSpec((None, L, LANES), lambda b, c: (b, c, gblk)),
                  pl.BlockSpec((1, LANES), lambda b, c: (0, 0)),
                  pl.BlockSpec((1, W), lambda b, c: (0, 0))],
        out_specs=pl.BlockSpec((None, L, W), lambda b, c: (b, c, 0)),
        scratch_shapes=[pltpu.VMEM((H, dh, dh + LANES), F32),
                        pltpu.VMEM((H, 8, LANES), F32)],
        compiler_params=_params("parallel", "arbitrary"),
        name="mlstm",
    )(qk3, qk3, u3, u3, u3, gb, head_norm.reshape(1, W))


def _compress_kernel(x_ref, pe_ref, w1_ref, w2_ref, o_ref):
    half = w1_ref.shape[0] // 2
    x = x_ref[...]
    n = x.shape[0]
    first = _dot(x, w1_ref[0:half, :])
    second = _dot(x, w1_ref[half:, :])
    pe = jnp.broadcast_to(pe_ref[...], (8, pe_ref.shape[1])).astype(BF16)
    bias = _dot(pe, w1_ref[...])[0:1, :]
    hid = first + pltpu.roll(second, n - 1, 0) + bias
    o_ref[...] = _dot(jax.nn.gelu(hid).astype(BF16), w2_ref[...]).astype(o_ref.dtype)


def compress(xc, pe, w1, w2):
    _, G, B, n, width = xc.shape
    hid = w1.shape[2]
    return pl.pallas_call(
        _compress_kernel,
        out_shape=jax.ShapeDtypeStruct((2, G, B, n, HEAD_DIM), BF16),
        grid=(2, G, B),
        in_specs=[pl.BlockSpec((None, None, None, n, width), lambda j, g, b: (j, g, b, 0, 0)),
                  pl.BlockSpec((None, 1, 2 * width), lambda j, g, b: (j, 0, 0)),
                  pl.BlockSpec((None, 2 * width, hid), lambda j, g, b: (j, 0, 0)),
                  pl.BlockSpec((None, hid, HEAD_DIM), lambda j, g, b: (j, 0, 0))],
        out_specs=pl.BlockSpec((None, None, None, n, HEAD_DIM), lambda j, g, b: (j, g, b, 0, 0)),
        compiler_params=_params("parallel", "parallel", "parallel"),
        name="nsa_compress",
    )(xc, pe, w1, w2)


def _cmp_select_kernel(q_ref, kc_ref, vc_ref, o_ref, sel_ref, *, tq, n_cmp, n_slc):
    i = pl.program_id(2)
    scale = HEAD_DIM ** -0.5
    npad = kc_ref.shape[0]
    kc = kc_ref[...]
    vc = vc_ref[...]
    pos = i * tq + lax.broadcasted_iota(jnp.int32, (tq, npad), 0)
    nidx = lax.broadcasted_iota(jnp.int32, (tq, npad), 1)
    cmask = (nidx * A_CMP_STRIDE + (A_CMP_LEN - 1) <= pos) & (nidx < n_cmp)
    heads = range(A_HPG)
    ss = [jnp.where(cmask, _dot_nt(q_ref[:, h * HEAD_DIM:(h + 1) * HEAD_DIM], kc) * scale, NEG_INF)
          for h in heads]
    ms = [jnp.max(s, axis=-1, keepdims=True) for s in ss]
    es = [jnp.where(cmask, jnp.exp(ss[h] - ms[h]), 0.0) for h in heads]
    ps = [es[h] / jnp.maximum(jnp.sum(es[h], axis=-1, keepdims=True), 1e-30) for h in heads]
    psum = ps[0]
    for h in heads:
        o_ref[:, h * HEAD_DIM:(h + 1) * HEAD_DIM] = _dot(ps[h].astype(BF16), vc)
        if h:
            psum = psum + ps[h]

    on = lax.broadcasted_iota(jnp.int32, (npad, LANES), 0) * A_CMP_STRIDE
    oj = lax.broadcasted_iota(jnp.int32, (npad, LANES), 1)
    overlap = ((on < (oj + 1) * A_SEL_LEN) & (on + A_CMP_LEN > oj * A_SEL_LEN)
               & (on < n_cmp * A_CMP_STRIDE) & (oj < n_slc)).astype(BF16)
    p_hi, p_lo = _split2(psum)
    imp = _dot(p_hi, overlap) + _dot(p_lo, overlap)

    sub = 8
    blk = lax.broadcasted_iota(jnp.int32, (LANES, tq), 0)
    cur = (i * tq + lax.broadcasted_iota(jnp.int32, (LANES, tq), 1)) // A_SEL_LEN
    valid = blk <= cur
    forced = (blk == 0) | (blk == cur) | (blk == cur - 1)
    score = jnp.where(valid, jnp.where(forced, A_FORCE, imp.T), A_EXCLUDE)
    row8 = lax.broadcasted_iota(jnp.int32, (sub, tq), 0)
    n_top = min(A_TOP_N, n_slc)
    chosen = []
    for v in range(0, n_slc, sub):
        mine = score[v:v + sub, :]
        rank = jnp.zeros((sub, tq), jnp.int32)
        for j in range(n_slc):
            other = score[j:j + 1, :]
            if v > j:
                ahead = other >= mine
            elif v + sub - 1 <= j:
                ahead = other > mine
            else:
                ahead = (other > mine) | ((other == mine) & (row8 + v > j))
            rank = rank + ahead.astype(jnp.int32)
        chosen.append(((rank < n_top) & valid[v:v + sub, :]).astype(F32))
    if n_slc < LANES:
        chosen.append(jnp.zeros((LANES - n_slc, tq), F32))
    sel_ref[...] = jnp.concatenate(chosen, axis=0).T.astype(sel_ref.dtype)


def cmp_select(u3, kvc, *, tq=128):
    B, S, _ = u3.shape
    G = A_KV
    npad = kvc.shape[3]
    n_cmp = (S - A_CMP_LEN) // A_CMP_STRIDE + 1
    n_slc = S // A_SEL_LEN
    tq = min(tq, S)
    gw = A_HPG * HEAD_DIM
    return pl.pallas_call(
        functools.partial(_cmp_select_kernel, tq=tq, n_cmp=n_cmp, n_slc=n_slc),
        out_shape=(jax.ShapeDtypeStruct((B, S, MIX_WIDTH), F32),
                   jax.ShapeDtypeStruct((B, G, S, LANES), BF16)),
        grid=(B, G, S // tq),
        in_specs=[pl.BlockSpec((None, tq, gw), lambda b, g, i: (b, i, g)),
                  pl.BlockSpec((None, None, None, npad, HEAD_DIM), lambda b, g, i: (0, g, b, 0, 0)),
                  pl.BlockSpec((None, None, None, npad, HEAD_DIM), lambda b, g, i: (1, g, b, 0, 0))],
        out_specs=(pl.BlockSpec((None, tq, gw), lambda b, g, i: (b, i, g)),
                   pl.BlockSpec((None, None, tq, LANES), lambda b, g, i: (b, g, i, 0))),
        compiler_params=_params("parallel", "parallel", "parallel"),
        name="nsa_cmp_select",
    )(u3, kvc, kvc)


def _masked_scores(qs, k, mask, scale):
    hp = qs.shape[0] // mask.shape[0]
    s = _dot_nt(qs, k) * scale
    return jnp.where(mask[None], s.reshape((hp,) + mask.shape), NEG_INF).reshape(s.shape)


def _slc_win_kernel(q_ref, ks_ref, vs_ref, kw_ref, vw_ref, sel_ref, oc_ref, gt_ref, o_ref, *, tq, tk, wk):
    i = pl.program_id(2)
    S = ks_ref.shape[0]
    scale = HEAD_DIM ** -0.5 * LOG2E
    hp = A_HPG
    qs = jnp.concatenate([q_ref[:, h * HEAD_DIM:(h + 1) * HEAD_DIM] for h in range(hp)], axis=0)
    sel = sel_ref[...]
    qpos = i * tq + lax.broadcasted_iota(jnp.int32, (tq, tk), 0)
    c = lax.broadcasted_iota(jnp.int32, (tq, tk), 1)
    eb = lax.broadcasted_iota(jnp.int32, (LANES, tk), 0)
    ec = lax.broadcasted_iota(jnp.int32, (LANES, tk), 1) // A_SEL_LEN

    def slc_body(j, carry):
        m, l, acc = carry
        off = pl.multiple_of(j * tk, tk)
        expand = (eb == j * (tk // A_SEL_LEN) + ec).astype(BF16)
        mask = (_dot(sel, expand) > 0.5) & (off + c <= qpos)
        s = _masked_scores(qs, ks_ref[pl.ds(off, tk), :], mask, scale)
        m_new = jnp.maximum(m, jnp.max(s, axis=-1, keepdims=True))
        p = jnp.exp2(s - m_new)
        alpha = jnp.exp2(m - m_new)
        l = alpha * l + jnp.sum(p, axis=-1, keepdims=True)
        acc = alpha * acc + _dot(p.astype(BF16), vs_ref[pl.ds(off, tk), :])
        return m_new, l, acc

    init = (jnp.full((hp * tq, 1), NEG_INF, F32), jnp.zeros((hp * tq, 1), F32),
            jnp.zeros((hp * tq, HEAD_DIM), F32))
    _, l_s, acc_s = lax.fori_loop(0, (i * tq + tq + tk - 1) // tk, slc_body, init)
    o_slc = acc_s / l_s

    start = pl.multiple_of(jnp.clip(i * tq + tq - wk, 0, S - wk), tq)
    wq = i * tq + lax.broadcasted_iota(jnp.int32, (tq, wk), 0)
    rel = wq - (start + lax.broadcasted_iota(jnp.int32, (tq, wk), 1))
    s = _masked_scores(qs, kw_ref[pl.ds(start, wk), :], (rel >= 0) & (rel < A_WINDOW), scale)
    p = jnp.exp2(s - jnp.max(s, axis=-1, keepdims=True))
    o_win = _dot(p.astype(BF16), vw_ref[pl.ds(start, wk), :]) / jnp.sum(p, axis=-1, keepdims=True)

    gate = jax.nn.sigmoid(gt_ref[...])
    for h in range(hp):
        sl = slice(h * HEAD_DIM, (h + 1) * HEAD_DIM)
        rs = slice(h * tq, (h + 1) * tq)
        o = (gate[:, h:h + 1] * oc_ref[:, sl]
             + gate[:, hp + h:hp + h + 1] * o_slc[rs]
             + gate[:, 2 * hp + h:2 * hp + h + 1] * o_win[rs])
        o_ref[:, sl] = o.astype(o_ref.dtype)


def slc_win(u3, sel, o_cmp, gate3, *, tq=128, tk=512):
    B, S, _ = u3.shape
    G = A_KV
    tq = min(tq, S)
    tk = min(tk, S)
    wk = min(A_WINDOW + tq, S)
    assert S % tk == 0 and tk % tq == 0 and S // A_SEL_LEN <= LANES
    gw = A_HPG * HEAD_DIM
    kv0 = A_HEADS

    def kv_spec(which):
        return pl.BlockSpec((None, S, HEAD_DIM), lambda b, g, i: (b, 0, kv0 + which * G + g))

    return pl.pallas_call(
        functools.partial(_slc_win_kernel, tq=tq, tk=tk, wk=wk),
        out_shape=jax.ShapeDtypeStruct((B, S, MIX_WIDTH), BF16),
        grid=(B, G, S // tq),
        in_specs=[pl.BlockSpec((None, tq, gw), lambda b, g, i: (b, i, g)),
                  kv_spec(0), kv_spec(1), kv_spec(2), kv_spec(3),
                  pl.BlockSpec((None, None, tq, LANES), lambda b, g, i: (b, g, i, 0)),
                  pl.BlockSpec((None, tq, gw), lambda b, g, i: (b, i, g)),
                  pl.BlockSpec((None, tq, LANES), lambda b, g, i: (b, i, g))],
        out_specs=pl.BlockSpec((None, tq, gw), lambda b, g, i: (b, i, g)),
        compiler_params=_params("parallel", "parallel", "arbitrary"),
        name="nsa_slc_win",
    )(u3, u3, u3, u3, u3, sel, o_cmp, gate3)


def _pad_cols(w, n):
    return jnp.pad(w, ((0, 0), (0, n - w.shape[1])))


def nsa_layer_mix(x, B, S, norm, w_in, pe_k, w1_k, w2_k, pe_v, w1_v, w2_v, mem_kv):
    H, G, dh = A_HEADS, A_KV, HEAD_DIM
    qw = H * dh
    kvw = 6 * G * dh
    gw = 3 * H
    wg = w_in[:, qw + kvw:qw + kvw + gw].reshape(-1, G, A_HPG, 3).transpose(0, 1, 3, 2)
    wg = jnp.pad(wg.reshape(-1, G, 3 * A_HPG), ((0, 0), (0, 0), (0, LANES - 3 * A_HPG)))
    n_main = qw + 4 * G * dh + MEM_WIDTH
    n_cmp = 2 * G * dh
    w_all = jnp.concatenate([w_in[:, :qw], w_in[:, qw + n_cmp:qw + kvw], w_in[:, qw + kvw + gw:],
                             w_in[:, qw:qw + n_cmp], wg.reshape(-1, G * LANES)], axis=1).astype(BF16)

    u, xc, gate3 = rms_proj(x, norm, w_all, [(0, n_main, BF16, None), (n_main, n_cmp, BF16, dh),
                                             (n_main + n_cmp, G * LANES, F32, None)])
    u = u.reshape(B, S, -1)
    gate3 = gate3.reshape(B, S, -1)
    xc = xc.reshape(2, G, B, S // A_CMP_STRIDE, A_CMP_STRIDE * dh)
    pe = jnp.stack([pe_k.reshape(1, -1), pe_v.reshape(1, -1)])
    kvc = compress(xc, pe, jnp.stack([w1_k, w1_v]).astype(BF16), jnp.stack([w2_k, w2_v]).astype(BF16))
    o_cmp, sel = cmp_select(u, kvc)
    mixed = slc_win(u, sel, o_cmp, gate3)
    mem_out = mem_attention(u, (qw + 4 * G * dh) // MEM_WIDTH, mem_kv)
    return mixed, mem_out


def stick_layer_mix(x, B, S, norm, w_in, mem_kv):
    (u,) = rms_proj(x, norm, w_in.astype(BF16), [(0, w_in.shape[1], BF16, None)])
    u = u.reshape(B, S, -1)
    mixed = stick_breaking(u)
    mem_out = mem_attention(u, 3 * MIX_WIDTH // MEM_WIDTH, mem_kv)
    return mixed, mem_out


def mlstm_layer_mix(x, B, S, norm, w_in, conv_w, conv_b, gate_b, head_norm, mem_kv):
    W = MIX_WIDTH
    nf = 4 * W + 2 * C_HEADS
    n_f = 4 * W + LANES
    w_all = jnp.concatenate([_pad_cols(w_in[:, :nf], n_f), w_in[:, nf:]], axis=1).astype(BF16)
    col_scale = jnp.concatenate([jnp.ones((W,), F32), jnp.full((W,), C_HEAD_DIM ** -0.5, F32)])
    qk, u, um = proj_conv(x, norm, w_all, conv_w, conv_b, col_scale, S, n_f - 2 * W)
    qk = qk.reshape(B, S, -1)
    u = u.reshape(B, S, -1)
    um = um.reshape(B, S, -1)
    mixed = mlstm(qk, u, gate_b, head_norm)
    mem_out = mem_attention(um, 0, mem_kv)
    return mixed, mem_out


def swa_layer_mix(x, B, S, norm, w_in, sinks, mem_kv):
    W = MIX_WIDTH
    kw = D_KV * D_HEAD_DIM
    wk = w_in[:, W:W + kw].reshape(-1, D_KV, D_HEAD_DIM)
    wv = w_in[:, W + kw:W + 2 * kw].reshape(-1, D_KV, D_HEAD_DIM)
    wkv = jnp.concatenate([wk, wv], axis=2).reshape(-1, 2 * kw)
    w_main = jnp.concatenate([w_in[:, :W], w_in[:, W + 2 * kw:], wkv], axis=1)
    w_main = _pad_cols(w_main, W + MEM_WIDTH + 4 * LANES).astype(BF16)
    (u,) = rms_proj(x, norm, w_main, [(0, w_main.shape[1], BF16, None)])
    u = u.reshape(B, S, -1)
    mixed = sink_window(u, sinks)
    mem_out = mem_attention(u, W // MEM_WIDTH, mem_kv)
    return mixed, mem_out


def kernel(x, mem, mem_norm, mem_w_kv, l0_norm_mix, l0_w_in, l0_cmp_pe_k, l0_cmp_w1_k, l0_cmp_w2_k, l0_cmp_pe_v, l0_cmp_w1_v, l0_cmp_w2_v, l0_w_out, l0_norm_ffn, l0_w_gate, l0_w_up, l0_w_down, l1_norm_mix, l1_w_in, l1_w_out, l1_norm_ffn, l1_w_gate, l1_w_up, l1_w_down, l2_norm_mix, l2_w_in, l2_conv_w, l2_conv_b, l2_gate_b, l2_head_norm, l2_w_out, l2_norm_ffn, l2_w_gate, l2_w_up, l2_w_down, l3_norm_mix, l3_w_in, l3_sinks, l3_w_out, l3_norm_ffn, l3_w_gate, l3_w_up, l3_w_down, final_norm):
    B, S, D = x.shape
    M = mem.shape[1]
    (mem_kv,) = rms_proj(mem.reshape(B * M, D), mem_norm, mem_w_kv.astype(BF16),
                         [(0, mem_w_kv.shape[1], BF16, None)])
    mem_kv = mem_kv.reshape(B, M, -1)
    xs = x.reshape(B * S, D)

    def finish(xs, mixed, mem_out, w_out, norm_ffn, ffn_w16, next_f32=None):
        return out_proj_ffn(xs, mixed.reshape(B * S, -1), mem_out.reshape(B * S, -1),
                            w_out.astype(BF16), norm_ffn, *ffn_w16, final_norm,
                            final_norm=next_f32 is None, next_f32=next_f32)

    w16 = (l0_w_gate.astype(BF16), l0_w_up.astype(BF16), l0_w_down.astype(BF16))
    mixed, mem_out = nsa_layer_mix(xs, B, S, l0_norm_mix, l0_w_in, l0_cmp_pe_k, l0_cmp_w1_k, l0_cmp_w2_k,
                                   l0_cmp_pe_v, l0_cmp_w1_v, l0_cmp_w2_v, mem_kv)
    xs, *w16 = finish(xs, mixed, mem_out, l0_w_out, l0_norm_ffn, w16, (l1_w_gate, l1_w_up, l1_w_down))
    mixed, mem_out = stick_layer_mix(xs, B, S, l1_norm_mix, l1_w_in, mem_kv)
    xs, *w16 = finish(xs, mixed, mem_out, l1_w_out, l1_norm_ffn, w16, (l2_w_gate, l2_w_up, l2_w_down))
    mixed, mem_out = mlstm_layer_mix(xs, B, S, l2_norm_mix, l2_w_in, l2_conv_w, l2_conv_b, l2_gate_b,
                                     l2_head_norm, mem_kv)
    xs, *w16 = finish(xs, mixed, mem_out, l2_w_out, l2_norm_ffn, w16, (l3_w_gate, l3_w_up, l3_w_down))
    mixed, mem_out = swa_layer_mix(xs, B, S, l3_norm_mix, l3_w_in, l3_sinks, mem_kv)
    xs = finish(xs, mixed, mem_out, l3_w_out, l3_norm_ffn, w16)
    return xs.reshape(B, S, D)
```

```python
import functools

import jax
import jax.numpy as jnp
from jax import lax
from jax.experimental import pallas as pl
from jax.experimental.pallas import tpu as pltpu

F32 = jnp.float32
BF16 = jnp.bfloat16

LANES = 128
VMEM_LIMIT = 56 * 1024 * 1024

D_MODEL = 2048
HEAD_DIM = 128
MEM_HEADS = 4
MEM_WIDTH = MEM_HEADS * HEAD_DIM
MIX_WIDTH = D_MODEL - MEM_WIDTH
EPS = 1e-6
NEG_INF = -1e30

A_HEADS = MIX_WIDTH // HEAD_DIM
A_KV = 2
A_HPG = A_HEADS // A_KV
A_CMP_LEN = 32
A_CMP_STRIDE = 16
A_SEL_LEN = 64
A_TOP_N = 16
A_WINDOW = 512
A_FORCE = 1e6
A_EXCLUDE = -1e9

B_HEADS = MIX_WIDTH // HEAD_DIM

C_HEADS = 4
C_HEAD_DIM = MIX_WIDTH // C_HEADS
C_CONV = 4
C_CHUNK = 256

D_HEAD_DIM = 64
D_HEADS = MIX_WIDTH // D_HEAD_DIM
D_KV = D_HEADS // 8
D_HPG = D_HEADS // D_KV
D_WINDOW = 128


def _params(*sem):
    return pltpu.CompilerParams(dimension_semantics=sem, vmem_limit_bytes=VMEM_LIMIT)


def _dot(a, b):
    return jnp.dot(a, b, preferred_element_type=F32)


def _dot_nt(a, b):
    return lax.dot_general(a, b, (((1,), (1,)), ((), ())), preferred_element_type=F32)


def _dot_tn(a, b):
    return lax.dot_general(a, b, (((0,), (0,)), ((), ())), preferred_element_type=F32)


def _split2(x):
    hi = x.astype(BF16)
    lo = (x - hi.astype(F32)).astype(BF16)
    return hi, lo


def _split3(x):
    hi = x.astype(BF16)
    r = x - hi.astype(F32)
    mid = r.astype(BF16)
    lo = (r - mid.astype(F32)).astype(BF16)
    return hi, mid, lo


PROJ_CHUNK = 1024


def _rms_proj_kernel(x_ref, g_ref, w_ref, *o_refs, outs):
    x = x_ref[...]
    r = lax.rsqrt(jnp.mean(x * x, axis=-1, keepdims=True) + EPS)
    h = (x * r * g_ref[...]).astype(BF16)
    for o_ref, (start, width, split) in zip(o_refs, outs):
        step = split if split else min(PROJ_CHUNK, width)
        for n, c0 in enumerate(range(0, width, step)):
            cw = min(step, width - c0)
            y = _dot(h, w_ref[:, start + c0:start + c0 + cw]).astype(o_ref.dtype)
            if split:
                o_ref[n] = y
            else:
                o_ref[:, c0:c0 + cw] = y


def rms_proj(x, g, w, outs, *, tm=512):
    T, D = x.shape
    N = w.shape[1]
    tm = min(tm, T)
    shapes, specs = [], []
    for start, width, dtype, split in outs:
        if split:
            shapes.append(jax.ShapeDtypeStruct((width // split, T, split), dtype))
            specs.append(pl.BlockSpec((width // split, tm, split), lambda i: (0, i, 0)))
        else:
            shapes.append(jax.ShapeDtypeStruct((T, width), dtype))
            specs.append(pl.BlockSpec((tm, width), lambda i: (i, 0)))
    return pl.pallas_call(
        functools.partial(_rms_proj_kernel, outs=tuple((s, wd, sp) for s, wd, _, sp in outs)),
        out_shape=tuple(shapes),
        grid=(T // tm,),
        in_specs=[pl.BlockSpec((tm, D), lambda i: (i, 0)),
                  pl.BlockSpec((1, D), lambda i: (0, 0)),
                  pl.BlockSpec((D, N), lambda i: (0, 0), pipeline_mode=pl.Buffered(1))],
        out_specs=tuple(specs),
        compiler_params=_params("parallel"),
        name="rms_proj",
    )(x, g.reshape(1, D), w)


FFN_TF = 512


def interleave_gate_up(w_gate, w_up):
    D, FF = w_gate.shape
    both = jnp.stack([w_gate.astype(BF16).reshape(D, FF // FFN_TF, FFN_TF),
                      w_up.astype(BF16).reshape(D, FF // FFN_TF, FFN_TF)], axis=2)
    return both.reshape(D, 2 * FF)


def _ffn_kernel(x_ref, y1_ref, y2_ref, wo1_ref, wo2_ref, g_ref, wgu_ref, wd_ref, fin_ref, *rest,
                final_norm, cast_next):
    tf = FFN_TF
    if cast_next:
        ng_ref, nu_ref, nd_ref, o_ref, cgu_ref, cd_ref, h_ref = rest
        cgu_ref[:, :tf] = ng_ref[...].astype(BF16)
        cgu_ref[:, tf:] = nu_ref[...].astype(BF16)
    else:
        o_ref, h_ref = rest

    @pl.when(pl.program_id(1) == 0)
    def _():
        x = x_ref[...] + _dot(y1_ref[...], wo1_ref[...]) + _dot(y2_ref[...], wo2_ref[...])
        r = lax.rsqrt(jnp.mean(x * x, axis=-1, keepdims=True) + EPS)
        h_ref[...] = (x * r * g_ref[...]).astype(BF16)
        o_ref[...] = x
        if cast_next:
            cd_ref[...] = nd_ref[...].astype(BF16)

    h = h_ref[...]
    a = _dot(h, wgu_ref[:, :tf])
    u = _dot(h, wgu_ref[:, tf:])
    act = (a * jax.nn.sigmoid(a) * u).astype(BF16)
    o_ref[...] += _dot(act, wd_ref[...])

    if final_norm:
        @pl.when(pl.program_id(1) == pl.num_programs(1) - 1)
        def _():
            y = o_ref[...]
            r = lax.rsqrt(jnp.mean(y * y, axis=-1, keepdims=True) + EPS)
            o_ref[...] = y * r * fin_ref[...]


def out_proj_ffn(x, y1, y2, wo, g, wgu, wd, fin, *, final_norm, next_f32=None, tm=512):
    T, D = x.shape
    K1, K2 = y1.shape[1], y2.shape[1]
    FF = wd.shape[0]
    tf = FFN_TF
    tm = min(tm, T)
    assert FF % tf == 0 and K1 % K2 == 0
    once = pl.Buffered(1)
    n_i = T // tm
    extra_in, extra_out, extra_shape, extra_args = [], [], [], []
    if next_f32 is not None:
        assert D % (16 * n_i) == 0 and FF % (16 * n_i) == 0
        up_spec = pl.BlockSpec((D // n_i, tf), lambda i, f: (i, f))
        down_spec = pl.BlockSpec((FF // n_i, D), lambda i, f: (i, 0))
        extra_in = [up_spec, up_spec, down_spec]
        extra_out = [pl.BlockSpec((D // n_i, 2 * tf), lambda i, f: (i, f)), down_spec]
        extra_shape = [jax.ShapeDtypeStruct((D, 2 * FF), BF16), jax.ShapeDtypeStruct((FF, D), BF16)]
        extra_args = list(next_f32)
    out = pl.pallas_call(
        functools.partial(_ffn_kernel, final_norm=final_norm, cast_next=next_f32 is not None),
        out_shape=[jax.ShapeDtypeStruct((T, D), F32)] + extra_shape,
        grid=(n_i, FF // tf),
        in_specs=[pl.BlockSpec((tm, D), lambda i, f: (i, 0)),
                  pl.BlockSpec((tm, K1), lambda i, f: (i, 0)),
                  pl.BlockSpec((tm, K2), lambda i, f: (i, 0)),
                  pl.BlockSpec((K1, D), lambda i, f: (0, 0), pipeline_mode=once),
                  pl.BlockSpec((K2, D), lambda i, f: (K1 // K2, 0), pipeline_mode=once),
                  pl.BlockSpec((1, D), lambda i, f: (0, 0)),
                  pl.BlockSpec((D, 2 * tf), lambda i, f: (0, f)),
                  pl.BlockSpec((tf, D), lambda i, f: (f, 0)),
                  pl.BlockSpec((1, D), lambda i, f: (0, 0))] + extra_in,
        out_specs=[pl.BlockSpec((tm, D), lambda i, f: (i, 0))] + extra_out,
        scratch_shapes=[pltpu.VMEM((tm, D), BF16)],
        compiler_params=_params("parallel", "arbitrary"),
        name="out_proj_ffn",
    )(x, y1, y2, wo, wo, g.reshape(1, D), wgu, wd, fin.reshape(1, D), *extra_args)
    return out[0] if next_f32 is None else tuple(out)


def _mem_attn_kernel(q_ref, k_ref, v_ref, o_ref):
    scale = HEAD_DIM ** -0.5
    sls = [slice(h * HEAD_DIM, (h + 1) * HEAD_DIM) for h in range(MEM_HEADS)]
    ss = [_dot_nt(q_ref[:, sl], k_ref[:, sl]) * scale for sl in sls]
    es = [jnp.exp(s - jnp.max(s, axis=-1, keepdims=True)) for s in ss]
    ps = [e / jnp.sum(e, axis=-1, keepdims=True) for e in es]
    for sl, p in zip(sls, ps):
        o_ref[:, sl] = _dot(p.astype(BF16), v_ref[:, sl]).astype(o_ref.dtype)


def mem_attention(u3, qblk, mem_kv, *, tq=512):
    B, S, _ = u3.shape
    M = mem_kv.shape[1]
    tq = min(tq, S)
    return pl.pallas_call(
        _mem_attn_kernel,
        out_shape=jax.ShapeDtypeStruct((B, S, MEM_WIDTH), BF16),
        grid=(B, S // tq),
        in_specs=[pl.BlockSpec((None, tq, MEM_WIDTH), lambda b, i: (b, i, qblk)),
                  pl.BlockSpec((None, M, MEM_WIDTH), lambda b, i: (b, 0, 0)),
                  pl.BlockSpec((None, M, MEM_WIDTH), lambda b, i: (b, 0, 1))],
        out_specs=pl.BlockSpec((None, tq, MEM_WIDTH), lambda b, i: (b, i, 0)),
        compiler_params=_params("parallel", "parallel"),
        name="mem_attention",
    )(u3, mem_kv, mem_kv)


LOG2E = 1.4426950408889634
STICK_DEAD_LOG2 = -160.0


def _stick_kernel(q_ref, k_ref, v_ref, o_ref, acc_ref, tail_ref, z0_ref, z1_ref, *, tq, tk):
    i = pl.program_id(2)
    z_refs = (z0_ref, z1_ref)
    c2 = HEAD_DIM ** -0.5 * LOG2E
    per_q = tq // tk
    nh = q_ref.shape[1] // HEAD_DIM
    lrow = lax.broadcasted_iota(jnp.int32, (2 * tk, tk), 0)
    lcol = lax.broadcasted_iota(jnp.int32, (2 * tk, tk), 1)
    later = ((lrow > lcol) & ((lrow < tk) | (lrow > lcol + tk))).astype(BF16)
    rows = lax.broadcasted_iota(jnp.int32, (tq, tk), 0)
    cols = lax.broadcasted_iota(jnp.int32, (tq, tk), 1)
    acc_ref[...] = jnp.zeros_like(acc_ref)
    tail_ref[...] = jnp.zeros_like(tail_ref)

    def scores(j, slot):
        off = pl.multiple_of(j * tk, tk)
        for hh in range(nh):
            sl = slice(hh * HEAD_DIM, (hh + 1) * HEAD_DIM)
            z_refs[slot][hh] = _dot_nt(q_ref[:, sl], k_ref[pl.ds(off, tk), sl])

    def step(j, slot, diag):
        off = pl.multiple_of(j * tk, tk)
        r0 = 0 if diag is None else diag * tk
        if diag is not None:
            causal = (cols + diag * tk < rows)[r0:]
        scores(jnp.maximum(j - 1, 0), 1 - slot)
        part = []
        for hh in range(nh):
            ns = z_refs[slot][hh, r0:, :] * (-c2)
            log_keep = jnp.minimum(ns, 0.0) - jnp.log2(1.0 + jnp.exp2(-jnp.abs(ns)))
            if diag is not None:
                log_keep = jnp.where(causal, log_keep, 0.0)
            hi, lo = _split2(log_keep)
            after = _dot(jnp.concatenate([hi, lo], axis=1), later) + tail_ref[hh, r0:, :]
            part.append((ns, log_keep, after))
        for hh in range(nh):
            sl = slice(hh * HEAD_DIM, (hh + 1) * HEAD_DIM)
            ns, log_keep, after = part[hh]
            a = jnp.exp2(log_keep + after - ns)
            if diag is not None:
                a = jnp.where(causal, a, 0.0)
            acc_ref[hh, r0:, :] += _dot(a.astype(BF16), v_ref[pl.ds(off, tk), sl])
            tail_ref[hh, r0:, :] = after[:, 0:1] + log_keep[:, 0:1]

    assert per_q % 2 == 0
    scores((i + 1) * per_q - 1, 1)
    for d in reversed(range(per_q)):
        step(i * per_q + d, d % 2, d)

    def alive():
        return jnp.max(tail_ref[...]) > STICK_DEAD_LOG2

    def pair(state):
        jj, _ = state
        step(i * per_q - 1 - 2 * jj, 1, None)

        @pl.when(alive())
        def _():
            step(i * per_q - 2 - 2 * jj, 0, None)

        return jj + 1, alive()

    lax.while_loop(lambda s: (s[0] < i * (per_q // 2)) & s[1], pair, (jnp.int32(0), alive()))

    for hh in range(nh):
        o_ref[:, hh * HEAD_DIM:(hh + 1) * HEAD_DIM] = acc_ref[hh].astype(o_ref.dtype)


def stick_breaking(u3, *, tq=512, tk=256, nh=3):
    B, S, _ = u3.shape
    H = B_HEADS // nh
    tq = min(tq, S)
    tk = min(tk, tq)
    hw = nh * HEAD_DIM
    return pl.pallas_call(
        functools.partial(_stick_kernel, tq=tq, tk=tk),
        out_shape=jax.ShapeDtypeStruct((B, S, MIX_WIDTH), BF16),
        grid=(B, H, S // tq),
        in_specs=[pl.BlockSpec((None, tq, hw), lambda b, h, i: (b, i, h)),
                  pl.BlockSpec((None, S, hw), lambda b, h, i: (b, 0, H + h)),
                  pl.BlockSpec((None, S, hw), lambda b, h, i: (b, 0, 2 * H + h))],
        out_specs=pl.BlockSpec((None, tq, hw), lambda b, h, i: (b, i, h)),
        scratch_shapes=[pltpu.VMEM((nh, tq, HEAD_DIM), F32), pltpu.VMEM((nh, tq, 1), F32),
                        pltpu.VMEM((nh, tq, tk), F32), pltpu.VMEM((nh, tq, tk), F32)],
        compiler_params=_params("parallel", "parallel", "arbitrary"),
        name="stick_breaking",
    )(u3, u3, u3)


def _swa_kernel(sink_ref, q_ref, kvp_ref, kvc_ref, o_ref, *, tq):
    g = pl.program_id(1)
    i = pl.program_id(2)
    dh = D_HEAD_DIM
    scale = dh ** -0.5
    kv = jnp.concatenate([kvp_ref[...], kvc_ref[...]], axis=0)
    k = kv[:, :dh]
    v = kv[:, dh:]
    r = lax.broadcasted_iota(jnp.int32, (tq, tq), 0)
    c = lax.broadcasted_iota(jnp.int32, (tq, tq), 1)
    from_prev = c > r
    visible = (c <= r) | (i > 0)
    heads = range(D_HPG)
    sks = [sink_ref[g * D_HPG + hh] for hh in heads]
    raw = [_dot_nt(q_ref[:, hh * dh:(hh + 1) * dh], k) for hh in heads]
    ss = [jnp.where(visible, jnp.where(from_prev, s[:, :tq], s[:, tq:]) * scale, NEG_INF) for s in raw]
    ms = [jnp.maximum(jnp.max(ss[hh], axis=-1, keepdims=True), sks[hh]) for hh in heads]
    es = [jnp.exp(ss[hh] - ms[hh]) for hh in heads]
    ps = [es[hh] / (jnp.sum(es[hh], axis=-1, keepdims=True) + jnp.exp(sks[hh] - ms[hh])) for hh in heads]
    for hh in heads:
        p2 = jnp.concatenate([jnp.where(from_prev, ps[hh], 0.0), jnp.where(from_prev, 0.0, ps[hh])], axis=1)
        o_ref[:, hh * dh:(hh + 1) * dh] = _dot(p2.astype(BF16), v).astype(o_ref.dtype)


def sink_window(u3, sinks):
    B, S, _ = u3.shape
    tq = D_WINDOW
    gw = D_HPG * D_HEAD_DIM
    kv0 = (MIX_WIDTH + MEM_WIDTH) // (2 * D_HEAD_DIM)
    return pl.pallas_call(
        functools.partial(_swa_kernel, tq=tq),
        out_shape=jax.ShapeDtypeStruct((B, S, MIX_WIDTH), BF16),
        grid=(B, D_KV, S // tq),
        in_specs=[pl.BlockSpec(memory_space=pltpu.SMEM),
                  pl.BlockSpec((None, tq, gw), lambda b, g, i: (b, i, g)),
                  pl.BlockSpec((None, tq, 2 * D_HEAD_DIM),
                               lambda b, g, i: (b, jnp.maximum(i - 1, 0), kv0 + g)),
                  pl.BlockSpec((None, tq, 2 * D_HEAD_DIM), lambda b, g, i: (b, i, kv0 + g))],
        out_specs=pl.BlockSpec((None, tq, gw), lambda b, g, i: (b, i, g)),
        compiler_params=_params("parallel", "parallel", "arbitrary"),
        name="sink_window",
    )(sinks, u3, u3, u3)


CONV_HALO = 8
CONV_CHUNK = 512


def _proj_conv_kernel(x_ref, g_ref, w_ref, cw_ref, cb_ref, cs_ref, qk_ref, u_ref, um_ref,
                      halo_ref, buf_ref, *, tiles_per_seq, n_conv, n_f32):
    tm = x_ref.shape[0]
    hl = CONV_HALO
    x = x_ref[...]
    r = lax.rsqrt(jnp.mean(x * x, axis=-1, keepdims=True) + EPS)
    h = (x * r * g_ref[...]).astype(BF16)
    seq_start = pl.program_id(0) % tiles_per_seq == 0

    @pl.when(pl.program_id(0) == 0)
    def _():
        halo_ref[...] = jnp.zeros_like(halo_ref)

    for c0 in range(0, n_conv, CONV_CHUNK):
        sl = slice(c0, c0 + CONV_CHUNK)
        y = _dot(h, w_ref[:, sl])
        prev = halo_ref[:, sl]
        buf_ref[0:hl, :] = jnp.where(seq_start, jnp.zeros_like(prev), prev)
        buf_ref[hl:hl + tm, :] = y
        halo_ref[:, sl] = y[tm - hl:tm, :]
        acc = cb_ref[:, sl] + cw_ref[C_CONV - 1:C_CONV, sl] * y
        for j in range(C_CONV - 1):
            sh = C_CONV - 1 - j
            acc = acc + cw_ref[j:j + 1, sl] * buf_ref[hl - sh:hl - sh + tm, :]
        qk_ref[:, sl] = (acc * jax.nn.sigmoid(acc) * cs_ref[:, sl]).astype(qk_ref.dtype)
    for c0 in range(0, n_f32, PROJ_CHUNK):
        cw = min(PROJ_CHUNK, n_f32 - c0)
        u_ref[:, c0:c0 + cw] = _dot(h, w_ref[:, n_conv + c0:n_conv + c0 + cw])
    um_ref[...] = _dot(h, w_ref[:, n_conv + n_f32:]).astype(um_ref.dtype)


def proj_conv(x, g, w, conv_w, conv_b, col_scale, S, n_f32, *, tm=256):
    T, D = x.shape
    N = w.shape[1]
    n_conv = conv_w.shape[1]
    tm = min(tm, S)
    assert S % tm == 0 and n_conv % CONV_CHUNK == 0 and tm >= CONV_HALO
    n_mem = N - n_conv - n_f32
    row = lambda i: (i, 0)
    fixed = lambda i: (0, 0)
    return pl.pallas_call(
        functools.partial(_proj_conv_kernel, tiles_per_seq=S // tm, n_conv=n_conv, n_f32=n_f32),
        out_shape=(jax.ShapeDtypeStruct((T, n_conv), BF16), jax.ShapeDtypeStruct((T, n_f32), F32),
                   jax.ShapeDtypeStruct((T, n_mem), BF16)),
        grid=(T // tm,),
        in_specs=[pl.BlockSpec((tm, D), row),
                  pl.BlockSpec((1, D), fixed),
                  pl.BlockSpec((D, N), fixed, pipeline_mode=pl.Buffered(1)),
                  pl.BlockSpec((C_CONV, n_conv), fixed),
                  pl.BlockSpec((1, n_conv), fixed),
                  pl.BlockSpec((1, n_conv), fixed)],
        out_specs=(pl.BlockSpec((tm, n_conv), row), pl.BlockSpec((tm, n_f32), row),
                   pl.BlockSpec((tm, n_mem), row)),
        scratch_shapes=[pltpu.VMEM((CONV_HALO, n_conv), F32), pltpu.VMEM((tm + CONV_HALO, CONV_CHUNK), F32)],
        compiler_params=_params("arbitrary"),
        name="proj_conv",
    )(x, g.reshape(1, D), w, conv_w, conv_b.reshape(1, n_conv), col_scale.reshape(1, n_conv))


def _mlstm_kernel(q_ref, k_ref, v_ref, o_ref, gt_ref, gb_ref, hn_ref, out_ref,
                  state_ref, m_ref, *, L):
    H, dh = C_HEADS, C_HEAD_DIM
    ext = dh + LANES

    @pl.when(pl.program_id(1) == 0)
    def _():
        state_ref[...] = jnp.zeros_like(state_ref)
        m_ref[...] = jnp.zeros_like(m_ref)

    rows = lax.broadcasted_iota(jnp.int32, (L, L), 0)
    cols = lax.broadcasted_iota(jnp.int32, (L, L), 1)
    tril = rows >= cols
    eye = rows == cols
    gates = gt_ref[...] + gb_ref[...]
    log_f = jnp.minimum(gates, 0.0) - jnp.log1p(jnp.exp(-jnp.abs(gates)))
    f_hi, f_mid, f_lo = _split3(log_f)
    trilb = tril.astype(BF16)
    bsum = _dot(trilb, f_hi) + _dot(trilb, f_mid) + _dot(trilb, f_lo)
    ones_col = (lax.broadcasted_iota(jnp.int32, (L, LANES), 1) == 0).astype(F32)

    for h in range(H):
        sl = slice(h * dh, (h + 1) * dh)
        q = q_ref[:, sl]
        k = k_ref[:, sl]
        v_ext = jnp.concatenate([v_ref[:, sl], ones_col], axis=1)
        ic = gates[:, h:h + 1]
        bc = bsum[:, H + h:H + h + 1]
        m_prev = m_ref[h][0:1, 0:1]
        rc = ic - bc
        rrow = jnp.sum(jnp.where(eye, rc, 0.0), axis=0, keepdims=True)
        dm = jnp.where(tril, bc + rrow, NEG_INF)
        m_inter = bc + m_prev
        m_t = jnp.maximum(m_inter, jnp.max(dm, axis=-1, keepdims=True))
        w = jnp.exp(dm - m_t)
        a_inter = jnp.exp(m_inter - m_t)
        sqk = _dot_nt(q, k) * w
        state = state_ref[h]
        num = a_inter * _dot(q, state.astype(BF16)) + _dot(sqk.astype(BF16), v_ext.astype(BF16))
        den = jnp.maximum(jnp.abs(num[:, dh:dh + 1]), jnp.exp(-m_t))
        hid = num[:, :dh] / den
        hid = jax.nn.sigmoid(o_ref[:, sl]) * hid
        hid = hid * lax.rsqrt(jnp.mean(hid * hid, axis=-1, keepdims=True) + EPS)
        out_ref[:, sl] = (hid * hn_ref[:, sl]).astype(out_ref.dtype)

        b_last = bc[L - 1:L, :]
        gk = b_last - bc + ic
        m_new = jnp.maximum(b_last + m_prev, jnp.max(gk, axis=0, keepdims=True))
        wk = jnp.exp(gk - m_new)
        decay = jnp.exp(b_last + m_prev - m_new)
        state_ref[h] = decay * state + _dot_tn(k, (wk * v_ext).astype(BF16))
        m_ref[h] = jnp.broadcast_to(m_new, m_ref.shape[1:])


def mlstm(qk3, u3, gate_b, head_norm, *, L=C_CHUNK):
    B, S, _ = u3.shape
    W = MIX_WIDTH
    H, dh = C_HEADS, C_HEAD_DIM
    L = min(L, S)
    gblk = 2 * W // LANES
    gb = jnp.zeros((1, LANES), F32).at[0, :2 * H].set(gate_b)
    return pl.pallas_call(
        functools.partial(_mlstm_kernel, L=L),
        out_shape=jax.ShapeDtypeStruct((B, S, W), BF16),
        grid=(B, S // L),
        in_specs=[pl.BlockSpec((None, L, W), lambda b, c: (b, c, 0)),
                  pl.BlockSpec((None, L, W), lambda b, c: (b, c, 1)),
                  pl.BlockSpec((None, L, W), lambda b, c: (b, c, 0)),
                  pl.BlockSpec((None, L, W), lambda b, c: (b, c, 1)),
                  pl.BlockSpec((None, L, LANES), lambda b, c: (b, c, gblk)),
                  pl.BlockSpec((1, LANES), lambda b, c: (0, 0)),
                  pl.BlockSpec((1, W), lambda b, c: (0, 0))],
        out_specs=pl.BlockSpec((None, L, W), lambda b, c: (b, c, 0)),
        scratch_shapes=[pltpu.VMEM((H, dh, dh + LANES), F32),
                        pltpu.VMEM((H, 8, LANES), F32)],
        compiler_params=_params("parallel", "arbitrary"),
        name="mlstm",
    )(qk3, qk3, u3, u3, u3, gb, head_norm.reshape(1, W))


def _compress_kernel(x_ref, pe_ref, w1_ref, w2_ref, o_ref):
    half = w1_ref.shape[0] // 2
    x = x_ref[...]
    n = x.shape[0]
    first = _dot(x, w1_ref[0:half, :])
    second = _dot(x, w1_ref[half:, :])
    pe = jnp.broadcast_to(pe_ref[...], (8, pe_ref.shape[1])).astype(BF16)
    bias = _dot(pe, w1_ref[...])[0:1, :]
    hid = first + pltpu.roll(second, n - 1, 0) + bias
    o_ref[...] = _dot(jax.nn.gelu(hid).astype(BF16), w2_ref[...]).astype(o_ref.dtype)


def compress(xc, pe, w1, w2):
    _, G, B, n, width = xc.shape
    hid = w1.shape[2]
    return pl.pallas_call(
        _compress_kernel,
        out_shape=jax.ShapeDtypeStruct((2, G, B, n, HEAD_DIM), BF16),
        grid=(2, G, B),
        in_specs=[pl.BlockSpec((None, None, None, n, width), lambda j, g, b: (j, g, b, 0, 0)),
                  pl.BlockSpec((None, 1, 2 * width), lambda j, g, b: (j, 0, 0)),
                  pl.BlockSpec((None, 2 * width, hid), lambda j, g, b: (j, 0, 0)),
                  pl.BlockSpec((None, hid, HEAD_DIM), lambda j, g, b: (j, 0, 0))],
        out_specs=pl.BlockSpec((None, None, None, n, HEAD_DIM), lambda j, g, b: (j, g, b, 0, 0)),
        compiler_params=_params("parallel", "parallel", "parallel"),
        name="nsa_compress",
    )(xc, pe, w1, w2)


def _cmp_select_kernel(q_ref, kc_ref, vc_ref, o_ref, sel_ref, *, tq, n_cmp, n_slc):
    i = pl.program_id(2)
    scale = HEAD_DIM ** -0.5
    npad = kc_ref.shape[0]
    kc = kc_ref[...]
    vc = vc_ref[...]
    pos = i * tq + lax.broadcasted_iota(jnp.int32, (tq, npad), 0)
    nidx = lax.broadcasted_iota(jnp.int32, (tq, npad), 1)
    cmask = (nidx * A_CMP_STRIDE + (A_CMP_LEN - 1) <= pos) & (nidx < n_cmp)
    heads = range(A_HPG)
    ss = [jnp.where(cmask, _dot_nt(q_ref[:, h * HEAD_DIM:(h + 1) * HEAD_DIM], kc) * scale, NEG_INF)
          for h in heads]
    ms = [jnp.max(s, axis=-1, keepdims=True) for s in ss]
    es = [jnp.where(cmask, jnp.exp(ss[h] - ms[h]), 0.0) for h in heads]
    ps = [es[h] / jnp.maximum(jnp.sum(es[h], axis=-1, keepdims=True), 1e-30) for h in heads]
    psum = ps[0]
    for h in heads:
        o_ref[:, h * HEAD_DIM:(h + 1) * HEAD_DIM] = _dot(ps[h].astype(BF16), vc)
        if h:
            psum = psum + ps[h]

    on = lax.broadcasted_iota(jnp.int32, (npad, LANES), 0) * A_CMP_STRIDE
    oj = lax.broadcasted_iota(jnp.int32, (npad, LANES), 1)
    overlap = ((on < (oj + 1) * A_SEL_LEN) & (on + A_CMP_LEN > oj * A_SEL_LEN)
               & (on < n_cmp * A_CMP_STRIDE) & (oj < n_slc)).astype(BF16)
    p_hi, p_lo = _split2(psum)
    imp = _dot(p_hi, overlap) + _dot(p_lo, overlap)

    sub = 8
    blk = lax.broadcasted_iota(jnp.int32, (LANES, tq), 0)
    cur = (i * tq + lax.broadcasted_iota(jnp.int32, (LANES, tq), 1)) // A_SEL_LEN
    valid = blk <= cur
    forced = (blk == 0) | (blk == cur) | (blk == cur - 1)
    score = jnp.where(valid, jnp.where(forced, A_FORCE, imp.T), A_EXCLUDE)
    row8 = lax.broadcasted_iota(jnp.int32, (sub, tq), 0)
    n_top = min(A_TOP_N, n_slc)
    chosen = []
    for v in range(0, n_slc, sub):
        mine = score[v:v + sub, :]
        rank = jnp.zeros((sub, tq), jnp.int32)
        for j in range(n_slc):
            other = score[j:j + 1, :]
            if v > j:
                ahead = other >= mine
            elif v + sub - 1 <= j:
                ahead = other > mine
            else:
                ahead = (other > mine) | ((other == mine) & (row8 + v > j))
            rank = rank + ahead.astype(jnp.int32)
        chosen.append(((rank < n_top) & valid[v:v + sub, :]).astype(F32))
    if n_slc < LANES:
        chosen.append(jnp.zeros((LANES - n_slc, tq), F32))
    sel_ref[...] = jnp.concatenate(chosen, axis=0).T.astype(sel_ref.dtype)


def cmp_select(u3, kvc, *, tq=128):
    B, S, _ = u3.shape
    G = A_KV
    npad = kvc.shape[3]
    n_cmp = (S - A_CMP_LEN) // A_CMP_STRIDE + 1
    n_slc = S // A_SEL_LEN
    tq = min(tq, S)
    gw = A_HPG * HEAD_DIM
    return pl.pallas_call(
        functools.partial(_cmp_select_kernel, tq=tq, n_cmp=n_cmp, n_slc=n_slc),
        out_shape=(jax.ShapeDtypeStruct((B, S, MIX_WIDTH), F32),
                   jax.ShapeDtypeStruct((B, G, S, LANES), BF16)),
        grid=(B, G, S // tq),
        in_specs=[pl.BlockSpec((None, tq, gw), lambda b, g, i: (b, i, g)),
                  pl.BlockSpec((None, None, None, npad, HEAD_DIM), lambda b, g, i: (0, g, b, 0, 0)),
                  pl.BlockSpec((None, None, None, npad, HEAD_DIM), lambda b, g, i: (1, g, b, 0, 0))],
        out_specs=(pl.BlockSpec((None, tq, gw), lambda b, g, i: (b, i, g)),
                   pl.BlockSpec((None, None, tq, LANES), lambda b, g, i: (b, g, i, 0))),
        compiler_params=_params("parallel", "parallel", "parallel"),
        name="nsa_cmp_select",
    )(u3, kvc, kvc)


def _masked_scores(qs, k, mask, scale):
    hp = qs.shape[0] // mask.shape[0]
    s = _dot_nt(qs, k) * scale
    return jnp.where(mask[None], s.reshape((hp,) + mask.shape), NEG_INF).reshape(s.shape)


def _slc_win_kernel(q_ref, ks_ref, vs_ref, kw_ref, vw_ref, sel_ref, oc_ref, gt_ref, o_ref, *, tq, tk, wk):
    i = pl.program_id(2)
    S = ks_ref.shape[0]
    scale = HEAD_DIM ** -0.5 * LOG2E
    hp = A_HPG
    qs = jnp.concatenate([q_ref[:, h * HEAD_DIM:(h + 1) * HEAD_DIM] for h in range(hp)], axis=0)
    sel = sel_ref[...]
    qpos = i * tq + lax.broadcasted_iota(jnp.int32, (tq, tk), 0)
    c = lax.broadcasted_iota(jnp.int32, (tq, tk), 1)
    eb = lax.broadcasted_iota(jnp.int32, (LANES, tk), 0)
    ec = lax.broadcasted_iota(jnp.int32, (LANES, tk), 1) // A_SEL_LEN

    def slc_body(j, carry):
        m, l, acc = carry
        off = pl.multiple_of(j * tk, tk)
        expand = (eb == j * (tk // A_SEL_LEN) + ec).astype(BF16)
        mask = (_dot(sel, expand) > 0.5) & (off + c <= qpos)
        s = _masked_scores(qs, ks_ref[pl.ds(off, tk), :], mask, scale)
        m_new = jnp.maximum(m, jnp.max(s, axis=-1, keepdims=True))
        p = jnp.exp2(s - m_new)
        alpha = jnp.exp2(m - m_new)
        l = alpha * l + jnp.sum(p, axis=-1, keepdims=True)
        acc = alpha * acc + _dot(p.astype(BF16), vs_ref[pl.ds(off, tk), :])
        return m_new, l, acc

    init = (jnp.full((hp * tq, 1), NEG_INF, F32), jnp.zeros((hp * tq, 1), F32),
            jnp.zeros((hp * tq, HEAD_DIM), F32))
    _, l_s, acc_s = lax.fori_loop(0, (i * tq + tq + tk - 1) // tk, slc_body, init)
    o_slc = acc_s / l_s

    start = pl.multiple_of(jnp.clip(i * tq + tq - wk, 0, S - wk), tq)
    wq = i * tq + lax.broadcasted_iota(jnp.int32, (tq, wk), 0)
    rel = wq - (start + lax.broadcasted_iota(jnp.int32, (tq, wk), 1))
    s = _masked_scores(qs, kw_ref[pl.ds(start, wk), :], (rel >= 0) & (rel < A_WINDOW), scale)
    p = jnp.exp2(s - jnp.max(s, axis=-1, keepdims=True))
    o_win = _dot(p.astype(BF16), vw_ref[pl.ds(start, wk), :]) / jnp.sum(p, axis=-1, keepdims=True)

    gate = jax.nn.sigmoid(gt_ref[...])
    for h in range(hp):
        sl = slice(h * HEAD_DIM, (h + 1) * HEAD_DIM)
        rs = slice(h * tq, (h + 1) * tq)
        o = (gate[:, h:h + 1] * oc_ref[:, sl]
             + gate[:, hp + h:hp + h + 1] * o_slc[rs]
             + gate[:, 2 * hp + h:2 * hp + h + 1] * o_win[rs])
        o_ref[:, sl] = o.astype(o_ref.dtype)


def slc_win(u3, sel, o_cmp, gate3, *, tq=128, tk=512):
    B, S, _ = u3.shape
    G = A_KV
    tq = min(tq, S)
    tk = min(tk, S)
    wk = min(A_WINDOW + tq, S)
    assert S % tk == 0 and tk % tq == 0 and S // A_SEL_LEN <= LANES
    gw = A_HPG * HEAD_DIM
    kv0 = A_HEADS

    def kv_spec(which):
        return pl.BlockSpec((None, S, HEAD_DIM), lambda b, g, i: (b, 0, kv0 + which * G + g))

    return pl.pallas_call(
        functools.partial(_slc_win_kernel, tq=tq, tk=tk, wk=wk),
        out_shape=jax.ShapeDtypeStruct((B, S, MIX_WIDTH), BF16),
        grid=(B, G, S // tq),
        in_specs=[pl.BlockSpec((None, tq, gw), lambda b, g, i: (b, i, g)),
                  kv_spec(0), kv_spec(1), kv_spec(2), kv_spec(3),
                  pl.BlockSpec((None, None, tq, LANES), lambda b, g, i: (b, g, i, 0)),
                  pl.BlockSpec((None, tq, gw), lambda b, g, i: (b, i, g)),
                  pl.BlockSpec((None, tq, LANES), lambda b, g, i: (b, i, g))],
        out_specs=pl.BlockSpec((None, tq, gw), lambda b, g, i: (b, i, g)),
        compiler_params=_params("parallel", "parallel", "arbitrary"),
        name="nsa_slc_win",
    )(u3, u3, u3, u3, u3, sel, o_cmp, gate3)


def _pad_cols(w, n):
    return jnp.pad(w, ((0, 0), (0, n - w.shape[1])))


def nsa_layer_mix(x, B, S, norm, w_in, pe_k, w1_k, w2_k, pe_v, w1_v, w2_v, mem_kv):
    H, G, dh = A_HEADS, A_KV, HEAD_DIM
    qw = H * dh
    kvw = 6 * G * dh
    gw = 3 * H
    wg = w_in[:, qw + kvw:qw + kvw + gw].reshape(-1, G, A_HPG, 3).transpose(0, 1, 3, 2)
    wg = jnp.pad(wg.reshape(-1, G, 3 * A_HPG), ((0, 0), (0, 0), (0, LANES - 3 * A_HPG)))
    n_main = qw + 4 * G * dh + MEM_WIDTH
    n_cmp = 2 * G * dh
    w_all = jnp.concatenate([w_in[:, :qw], w_in[:, qw + n_cmp:qw + kvw], w_in[:, qw + kvw + gw:],
                             w_in[:, qw:qw + n_cmp], wg.reshape(-1, G * LANES)], axis=1).astype(BF16)

    u, xc, gate3 = rms_proj(x, norm, w_all, [(0, n_main, BF16, None), (n_main, n_cmp, BF16, dh),
                                             (n_main + n_cmp, G * LANES, F32, None)])
    u = u.reshape(B, S, -1)
    gate3 = gate3.reshape(B, S, -1)
    xc = xc.reshape(2, G, B, S // A_CMP_STRIDE, A_CMP_STRIDE * dh)
    pe = jnp.stack([pe_k.reshape(1, -1), pe_v.reshape(1, -1)])
    kvc = compress(xc, pe, jnp.stack([w1_k, w1_v]).astype(BF16), jnp.stack([w2_k, w2_v]).astype(BF16))
    o_cmp, sel = cmp_select(u, kvc)
    mixed = slc_win(u, sel, o_cmp, gate3)
    mem_out = mem_attention(u, (qw + 4 * G * dh) // MEM_WIDTH, mem_kv)
    return mixed, mem_out


def stick_layer_mix(x, B, S, norm, w_in, mem_kv):
    (u,) = rms_proj(x, norm, w_in.astype(BF16), [(0, w_in.shape[1], BF16, None)])
    u = u.reshape(B, S, -1)
    mixed = stick_breaking(u)
    mem_out = mem_attention(u, 3 * MIX_WIDTH // MEM_WIDTH, mem_kv)
    return mixed, mem_out


def mlstm_layer_mix(x, B, S, norm, w_in, conv_w, conv_b, gate_b, head_norm, mem_kv):
    W = MIX_WIDTH
    nf = 4 * W + 2 * C_HEADS
    n_f = 4 * W + LANES
    w_all = jnp.concatenate([_pad_cols(w_in[:, :nf], n_f), w_in[:, nf:]], axis=1).astype(BF16)
    col_scale = jnp.concatenate([jnp.ones((W,), F32), jnp.full((W,), C_HEAD_DIM ** -0.5, F32)])
    qk, u, um = proj_conv(x, norm, w_all, conv_w, conv_b, col_scale, S, n_f - 2 * W)
    qk = qk.reshape(B, S, -1)
    u = u.reshape(B, S, -1)
    um = um.reshape(B, S, -1)
    mixed = mlstm(qk, u, gate_b, head_norm)
    mem_out = mem_attention(um, 0, mem_kv)
    return mixed, mem_out


def swa_layer_mix(x, B, S, norm, w_in, sinks, mem_kv):
    W = MIX_WIDTH
    kw = D_KV * D_HEAD_DIM
    wk = w_in[:, W:W + kw].reshape(-1, D_KV, D_HEAD_DIM)
    wv = w_in[:, W + kw:W + 2 * kw].reshape(-1, D_KV, D_HEAD_DIM)
    wkv = jnp.concatenate([wk, wv], axis=2).reshape(-1, 2 * kw)
    w_main = jnp.concatenate([w_in[:, :W], w_in[:, W + 2 * kw:], wkv], axis=1)
    w_main = _pad_cols(w_main, W + MEM_WIDTH + 4 * LANES).astype(BF16)
    (u,) = rms_proj(x, norm, w_main, [(0, w_main.shape[1], BF16, None)])
    u = u.reshape(B, S, -1)
    mixed = sink_window(u, sinks)
    mem_out = mem_attention(u, W // MEM_WIDTH, mem_kv)
    return mixed, mem_out


def kernel(x, mem, mem_norm, mem_w_kv, l0_norm_mix, l0_w_in, l0_cmp_pe_k, l0_cmp_w1_k, l0_cmp_w2_k, l0_cmp_pe_v, l0_cmp_w1_v, l0_cmp_w2_v, l0_w_out, l0_norm_ffn, l0_w_gate, l0_w_up, l0_w_down, l1_norm_mix, l1_w_in, l1_w_out, l1_norm_ffn, l1_w_gate, l1_w_up, l1_w_down, l2_norm_mix, l2_w_in, l2_conv_w, l2_conv_b, l2_gate_b, l2_head_norm, l2_w_out, l2_norm_ffn, l2_w_gate, l2_w_up, l2_w_down, l3_norm_mix, l3_w_in, l3_sinks, l3_w_out, l3_norm_ffn, l3_w_gate, l3_w_up, l3_w_down, final_norm):
    B, S, D = x.shape
    M = mem.shape[1]
    (mem_kv,) = rms_proj(mem.reshape(B * M, D), mem_norm, mem_w_kv.astype(BF16),
                         [(0, mem_w_kv.shape[1], BF16, None)])
    mem_kv = mem_kv.reshape(B, M, -1)
    xs = x.reshape(B * S, D)

    def finish(xs, mixed, mem_out, w_out, norm_ffn, ffn_w16, next_f32=None):
        return out_proj_ffn(xs, mixed.reshape(B * S, -1), mem_out.reshape(B * S, -1),
                            w_out.astype(BF16), norm_ffn, *ffn_w16, final_norm,
                            final_norm=next_f32 is None, next_f32=next_f32)

    w16 = (interleave_gate_up(l0_w_gate, l0_w_up), l0_w_down.astype(BF16))
    mixed, mem_out = nsa_layer_mix(xs, B, S, l0_norm_mix, l0_w_in, l0_cmp_pe_k, l0_cmp_w1_k, l0_cmp_w2_k,
                                   l0_cmp_pe_v, l0_cmp_w1_v, l0_cmp_w2_v, mem_kv)
    xs, *w16 = finish(xs, mixed, mem_out, l0_w_out, l0_norm_ffn, w16, (l1_w_gate, l1_w_up, l1_w_down))
    mixed, mem_out = stick_layer_mix(xs, B, S, l1_norm_mix, l1_w_in, mem_kv)
    xs, *w16 = finish(xs, mixed, mem_out, l1_w_out, l1_norm_ffn, w16, (l2_w_gate, l2_w_up, l2_w_down))
    mixed, mem_out = mlstm_layer_mix(xs, B, S, l2_norm_mix, l2_w_in, l2_conv_w, l2_conv_b, l2_gate_b,
                                     l2_head_norm, mem_kv)
    xs, *w16 = finish(xs, mixed, mem_out, l2_w_out, l2_norm_ffn, w16, (l3_w_gate, l3_w_up, l3_w_down))
    mixed, mem_out = swa_layer_mix(xs, B, S, l3_norm_mix, l3_w_in, l3_sinks, mem_kv)
    xs = finish(xs, mixed, mem_out, l3_w_out, l3_norm_ffn, w16)
    return xs.reshape(B, S, D)
```
